```python
import math
import numpy as np
import jax
import jax.numpy as jnp
from jax import lax

D_MODEL = 2048
BATCH = 1
SEQ = 16384
DEPTH = 2

GRID_W = 64
CTX_LEN = 256
CHUNK = 64
NORM_EPS = 1e-6

GLA_HEADS = 4
GLA_DV = D_MODEL // (2 * GLA_HEADS)
GLA_DK = GLA_DV // 2
GLA_RANK = 16
GLA_GATE_NORM = 16.0
GDN_HEADS = 8
GDN_DK = D_MODEL // (2 * GDN_HEADS)
GDN_DV = GDN_DK
GDN_CONV = 4
LRU_WIDTH = D_MODEL // 2
LRU_BLOCKS = 8
LRU_BLOCK = LRU_WIDTH // LRU_BLOCKS
LRU_CONV = 4
LRU_C = 8.0
HY_WIDTH = D_MODEL // 2
HY_SHORT = 3
HY_BANDS = 16
HY_EMB = 1 + 2 * HY_BANDS
HY_HIDDEN = 64
HY_TARGET = 1e-2
HY_FAST = 0.3
HY_SLOW = 1.5

GLA_QK = GLA_HEADS * GLA_DK
GLA_V = GLA_HEADS * GLA_DV
GDN_QK = GDN_HEADS * GDN_DK
GDN_V = GDN_HEADS * GDN_DV
AB_COLS = (GLA_QK, GLA_QK, GLA_V, 2 * GLA_RANK, GLA_V, 2 * GDN_QK + GDN_V, 2 * GDN_HEADS, 2 * GDN_HEADS, GDN_V)
CD_COLS = (LRU_WIDTH, LRU_WIDTH, 3 * HY_WIDTH, HY_WIDTH)
AB_OUT = GLA_V + GDN_V
CD_OUT = LRU_WIDTH + HY_WIDTH

kernel_name = 'hybrid_gla_gdn_rglru_hyena_prefix_dit'


def split_cols(z, sizes):
    return jnp.split(z, np.cumsum(sizes)[:-1].tolist(), axis=-1)


def rms_norm(x, g):
    x32 = x.astype(jnp.float32)
    y = x32 * lax.rsqrt(jnp.mean(x32 * x32, axis=-1, keepdims=True) + NORM_EPS)
    return (y * g.astype(jnp.float32)).astype(x.dtype)


def l2_normalize(x):
    x32 = x.astype(jnp.float32)
    return x32 * lax.rsqrt(jnp.sum(x32 * x32, axis=-1, keepdims=True) + NORM_EPS)


def centred_dwconv(x, w, b=None):
    width = w.shape[0]
    left = (width - 1) // 2
    y = lax.conv_general_dilated(x, w[:, None, :].astype(x.dtype), (1,), [(left, width - 1 - left)],
                                 dimension_numbers=('NWC', 'WIO', 'NWC'), feature_group_count=x.shape[-1])
    return y if b is None else y + b


def raster_to_columns(x):
    b, t, ch = x.shape
    rows = t // GRID_W
    return x.reshape(b, rows, GRID_W, ch).transpose(0, 2, 1, 3).reshape(b, t, ch)


def columns_to_raster(x):
    b, t, ch = x.shape
    rows = t // GRID_W
    return x.reshape(b, GRID_W, rows, ch).transpose(0, 2, 1, 3).reshape(b, t, ch)


def bidir_with_context_prefix(scan_fn, ctx_dirs, lat_dirs, state0):
    flip = lambda args: tuple(a[:, ::-1] for a in args)
    out_c_f, st_f = scan_fn(ctx_dirs[0], state0)
    out_l_f, _ = scan_fn(lat_dirs[0], st_f)
    out_c_b, st_b = scan_fn(flip(ctx_dirs[1]), state0)
    out_l_b, _ = scan_fn(flip(lat_dirs[1]), st_b)
    return out_c_f + out_c_b[:, ::-1], out_l_f + out_l_b[:, ::-1]


def gla_chunked(q, k, v, g, s0):
    b, t, h, dk = q.shape
    dv = v.shape[-1]
    n = t // CHUNK
    f32 = jnp.float32
    q, k, g = (a.astype(f32).reshape(b, n, CHUNK, h, dk) for a in (q, k, g))
    v = v.astype(f32).reshape(b, n, CHUNK, h, dv)
    cum = jnp.cumsum(g, axis=2)
    last = cum[:, :, -1:]
    q_dec = q * jnp.exp(cum)
    k_inv = k * jnp.exp(-cum)
    k_end = k * jnp.exp(last - cum)
    lower = jnp.tril(jnp.ones((CHUNK, CHUNK), bool))
    scores = jnp.where(lower, jnp.einsum('bnihk,bnjhk->bnhij', q_dec, k_inv), 0.0)
    o_intra = jnp.einsum('bnhij,bnjhv->bnihv', scores, v)
    chunk_kv = jnp.einsum('bnchk,bnchv->bnhkv', k_end, v)
    chunk_decay = jnp.exp(last[:, :, 0])

    def step(state, inp):
        dec, kv = inp
        return dec[..., None] * state + kv, state

    s_fin, s_in = lax.scan(step, s0, (jnp.moveaxis(chunk_decay, 1, 0), jnp.moveaxis(chunk_kv, 1, 0)))
    o_inter = jnp.einsum('bnchk,nbhkv->bnchv', q_dec, s_in)
    return (o_intra + o_inter).reshape(b, t, h, dv), s_fin


def gdn_chunked(q, k, v, log_a, beta, s0):
    b, t, h, dk = q.shape
    dv = v.shape[-1]
    n = t // CHUNK
    f32 = jnp.float32
    q, k = (a.astype(f32).reshape(b, n, CHUNK, h, dk) for a in (q, k))
    v = v.astype(f32).reshape(b, n, CHUNK, h, dv)
    log_a = log_a.astype(f32).reshape(b, n, CHUNK, h)
    beta = beta.astype(f32).reshape(b, n, CHUNK, h)
    cum = jnp.cumsum(log_a, axis=2)
    cum_h = jnp.moveaxis(cum, 2, 3)
    lower = jnp.tril(jnp.ones((CHUNK, CHUNK), bool))
    strict = jnp.tril(jnp.ones((CHUNK, CHUNK), bool), -1)
    decay = jnp.exp(jnp.where(lower, cum_h[..., :, None] - cum_h[..., None, :], -jnp.inf))
    beta_h = jnp.moveaxis(beta, 2, 3)
    m = jnp.where(strict, jnp.einsum('bnihk,bnjhk->bnhij', k, k) * decay * beta_h[..., :, None], 0.0)
    eye = jnp.eye(CHUNK, dtype=f32)
    t_inv = lax.linalg.triangular_solve(eye + m, jnp.broadcast_to(eye, m.shape), left_side=True,
                                        lower=True, unit_diagonal=True)
    u = jnp.einsum('bnhij,bnjhv->bnihv', t_inv, v * beta[..., None])
    w = jnp.einsum('bnhij,bnjhk->bnihk', t_inv, k * (beta * jnp.exp(cum))[..., None])
    q_dec = q * jnp.exp(cum)[..., None]
    scores = jnp.einsum('bnihk,bnjhk->bnhij', q, k) * decay
    k_end = k * jnp.exp(cum[:, :, -1:] - cum)[..., None]
    chunk_decay = jnp.exp(cum[:, :, -1])

    def step(state, inp):
        u_c, w_c, q_c, s_c, k_c, d_c = inp
        v_new = u_c - jnp.einsum('bchk,bhkv->bchv', w_c, state)
        o_c = jnp.einsum('bchk,bhkv->bchv', q_c, state) + jnp.einsum('bhij,bjhv->bihv', s_c, v_new)
        state = d_c[..., None, None] * state + jnp.einsum('bchk,bchv->bhkv', k_c, v_new)
        return state, o_c

    xs = tuple(jnp.moveaxis(a, 1, 0) for a in (u, w, q_dec, scores, k_end, chunk_decay))
    s_fin, o = lax.scan(step, s0, xs)
    return jnp.moveaxis(o, 0, 1).reshape(b, t, h, dv), s_fin


def lru_scan(log_a, gated_x, h0):
    log_a = log_a.astype(jnp.float32)
    a = jnp.exp(log_a)
    drive = jnp.sqrt(-jnp.expm1(2.0 * log_a)) * gated_x.astype(jnp.float32)
    drive = drive.at[:, 0].add(a[:, 0] * h0)

    def combine(left, right):
        a_l, b_l = left
        a_r, b_r = right
        return a_l * a_r, a_r * b_l + b_r

    _, h = lax.associative_scan(combine, (a, drive), axis=1)
    return h, h[:, -1]


def hyena_filter_spectrum(length, w1, b1, w2, b2, w3, freq, decay):
    f32 = jnp.float32
    pos = jnp.arange(length, dtype=f32)
    t01 = pos / max(length - 1, 1)
    bands = jnp.linspace(1e-4, HY_BANDS - 1, HY_BANDS, dtype=f32)
    ang = (2.0 * math.pi / length) * pos[:, None] * bands[None, :]
    z = jnp.concatenate([t01[:, None], jnp.cos(ang), jnp.sin(ang)], axis=-1)
    hid = jnp.sin(freq * (z @ w1 + b1))
    hid = jnp.sin(freq * (hid @ w2 + b2))
    filt = (hid @ w3).astype(f32).reshape(length, 2, HY_WIDTH)
    filt = filt * jnp.exp(-t01[:, None, None] * decay.astype(f32)[None])
    two_sided = jnp.concatenate([filt[:, 0], jnp.zeros((1, HY_WIDTH), f32), filt[:0:-1, 1]], axis=0)
    two_sided = two_sided * lax.rsqrt(jnp.sum(two_sided * two_sided, axis=0, keepdims=True) + NORM_EPS)
    return jnp.fft.rfft(two_sided, axis=0)


def hyena_mix(proj, conv_w, conv_b, bias, spectrum):
    f32 = jnp.float32
    length = proj.shape[1]
    x0, x1, v = split_cols(centred_dwconv(proj, conv_w, conv_b).astype(f32), (HY_WIDTH, HY_WIDTH, HY_WIDTH))
    z = x1 * v
    y_conv = jnp.fft.irfft(jnp.fft.rfft(z, n=2 * length, axis=1) * spectrum, n=2 * length, axis=1)[:, :length]
    return x0 * (y_conv + z * bias.astype(f32))


def mixer_ab(h_lat, h_ctx, w_in, gla_wg2, gla_bg2, gla_norm, gdn_conv, gdn_a_log, gdn_dt_bias, gdn_norm,
             w_out, with_ctx):
    f32 = jnp.float32

    def branch_inputs(h):
        bsz, t, _ = h.shape
        gq, gk, gv, glr, gz, dqkv, da, db, dz = split_cols(h @ w_in, AB_COLS)
        q = gq.reshape(bsz, t, GLA_HEADS, GLA_DK) * (GLA_DK ** -0.5)
        k = gk.reshape(bsz, t, GLA_HEADS, GLA_DK)
        v = gv.reshape(bsz, t, GLA_HEADS, GLA_DV)
        g_logit = jnp.einsum('btdr,drk->btdk', glr.reshape(bsz, t, 2, GLA_RANK), gla_wg2) + gla_bg2
        g = (jax.nn.log_sigmoid(g_logit.astype(f32)) / GLA_GATE_NORM).reshape(bsz, t, 2, GLA_HEADS, GLA_DK)
        dq, dk, dv = split_cols(jax.nn.silu(centred_dwconv(dqkv, gdn_conv)), (GDN_QK, GDN_QK, GDN_V))
        dq = l2_normalize(dq.reshape(bsz, t, GDN_HEADS, GDN_DK)) * (GDN_DK ** -0.5)
        dk = l2_normalize(dk.reshape(bsz, t, GDN_HEADS, GDN_DK))
        dv = dv.reshape(bsz, t, GDN_HEADS, GDN_DV)
        log_a = -jnp.exp(gdn_a_log.astype(f32)) * jax.nn.softplus(
            da.reshape(bsz, t, 2, GDN_HEADS).astype(f32) + gdn_dt_bias.astype(f32))
        beta = jax.nn.sigmoid(db.reshape(bsz, t, 2, GDN_HEADS).astype(f32))
        gla = tuple((q, k, v, g[:, :, d]) for d in range(2))
        gdn = tuple((dq, dk, dv, log_a[:, :, d], beta[:, :, d]) for d in range(2))
        return gla, gdn, gz, dz

    gla_c, gdn_c, gz_c, dz_c = branch_inputs(h_ctx)
    gla_l, gdn_l, gz_l, dz_l = branch_inputs(h_lat)
    bsz = h_lat.shape[0]
    gla_ctx_o, gla_lat_o = bidir_with_context_prefix(
        lambda a, s: gla_chunked(*a, s), gla_c, gla_l, jnp.zeros((bsz, GLA_HEADS, GLA_DK, GLA_DV), f32))
    gdn_ctx_o, gdn_lat_o = bidir_with_context_prefix(
        lambda a, s: gdn_chunked(*a, s), gdn_c, gdn_l, jnp.zeros((bsz, GDN_HEADS, GDN_DK, GDN_DV), f32))

    def merge(gla_o, gdn_o, gz, dz):
        b_, t = gz.shape[:2]
        y_gla = rms_norm(gla_o, gla_norm).reshape(b_, t, GLA_V).astype(gz.dtype) * jax.nn.silu(gz)
        y_gdn = rms_norm(gdn_o, gdn_norm).reshape(b_, t, GDN_V).astype(dz.dtype) * jax.nn.silu(dz)
        return jnp.concatenate([y_gla, y_gdn], axis=-1) @ w_out

    y_lat = merge(gla_lat_o, gdn_lat_o, gz_l, dz_l)
    y_ctx = merge(gla_ctx_o, gdn_ctx_o, gz_c, dz_c) if with_ctx else None
    return y_lat, y_ctx


def mixer_cd(h_lat, h_ctx, w_in, lru_conv_w, lru_conv_b, lru_wa, lru_ba, lru_wx, lru_bx, lru_lambda,
             hy_conv_w, hy_conv_b, hy_w1, hy_b1, hy_w2, hy_b2, hy_w3, hy_freq, hy_decay, hy_bias, w_out, with_ctx):
    f32 = jnp.float32
    lx_c, lz_c, hp_c, hz_c = split_cols(h_ctx @ w_in, CD_COLS)
    lx_l, lz_l, hp_l, hz_l = split_cols(h_lat @ w_in, CD_COLS)

    def lru_inputs(xs):
        bsz, t, _ = xs.shape
        xc = centred_dwconv(xs, lru_conv_w, lru_conv_b).astype(f32)
        xb = xc.reshape(bsz, t, LRU_BLOCKS, LRU_BLOCK)
        r = jax.nn.sigmoid(jnp.einsum('bthi,dhij->dbthj', xb, lru_wa.astype(f32)).reshape(2, bsz, t, LRU_WIDTH)
                           + lru_ba.astype(f32)[:, None, None])
        i = jax.nn.sigmoid(jnp.einsum('bthi,dhij->dbthj', xb, lru_wx.astype(f32)).reshape(2, bsz, t, LRU_WIDTH)
                           + lru_bx.astype(f32)[:, None, None])
        log_a = -LRU_C * jax.nn.softplus(-lru_lambda.astype(f32))[:, None, None] * r
        gated = i * xc
        return tuple((log_a[d], gated[d]) for d in range(2))

    bsz = h_lat.shape[0]
    lru_ctx_o, lru_lat_cols = bidir_with_context_prefix(
        lambda a, s: lru_scan(*a, s), lru_inputs(lx_c), lru_inputs(raster_to_columns(lx_l)),
        jnp.zeros((bsz, LRU_WIDTH), f32))
    lru_lat_o = columns_to_raster(lru_lat_cols)

    def hyena(proj):
        spectrum = hyena_filter_spectrum(proj.shape[1], hy_w1, hy_b1, hy_w2, hy_b2, hy_w3, hy_freq, hy_decay)
        return hyena_mix(proj, hy_conv_w, hy_conv_b, hy_bias, spectrum)

    def merge(lru_o, hy_o, lz, hz):
        y_lru = lru_o.astype(lz.dtype) * jax.nn.silu(lz)
        y_hy = hy_o.astype(hz.dtype) * jax.nn.silu(hz)
        return jnp.concatenate([y_lru, y_hy], axis=-1) @ w_out

    y_lat = merge(lru_lat_o, hyena(hp_l), lz_l, hz_l)
    y_ctx = merge(lru_ctx_o, hyena(hp_c), lz_c, hz_c) if with_ctx else None
    return y_lat, y_ctx


def setup_inputs(seed: int = 0) -> dict:
    key = jax.random.key(seed)
    keys = iter(jax.random.split(key, 48))
    f32 = jnp.float32

    def normal(shape, scale):
        return scale * jax.random.normal(next(keys), shape, f32)

    def uniform(shape, lo, hi):
        return jax.random.uniform(next(keys), shape, f32, lo, hi)

    d = D_MODEL
    n_ab = (DEPTH + 1) // 2
    n_cd = DEPTH // 2
    ab_in = sum(AB_COLS)
    cd_in = sum(CD_COLS)
    dt = jnp.exp(uniform((n_ab, 2, GDN_HEADS), math.log(1e-3), math.log(1e-1)))
    a_pow = uniform((n_cd, 2, LRU_WIDTH), 0.9, 0.999) ** (1.0 / LRU_C)
    rates = jnp.linspace(-math.log(HY_TARGET) / HY_SLOW, -math.log(HY_TARGET) / HY_FAST, HY_WIDTH, dtype=f32)
    return {
        'x': normal((BATCH, SEQ, d), 1.0),
        'c': normal((BATCH, d), 1.0),
        'ctx': normal((BATCH, CTX_LEN, d), 1.0),
        'c_ctx': normal((d,), 1.0),
        'mod_w': normal((DEPTH, d, 3 * d), d ** -0.5),
        'mod_b': normal((DEPTH, 3 * d), 0.02),
        'norm_g': 1.0 + normal((DEPTH, d), 0.02),
        'ab_w_in': normal((n_ab, d, ab_in), d ** -0.5),
        'ab_gla_wg2': normal((n_ab, 2, GLA_RANK, GLA_QK), GLA_RANK ** -0.5),
        'ab_gla_bg2': normal((n_ab, 2, GLA_QK), 0.1),
        'ab_gla_norm': 1.0 + normal((n_ab, GLA_DV), 0.02),
        'ab_gdn_conv': normal((n_ab, GDN_CONV, 2 * GDN_QK + GDN_V), GDN_CONV ** -0.5),
        'ab_gdn_a_log': jnp.log(uniform((n_ab, 2, GDN_HEADS), 1.0, 16.0)),
        'ab_gdn_dt_bias': dt + jnp.log(-jnp.expm1(-dt)),
        'ab_gdn_norm': 1.0 + normal((n_ab, GDN_DV), 0.02),
        'ab_w_out': normal((n_ab, AB_OUT, d), AB_OUT ** -0.5),
        'cd_w_in': normal((n_cd, d, cd_in), d ** -0.5),
        'cd_lru_conv_w': normal((n_cd, LRU_CONV, LRU_WIDTH), LRU_CONV ** -0.5),
        'cd_lru_conv_b': normal((n_cd, LRU_WIDTH), 0.02),
        'cd_lru_wa': normal((n_cd, 2, LRU_BLOCKS, LRU_BLOCK, LRU_BLOCK), LRU_BLOCK ** -0.5),
        'cd_lru_ba': normal((n_cd, 2, LRU_WIDTH), 0.02),
        'cd_lru_wx': normal((n_cd, 2, LRU_BLOCKS, LRU_BLOCK, LRU_BLOCK), LRU_BLOCK ** -0.5),
        'cd_lru_bx': normal((n_cd, 2, LRU_WIDTH), 0.02),
        'cd_lru_lambda': jnp.log(a_pow) - jnp.log1p(-a_pow),
        'cd_hy_conv_w': normal((n_cd, HY_SHORT, 3 * HY_WIDTH), HY_SHORT ** -0.5),
        'cd_hy_conv_b': normal((n_cd, 3 * HY_WIDTH), 0.02),
        'cd_hy_w1': normal((n_cd, HY_EMB, HY_HIDDEN), HY_EMB ** -0.5),
        'cd_hy_b1': normal((n_cd, HY_HIDDEN), 0.02),
        'cd_hy_w2': normal((n_cd, HY_HIDDEN, HY_HIDDEN), HY_HIDDEN ** -0.5),
        'cd_hy_b2': normal((n_cd, HY_HIDDEN), 0.02),
        'cd_hy_w3': normal((n_cd, HY_HIDDEN, 2 * HY_WIDTH), HY_HIDDEN ** -0.5),
        'cd_hy_freq': 1.0 + normal((n_cd, HY_HIDDEN), 0.02),
        'cd_hy_decay': rates * (1.0 + normal((n_cd, 2, HY_WIDTH), 0.05)),
        'cd_hy_bias': normal((n_cd, HY_WIDTH), 1.0),
        'cd_w_out': normal((n_cd, CD_OUT, d), CD_OUT ** -0.5),
        'final_g': 1.0 + normal((d,), 0.02),
    }


def reference(x, c, ctx, c_ctx, mod_w, mod_b, norm_g,
              ab_w_in, ab_gla_wg2, ab_gla_bg2, ab_gla_norm, ab_gdn_conv, ab_gdn_a_log, ab_gdn_dt_bias,
              ab_gdn_norm, ab_w_out,
              cd_w_in, cd_lru_conv_w, cd_lru_conv_b, cd_lru_wa, cd_lru_ba, cd_lru_wx, cd_lru_bx, cd_lru_lambda,
              cd_hy_conv_w, cd_hy_conv_b, cd_hy_w1, cd_hy_b1, cd_hy_w2, cd_hy_b2, cd_hy_w3, cd_hy_freq,
              cd_hy_decay, cd_hy_bias, cd_w_out,
              final_g):
    x_lat, x_ctx = x, ctx
    for layer in range(DEPTH):
        with_ctx = layer < DEPTH - 1
        j = layer // 2
        m_lat = jax.nn.silu(c) @ mod_w[layer] + mod_b[layer]
        m_ctx = jax.nn.silu(c_ctx) @ mod_w[layer] + mod_b[layer]
        shift_l, scale_l, gate_l = jnp.split(m_lat[:, None, :], 3, axis=-1)
        shift_c, scale_c, gate_c = jnp.split(m_ctx[None, None, :], 3, axis=-1)
        h_lat = rms_norm(x_lat, norm_g[layer]) * (1.0 + scale_l) + shift_l
        h_ctx = rms_norm(x_ctx, norm_g[layer]) * (1.0 + scale_c) + shift_c
        if layer % 2 == 0:
            y_lat, y_ctx = mixer_ab(h_lat, h_ctx, ab_w_in[j], ab_gla_wg2[j], ab_gla_bg2[j], ab_gla_norm[j],
                                    ab_gdn_conv[j], ab_gdn_a_log[j], ab_gdn_dt_bias[j], ab_gdn_norm[j],
                                    ab_w_out[j], with_ctx)
        else:
            y_lat, y_ctx = mixer_cd(h_lat, h_ctx, cd_w_in[j], cd_lru_conv_w[j], cd_lru_conv_b[j], cd_lru_wa[j],
                                    cd_lru_ba[j], cd_lru_wx[j], cd_lru_bx[j], cd_lru_lambda[j],
                                    cd_hy_conv_w[j], cd_hy_conv_b[j], cd_hy_w1[j], cd_hy_b1[j], cd_hy_w2[j],
                                    cd_hy_b2[j], cd_hy_w3[j], cd_hy_freq[j], cd_hy_decay[j], cd_hy_bias[j],
                                    cd_w_out[j], with_ctx)
        x_lat = x_lat + gate_l * y_lat
        if with_ctx:
            x_ctx = x_ctx + gate_c * y_ctx
    return rms_norm(x_lat, final_g)
```

```python
import functools
import math

import numpy as np
import jax
import jax.numpy as jnp
from jax import lax
from jax.experimental import pallas as pl
from jax.experimental.pallas import tpu as pltpu

F32 = jnp.float32
BF16 = jnp.bfloat16
HI = lax.Precision.HIGHEST

NORM_EPS = 1e-6
CHUNK = 64
ROW_TILE = 256
GRID_W = 64
LANE = 128
GLA_HEADS, GLA_DK, GLA_DV, GLA_RANK, GLA_GATE_NORM = 4, 128, 256, 16, 16.0
GDN_HEADS, GDN_DK = 8, 128
LRU_BLOCKS, LRU_C = 8, 8.0
HY_BANDS = 16
VMEM_LIMIT = 56 * 1024 * 1024


def _cparams(n_axes):
    return pltpu.CompilerParams(dimension_semantics=("arbitrary",) * n_axes,
                                vmem_limit_bytes=VMEM_LIMIT)


def _silu(x):
    return x * jax.nn.sigmoid(x)


def _softplus(x):
    return jnp.maximum(x, 0.0) + jnp.log1p(jnp.exp(-jnp.abs(x)))


def _dot(a, b, precision=None):
    return jnp.dot(a, b, precision=precision, preferred_element_type=F32)


def _dot_nt(a, b, precision=None):
    return lax.dot_general(a, b, (((1,), (1,)), ((), ())), precision=precision,
                           preferred_element_type=F32)


def _dot_tn(a, b, precision=None):
    return lax.dot_general(a, b, (((0,), (0,)), ((), ())), precision=precision,
                           preferred_element_type=F32)


def _mod_kernel(c_ref, w_ref, b_ref, o_ref):
    o_ref[...] = _dot(_silu(c_ref[...]), w_ref[...], HI) + b_ref[...]


def _mod_call(cvec, mod_w, mod_b):
    depth, d, n = mod_w.shape
    tn = 512
    return pl.pallas_call(
        _mod_kernel, grid=(depth, n // tn),
        in_specs=[pl.BlockSpec((8, d), lambda l, j: (0, 0)),
                  pl.BlockSpec((None, d, tn), lambda l, j: (l, 0, j)),
                  pl.BlockSpec((None, 1, tn), lambda l, j: (l, 0, j))],
        out_specs=pl.BlockSpec((None, 8, tn), lambda l, j: (l, 0, j)),
        out_shape=jax.ShapeDtypeStruct((depth, 8, n), F32),
        compiler_params=_cparams(2), name="adaln_mod",
    )(cvec, mod_w, mod_b.reshape(depth, 1, n))


def _norm_mod_kernel(*refs, two_inputs, n_lat_tiles, d):
    if two_inputs:
        x_ref, c_ref, m_ref, g_ref, o_ref = refs
    else:
        x_ref, m_ref, g_ref, o_ref = refs
    is_ctx = pl.program_id(0) == n_lat_tiles
    xt = x_ref[...]
    if two_inputs:
        xt = jnp.where(is_ctx, c_ref[...], xt)
    y = xt * lax.rsqrt(jnp.mean(xt * xt, axis=-1, keepdims=True) + NORM_EPS) * g_ref[...]
    m = m_ref[...]
    shift = jnp.where(is_ctx, m[1:2, :d], m[0:1, :d])
    scale = jnp.where(is_ctx, m[1:2, d:2 * d], m[0:1, d:2 * d])
    o_ref[...] = (y * (1.0 + scale) + shift).astype(BF16)


def _norm_mod_call(x_lat, x_ctx, mods, g, n_lat_tiles):
    d = x_lat.shape[1]
    nt = n_lat_tiles + 1
    two = x_ctx is not None
    in_specs = [pl.BlockSpec((ROW_TILE, d), (lambda i: (jnp.minimum(i, n_lat_tiles - 1), 0)) if two
                             else (lambda i: (i, 0)))]
    args = [x_lat]
    if two:
        in_specs.append(pl.BlockSpec((ROW_TILE, d), lambda i: (0, 0)))
        args.append(x_ctx)
    in_specs += [pl.BlockSpec((8, 3 * d), lambda i: (0, 0)), pl.BlockSpec((1, d), lambda i: (0, 0))]
    args += [mods, g.reshape(1, d)]
    return pl.pallas_call(
        functools.partial(_norm_mod_kernel, two_inputs=two, n_lat_tiles=n_lat_tiles, d=d),
        grid=(nt,), in_specs=in_specs,
        out_specs=pl.BlockSpec((ROW_TILE, d), lambda i: (i, 0)),
        out_shape=jax.ShapeDtypeStruct((nt * ROW_TILE, d), BF16),
        compiler_params=_cparams(1), name="norm_mod",
    )(*args)


def _mm_kernel(a_ref, w_ref, o_ref):
    o_ref[...] = _dot(a_ref[...], w_ref[...])


def _row_tile(n_tiles, max_mult):
    k = max(m for m in range(1, max_mult + 1) if n_tiles % m == 0)
    return k * ROW_TILE


def _mm_call(a, w, tn, name):
    m, k = a.shape
    n = w.shape[1]
    tm = _row_tile(m // ROW_TILE, 5)
    return pl.pallas_call(
        _mm_kernel, grid=(n // tn, m // tm),
        in_specs=[pl.BlockSpec((tm, k), lambda j, i: (i, 0)),
                  pl.BlockSpec((k, tn), lambda j, i: (0, j))],
        out_specs=pl.BlockSpec((tm, tn), lambda j, i: (i, j)),
        out_shape=jax.ShapeDtypeStruct((m, n), F32),
        compiler_params=_cparams(2), name=name,
    )(a, w)


def _row_shift(x, k, row, prev_row, next0, next1):
    n = x.shape[0]
    if k == -1:
        return jnp.where(row == 0, prev_row, pltpu.roll(x, 1, 0))
    if k == 1:
        return jnp.where(row == n - 1, next0, pltpu.roll(x, n - 1, 0))
    r = pltpu.roll(x, n - 2, 0)
    return jnp.where(row == n - 2, next0, jnp.where(row == n - 1, next1, r))


def _gdn_prep_kernel(x_ref, p_ref, n_ref, w_ref, o_ref, *, n_lat_tiles):
    i = pl.program_id(0)
    has_prev = jnp.logical_and(i != 0, i != n_lat_tiles)
    has_next = i < n_lat_tiles - 1
    x = x_ref[...]
    row = lax.broadcasted_iota(jnp.int32, x.shape, 0)
    prev_row = jnp.where(has_prev, p_ref[7:8, :], 0.0)
    next0 = jnp.where(has_next, n_ref[0:1, :], 0.0)
    next1 = jnp.where(has_next, n_ref[1:2, :], 0.0)
    w = w_ref[...]
    y = (w[0:1] * _row_shift(x, -1, row, prev_row, next0, next1) + w[1:2] * x
         + w[2:3] * _row_shift(x, 1, row, prev_row, next0, next1)
         + w[3:4] * _row_shift(x, 2, row, prev_row, next0, next1))
    y = _silu(y)
    qk = GDN_HEADS * GDN_DK
    for h in range(2 * GDN_HEADS):
        seg = y[:, h * GDN_DK:(h + 1) * GDN_DK]
        seg = seg * lax.rsqrt(jnp.sum(seg * seg, axis=-1, keepdims=True) + NORM_EPS)
        if h < GDN_HEADS:
            seg = seg * (GDN_DK ** -0.5)
        o_ref[:, h * GDN_DK:(h + 1) * GDN_DK] = seg
    o_ref[:, 2 * qk:] = y[:, 2 * qk:]


def _gdn_prep_call(zab, conv_w, n_lat_tiles):
    nt = n_lat_tiles + 1
    w = 3 * GDN_HEADS * GDN_DK
    hb = ROW_TILE // 8
    last8 = nt * hb - 1
    return pl.pallas_call(
        functools.partial(_gdn_prep_kernel, n_lat_tiles=n_lat_tiles), grid=(nt,),
        in_specs=[pl.BlockSpec((ROW_TILE, w), lambda i: (i, 1)),
                  pl.BlockSpec((8, w), lambda i: (jnp.maximum(i * hb - 1, 0), 1)),
                  pl.BlockSpec((8, w), lambda i: (jnp.minimum((i + 1) * hb, last8), 1)),
                  pl.BlockSpec((4, w), lambda i: (0, 0))],
        out_specs=pl.BlockSpec((ROW_TILE, w), lambda i: (i, 0)),
        out_shape=jax.ShapeDtypeStruct((nt * ROW_TILE, w), F32),
        compiler_params=_cparams(1), name="gdn_prep",
    )(zab, zab, zab, conv_w)


def _tile_fwd(s, n_lat_tiles):
    return jnp.where(s == 0, n_lat_tiles, s - 1)


def _tile_bwd(s, n_lat_tiles):
    return jnp.where(s == 0, n_lat_tiles, n_lat_tiles - s)


def _tri_masks(forward):
    row = lax.broadcasted_iota(jnp.int32, (CHUNK, CHUNK), 0)
    col = lax.broadcasted_iota(jnp.int32, (CHUNK, CHUNK), 1)
    incl = (col <= row) if forward else (col >= row)
    strict = (col < row) if forward else (col > row)
    return incl, strict, (row == col).astype(F32)


def _gla_kernel(qf, kf, vf, zf, qb, kb, vb, zb, wf, wb, bf, bb, of, ob, st_ref):
    @pl.when(pl.program_id(1) == 0)
    def _():
        st_ref[...] = jnp.zeros_like(st_ref)

    nsub = ROW_TILE // CHUNK
    dirs = ((qf, kf, vf, zf, wf, bf, of), (qb, kb, vb, zb, wb, bb, ob))
    for d, (q_ref, k_ref, v_ref, z_ref, w_ref, b_ref, o_ref) in enumerate(dirs):
        forward = d == 0
        incl, _, _ = _tri_masks(forward)
        tri = incl.astype(F32)
        st = st_ref[d]
        for c in (range(nsub) if forward else reversed(range(nsub))):
            sl = slice(c * CHUNK, (c + 1) * CHUNK)
            q, k, v, zs = q_ref[sl, :], k_ref[sl, :], v_ref[sl, :], z_ref[sl, :]
            logit = _dot(zs, w_ref[...], HI) + b_ref[...]
            g = -_softplus(-logit) / GLA_GATE_NORM
            cum = _dot(tri, g, HI)
            tot = cum[CHUNK - 1:CHUNK] if forward else cum[0:1]
            qd = (q * (GLA_DK ** -0.5) * jnp.exp(cum)).astype(BF16)
            ki = (k * jnp.exp(-cum)).astype(BF16)
            ke = (k * jnp.exp(tot - cum)).astype(BF16)
            vb16 = v.astype(BF16)
            sc = jnp.where(incl, _dot_nt(qd, ki), 0.0).astype(BF16)
            o_ref[sl, :] = _dot(sc, vb16) + _dot_nt(qd, st.astype(BF16))
            st = jnp.exp(tot) * st + _dot_tn(vb16, ke)
        st_ref[d] = st


def _gla_call(zab, zsm, wpad, bias, n_lat_tiles):
    nt = n_lat_tiles + 1
    tf = functools.partial(_tile_fwd, n_lat_tiles=n_lat_tiles)
    tb = functools.partial(_tile_bwd, n_lat_tiles=n_lat_tiles)
    kcol = GLA_HEADS
    vcol = GLA_HEADS

    def dir_specs(t):
        return [pl.BlockSpec((ROW_TILE, GLA_DK), lambda h, s: (t(s), h)),
                pl.BlockSpec((ROW_TILE, GLA_DK), lambda h, s: (t(s), kcol + h)),
                pl.BlockSpec((ROW_TILE, GLA_DV), lambda h, s: (t(s), vcol + h)),
                pl.BlockSpec((ROW_TILE, LANE), lambda h, s: (t(s), 0))]

    def w_spec(d):
        return pl.BlockSpec((None, None, LANE, LANE), lambda h, s: (d, h, 0, 0))

    def b_spec(d):
        return pl.BlockSpec((None, None, 1, LANE), lambda h, s: (d, h, 0, 0))

    out_sds = jax.ShapeDtypeStruct((nt * ROW_TILE, GLA_HEADS * GLA_DV), F32)
    return pl.pallas_call(
        _gla_kernel, grid=(GLA_HEADS, nt),
        in_specs=dir_specs(tf) + dir_specs(tb) + [w_spec(0), w_spec(1), b_spec(0), b_spec(1)],
        out_specs=[pl.BlockSpec((ROW_TILE, GLA_DV), lambda h, s: (tf(s), h)),
                   pl.BlockSpec((ROW_TILE, GLA_DV), lambda h, s: (tb(s), h))],
        out_shape=[out_sds, out_sds],
        scratch_shapes=[pltpu.VMEM((2, GLA_DV, GLA_DK), F32)],
        compiler_params=_cparams(2), name="gla_scan",
    )(zab, zab, zab, zsm, zab, zab, zab, zsm, wpad, wpad, bias, bias)


def _unit_lower_inverse(m, eye):
    a = -m
    p = eye + a
    x = a
    for _ in range(int(math.log2(CHUNK)) - 1):
        x = _dot(x, x, HI)
        p = p + _dot(p, x, HI)
    return p


def _gdn_kernel(qf, kf, vf, zf, qb, kb, vb, zb, saf, sab, sbf, sbb, cf, cb, of, ob, s_ref):
    @pl.when(pl.program_id(1) == 0)
    def _():
        s_ref[...] = jnp.zeros_like(s_ref)

    nsub = ROW_TILE // CHUNK
    ones = jnp.ones((CHUNK, CHUNK), F32)
    dirs = ((qf, kf, vf, zf, saf, sbf, cf, of), (qb, kb, vb, zb, sab, sbb, cb, ob))
    for d, (q_ref, k_ref, v_ref, z_ref, sa_ref, sb_ref, c_ref, o_ref) in enumerate(dirs):
        forward = d == 0
        incl, strict, eye = _tri_masks(forward)
        tri = incl.astype(F32)
        a_log, dt_bias = c_ref[0:1, :], c_ref[1:2, :]
        s = s_ref[d]
        for c in (range(nsub) if forward else reversed(range(nsub))):
            sl = slice(c * CHUNK, (c + 1) * CHUNK)
            q, k, v, zs = q_ref[sl, :], k_ref[sl, :], v_ref[sl, :], z_ref[sl, :]
            log_a = -jnp.exp(a_log) * _softplus(_dot(zs, sa_ref[...], HI) + dt_bias)
            beta = jax.nn.sigmoid(_dot(zs, sb_ref[...], HI))
            cum = _dot(tri, log_a, HI)
            tot = cum[CHUNK - 1:CHUNK] if forward else cum[0:1]
            cum_i = cum[:, :CHUNK]
            cum_j = _dot(ones, eye * cum_i, HI)
            decay = jnp.exp(jnp.where(incl, cum_i - cum_j, -jnp.inf))
            k16 = k.astype(BF16)
            m = jnp.where(strict, _dot_nt(k16, k16) * decay * beta[:, :CHUNK], 0.0)
            t_inv = _unit_lower_inverse(m, eye)
            e_cum = jnp.exp(cum)
            u = _dot(t_inv, v * beta, HI)
            w = _dot(t_inv, k * (beta * e_cum), HI)
            qd = (q * e_cum).astype(BF16)
            sc = jnp.where(incl, _dot_nt(q.astype(BF16), k16) * decay, 0.0).astype(BF16)
            ke = (k * jnp.exp(tot - cum)).astype(BF16)
            s16 = s.astype(BF16)
            v_new = u - _dot(w.astype(BF16), s16)
            vn16 = v_new.astype(BF16)
            o_ref[sl, :] = _dot(qd, s16) + _dot(sc, vn16)
            s = jnp.exp(tot) * s + _dot_tn(ke, vn16)
        s_ref[d] = s


def _gdn_call(qkv, zsm, sel_a, sel_b, consts, n_lat_tiles):
    nt = n_lat_tiles + 1
    tf = functools.partial(_tile_fwd, n_lat_tiles=n_lat_tiles)
    tb = functools.partial(_tile_bwd, n_lat_tiles=n_lat_tiles)
    nh = GDN_HEADS

    def dir_specs(t):
        return [pl.BlockSpec((ROW_TILE, GDN_DK), lambda h, s: (t(s), h)),
                pl.BlockSpec((ROW_TILE, GDN_DK), lambda h, s: (t(s), nh + h)),
                pl.BlockSpec((ROW_TILE, GDN_DK), lambda h, s: (t(s), 2 * nh + h)),
                pl.BlockSpec((ROW_TILE, LANE), lambda h, s: (t(s), 0))]

    def sel_spec(d):
        return pl.BlockSpec((None, None, LANE, LANE), lambda h, s: (d, h, 0, 0))

    def c_spec(d):
        return pl.BlockSpec((None, None, 8, LANE), lambda h, s: (d, h, 0, 0))

    out_sds = jax.ShapeDtypeStruct((nt * ROW_TILE, nh * GDN_DK), F32)
    return pl.pallas_call(
        _gdn_kernel, grid=(nh, nt),
        in_specs=(dir_specs(tf) + dir_specs(tb)
                  + [sel_spec(0), sel_spec(1), sel_spec(0), sel_spec(1), c_spec(0), c_spec(1)]),
        out_specs=[pl.BlockSpec((ROW_TILE, GDN_DK), lambda h, s: (tf(s), h)),
                   pl.BlockSpec((ROW_TILE, GDN_DK), lambda h, s: (tb(s), h))],
        out_shape=[out_sds, out_sds],
        scratch_shapes=[pltpu.VMEM((2, GDN_DK, GDN_DK), F32)],
        compiler_params=_cparams(2), name="gdn_scan",
    )(qkv, qkv, qkv, zsm, qkv, qkv, qkv, zsm, sel_a, sel_a, sel_b, sel_b, consts, consts)


def _merge_ab_kernel(gf, gb, df, db, gz, dz, gn, dn, o_ref):
    gla_w = GLA_HEADS * GLA_DV
    og = gf[...] + gb[...]
    zg = gz[...]
    for h in range(GLA_HEADS):
        sl = slice(h * GLA_DV, (h + 1) * GLA_DV)
        seg = og[:, sl]
        y = seg * lax.rsqrt(jnp.mean(seg * seg, axis=-1, keepdims=True) + NORM_EPS) * gn[...]
        o_ref[:, sl] = (y * _silu(zg[:, sl])).astype(BF16)
    od = df[...] + db[...]
    zd = dz[...]
    for h in range(GDN_HEADS):
        sl = slice(h * GDN_DK, (h + 1) * GDN_DK)
        seg = od[:, sl]
        y = seg * lax.rsqrt(jnp.mean(seg * seg, axis=-1, keepdims=True) + NORM_EPS) * dn[...]
        o_ref[:, gla_w + h * GDN_DK:gla_w + (h + 1) * GDN_DK] = (y * _silu(zd[:, sl])).astype(BF16)


def _merge_ab_call(gla_f, gla_b, gdn_f, gdn_b, zab, gla_norm, gdn_norm):
    m = gla_f.shape[0]
    w = GLA_HEADS * GLA_DV
    row = lambda i: (i, 0)
    return pl.pallas_call(
        _merge_ab_kernel, grid=(m // ROW_TILE,),
        in_specs=[pl.BlockSpec((ROW_TILE, w), row)] * 4
        + [pl.BlockSpec((ROW_TILE, w), lambda i: (i, 2)), pl.BlockSpec((ROW_TILE, w), lambda i: (i, 6)),
           pl.BlockSpec((1, GLA_DV), lambda i: (0, 0)), pl.BlockSpec((1, GDN_DK), lambda i: (0, 0))],
        out_specs=pl.BlockSpec((ROW_TILE, 2 * w), row),
        out_shape=jax.ShapeDtypeStruct((m, 2 * w), BF16),
        compiler_params=_cparams(1), name="merge_ab",
    )(gla_f, gla_b, gdn_f, gdn_b, zab, zab, gla_norm.reshape(1, -1), gdn_norm.reshape(1, -1))


def _out0_kernel(a_ref, w_ref, x_ref, c_ref, m_ref, o_ref, *, n_lat_tiles, d):
    is_ctx = pl.program_id(0) == n_lat_tiles
    acc = _dot(a_ref[...], w_ref[...])
    m = m_ref[...]
    gate = jnp.where(is_ctx, m[1:2, 2 * d:], m[0:1, 2 * d:])
    o_ref[...] = jnp.where(is_ctx, c_ref[...], x_ref[...]) + gate * acc


def _out0_call(ycat, w_out, x_lat, x_ctx, mods, n_lat_tiles):
    m, k = ycat.shape
    d = w_out.shape[1]
    return pl.pallas_call(
        functools.partial(_out0_kernel, n_lat_tiles=n_lat_tiles, d=d), grid=(m // ROW_TILE,),
        in_specs=[pl.BlockSpec((ROW_TILE, k), lambda i: (i, 0)),
                  pl.BlockSpec((k, d), lambda i: (0, 0)),
                  pl.BlockSpec((ROW_TILE, d), lambda i: (jnp.minimum(i, n_lat_tiles - 1), 0)),
                  pl.BlockSpec((ROW_TILE, d), lambda i: (0, 0)),
                  pl.BlockSpec((8, 3 * d), lambda i: (0, 0))],
        out_specs=pl.BlockSpec((ROW_TILE, d), lambda i: (i, 0)),
        out_shape=jax.ShapeDtypeStruct((m, d), F32),
        compiler_params=_cparams(1), name="out_proj_ab",
    )(ycat, w_out, x_lat, x_ctx, mods)


def _lru_gates(seg, wa, ba, wx, bx, lam):
    seg16 = seg.astype(BF16)
    r = jax.nn.sigmoid(_dot(seg16, wa) + ba)
    gi = jax.nn.sigmoid(_dot(seg16, wx) + bx)
    log_a = -LRU_C * _softplus(-lam) * r
    a = jnp.exp(log_a)
    return a, jnp.sqrt(jnp.tanh(-log_a) * (a * a + 1.0)) * (gi * seg)


def _lru_ctx_kernel(x_ref, cw_ref, cb_ref, wa_ref, ba_ref, wx_ref, bx_ref, lam_ref, o_ref, a_scr, d_scr):
    x = x_ref[...]
    rows, width = x.shape
    blk = width // LRU_BLOCKS
    cw = cw_ref[...]
    row = lax.broadcasted_iota(jnp.int32, x.shape, 0)
    zero = jnp.zeros((1, width), F32)
    xc = (cw[0:1] * _row_shift(x, -1, row, zero, zero, zero) + cw[1:2] * x
          + cw[2:3] * _row_shift(x, 1, row, zero, zero, zero)
          + cw[3:4] * _row_shift(x, 2, row, zero, zero, zero) + cb_ref[...])
    for d in range(2):
        for b in range(LRU_BLOCKS):
            sl = slice(b * blk, (b + 1) * blk)
            a, drv = _lru_gates(xc[:, sl], wa_ref[d, b], ba_ref[d, :, sl], wx_ref[d, b],
                                bx_ref[d, :, sl], lam_ref[d, :, sl])
            a_scr[d, :, sl] = a
            d_scr[d, :, sl] = drv

    def body(t, carry):
        hf, hb = carry
        hf = a_scr[0, pl.ds(t, 1), :] * hf + d_scr[0, pl.ds(t, 1), :]
        tb = rows - 1 - t
        hb = a_scr[1, pl.ds(tb, 1), :] * hb + d_scr[1, pl.ds(tb, 1), :]
        return hf, hb

    hf, hb = lax.fori_loop(0, rows, body, (zero, zero), unroll=8)
    o_ref[...] = jnp.zeros_like(o_ref)
    o_ref[0:1, :] = hf
    o_ref[1:2, :] = hb


def _lru_lat_kernel(xf, pf, nf, xb, pb, nb, h0_ref, cw_ref, cb_ref, wa_ref, ba_ref, wx_ref, bx_ref,
                    lam_ref, of, ob, a_scr, d_scr, hl_scr, pc_scr, carry, *, n_colblocks):
    ci, s = pl.program_id(0), pl.program_id(1)

    @pl.when(s == 0)
    def _():
        carry[...] = h0_ref[...]

    cw = cw_ref[...]
    dirs = ((xf, pf, nf, of, s), (xb, pb, nb, ob, n_colblocks - 1 - s))
    for d, (x_ref, p_ref, n_ref, o_ref, cblk) in enumerate(dirs):
        forward = d == 0
        x3 = x_ref[...]
        rows, ncol, tc = x3.shape
        has_prev, has_next = cblk > 0, cblk < n_colblocks - 1
        sub = lax.broadcasted_iota(jnp.int32, (ncol, tc), 0)
        prev_last = jnp.where(has_prev, p_ref[7, ncol - 1:ncol, :], 0.0)
        next_r0 = jnp.where(has_next, n_ref[0, 0:1, :], 0.0)
        next_r1 = jnp.where(has_next, n_ref[1, 0:1, :], 0.0)
        first_m1 = jnp.where(sub == 0, prev_last, pltpu.roll(x3[rows - 1], 1, 0))
        last_p1 = jnp.where(sub == ncol - 1, next_r0, pltpu.roll(x3[0], ncol - 1, 0))
        last_p2 = jnp.where(sub == ncol - 1, next_r1, pltpu.roll(x3[1], ncol - 1, 0))
        xm1 = jnp.concatenate([first_m1[None], x3[:rows - 1]], axis=0)
        xp1 = jnp.concatenate([x3[1:], last_p1[None]], axis=0)
        xp2 = jnp.concatenate([x3[2:], last_p1[None], last_p2[None]], axis=0)
        xc = cw[0:1] * xm1 + cw[1:2] * x3 + cw[2:3] * xp1 + cw[3:4] * xp2 + cb_ref[...]
        a, drv = _lru_gates(xc.reshape(rows * ncol, tc), wa_ref[d, ci], ba_ref[d], wx_ref[d, ci],
                            bx_ref[d], lam_ref[d])
        a_scr[d] = a.reshape(rows, ncol, tc)
        d_scr[d] = drv.reshape(rows, ncol, tc)

        def body(i, hp):
            h, p = hp
            r = i if forward else rows - 1 - i
            ar = a_scr[d, r]
            h = ar * h + d_scr[d, r]
            p = ar * p
            hl_scr[d, r] = h
            pc_scr[d, r] = p
            return h, p

        lax.fori_loop(0, rows, body, (jnp.zeros((ncol, tc), F32), jnp.ones((ncol, tc), F32)), unroll=8)

        end = rows - 1 if forward else 0
        hl_end, pc_end = hl_scr[d, end], pc_scr[d, end]
        h = carry[d:d + 1, :]
        h_in = jnp.zeros((ncol, tc), F32)
        for j in (range(ncol) if forward else reversed(range(ncol))):
            h_in = jnp.where(sub == j, h, h_in)
            h = hl_end[j:j + 1, :] + pc_end[j:j + 1, :] * h
        carry[d:d + 1, :] = h
        o_ref[...] = hl_scr[d] + pc_scr[d] * h_in[None]


def _lru_call(zcd, conv_w, conv_b, wa, ba, wx, bx, lam, n_lat):
    m, ncols_z = zcd.shape
    width = conv_w.shape[1]
    n_cols = GRID_W
    rows = n_lat // n_cols
    tc = width // LRU_BLOCKS
    ncb = n_cols // 8
    full = lambda a: pl.BlockSpec(a.shape, lambda *_: (0,) * a.ndim)
    wa16, wx16 = wa.astype(BF16), wx.astype(BF16)
    cb2, ba3, bx3, lam3 = (conv_b.reshape(1, width), ba.reshape(2, 1, width), bx.reshape(2, 1, width),
                           lam.reshape(2, 1, width))

    params = (conv_w, cb2, wa16, ba3, wx16, bx3, lam3)
    h0 = pl.pallas_call(
        _lru_ctx_kernel, grid=(1,),
        in_specs=[pl.BlockSpec((ROW_TILE, width), lambda i: (n_lat // ROW_TILE, 0))] + [full(p) for p in params],
        out_specs=pl.BlockSpec((8, width), lambda i: (0, 0)),
        out_shape=jax.ShapeDtypeStruct((8, width), F32),
        scratch_shapes=[pltpu.VMEM((2, ROW_TILE, width), F32)] * 2,
        compiler_params=_cparams(1), name="rglru_ctx",
    )(zcd, *params)

    z3 = zcd.reshape(m // n_cols, n_cols, ncols_z)
    hb8 = rows // 8
    bf = lambda s: s
    bb = lambda s: ncb - 1 - s

    def dir_specs(blk):
        return [pl.BlockSpec((rows, 8, tc), lambda c, s: (0, blk(s), c)),
                pl.BlockSpec((8, 8, tc), lambda c, s: (hb8 - 1, jnp.maximum(blk(s) - 1, 0), c)),
                pl.BlockSpec((8, 8, tc), lambda c, s: (0, jnp.minimum(blk(s) + 1, ncb - 1), c))]

    vec = lambda lead: pl.BlockSpec((lead, 1, tc), lambda c, s: (0, 0, c))
    out_sds = jax.ShapeDtypeStruct((rows, n_cols, width), F32)
    hf, hb = pl.pallas_call(
        functools.partial(_lru_lat_kernel, n_colblocks=ncb), grid=(width // tc, ncb),
        in_specs=dir_specs(bf) + dir_specs(bb)
        + [pl.BlockSpec((8, tc), lambda c, s: (0, c)),
           pl.BlockSpec((4, tc), lambda c, s: (0, c)), pl.BlockSpec((1, tc), lambda c, s: (0, c)),
           full(wa16), vec(2), full(wx16), vec(2), vec(2)],
        out_specs=[pl.BlockSpec((rows, 8, tc), lambda c, s: (0, bf(s), c)),
                   pl.BlockSpec((rows, 8, tc), lambda c, s: (0, bb(s), c))],
        out_shape=[out_sds, out_sds],
        scratch_shapes=[pltpu.VMEM((2, rows, 8, tc), F32)] * 4 + [pltpu.VMEM((8, tc), F32)],
        compiler_params=_cparams(2), name="rglru_scan",
    )(z3, z3, z3, z3, z3, z3, h0, conv_w, cb2, wa16, ba3, wx16, bx3, lam3)
    return hf.reshape(n_lat, width), hb.reshape(n_lat, width)


def _hy_prep_kernel(*refs, n_lat_tiles):
    xs, (w_ref, b_ref, x0_ref, z_ref) = refs[:9], refs[9:]
    i = pl.program_id(0)
    has_prev = i > 0
    has_next = i < n_lat_tiles - 1
    outs = []
    width = x0_ref.shape[1]
    for j in range(3):
        x_ref, p_ref, n_ref = xs[3 * j:3 * j + 3]
        x = x_ref[...]
        row = lax.broadcasted_iota(jnp.int32, x.shape, 0)
        prev_row = jnp.where(has_prev, p_ref[7:8, :], 0.0)
        next0 = jnp.where(has_next, n_ref[0:1, :], 0.0)
        sl = slice(j * width, (j + 1) * width)
        outs.append(w_ref[0:1, sl] * _row_shift(x, -1, row, prev_row, next0, next0)
                    + w_ref[1:2, sl] * x
                    + w_ref[2:3, sl] * _row_shift(x, 1, row, prev_row, next0, next0)
                    + b_ref[:, sl])
    x0_ref[...] = outs[0]
    z_ref[...] = outs[1] * outs[2]


def _hy_prep_call(zcd, conv_w, conv_b, n_lat_tiles):
    width = conv_w.shape[1] // 3
    hb = ROW_TILE // 8
    last8 = n_lat_tiles * hb - 1
    in_specs, args = [], []
    for j in range(3):
        cbk = 2 + j
        in_specs += [pl.BlockSpec((ROW_TILE, width), lambda i, cbk=cbk: (i, cbk)),
                     pl.BlockSpec((8, width), lambda i, cbk=cbk: (jnp.maximum(i * hb - 1, 0), cbk)),
                     pl.BlockSpec((8, width), lambda i, cbk=cbk: (jnp.minimum((i + 1) * hb, last8), cbk))]
        args += [zcd, zcd, zcd]
    in_specs += [pl.BlockSpec((3, 3 * width), lambda i: (0, 0)), pl.BlockSpec((1, 3 * width), lambda i: (0, 0))]
    out_sds = jax.ShapeDtypeStruct((n_lat_tiles * ROW_TILE, width), F32)
    return pl.pallas_call(
        functools.partial(_hy_prep_kernel, n_lat_tiles=n_lat_tiles), grid=(n_lat_tiles,),
        in_specs=in_specs,
        out_specs=[pl.BlockSpec((ROW_TILE, width), lambda i: (i, 0))] * 2,
        out_shape=[out_sds, out_sds],
        compiler_params=_cparams(1), name="hyena_prep",
    )(*args, conv_w, conv_b.reshape(1, -1))


def _hy_filter_kernel(e_ref, w1, b1, w2, b2, w3, fr, dec, f_ref, ssq_ref):
    i = pl.program_id(0)
    e = e_ref[...]
    hid = jnp.sin(fr[...] * (_dot(e, w1[...], HI) + b1[...]))
    hid = jnp.sin(fr[...] * (_dot(hid, w2[...], HI) + b2[...]))
    filt = _dot(hid, w3[...], HI) * jnp.exp(-e[:, 0:1] * dec[...])
    half = filt.shape[1] // 2
    row = lax.broadcasted_iota(jnp.int32, filt.shape, 0)
    col = lax.broadcasted_iota(jnp.int32, filt.shape, 1)
    unused = jnp.logical_and(jnp.logical_and(i == 0, row == 0), col >= half)
    filt = jnp.where(unused, 0.0, filt)
    f_ref[...] = filt

    @pl.when(i == 0)
    def _():
        ssq_ref[...] = jnp.zeros_like(ssq_ref)

    ssq_ref[...] = ssq_ref[...] + jnp.sum(filt * filt, axis=0, keepdims=True)


def _hy_filter_call(emb, w1, b1, w2, b2, w3, freq, decay):
    length = emb.shape[0]
    n_out = w3.shape[1]
    hid = w1.shape[1]
    pad_r = lambda a: jnp.pad(a, ((0, LANE - a.shape[0]), (0, 0)))
    pad_c = lambda a: jnp.pad(a, ((0, 0), (0, LANE - a.shape[1])))
    params = (pad_c(pad_r(w1)), pad_c(b1.reshape(1, hid)), pad_c(pad_r(w2)), pad_c(b2.reshape(1, hid)),
              pad_r(w3), pad_c(freq.reshape(1, hid)), decay.reshape(1, n_out))
    full = lambda a: pl.BlockSpec(a.shape, lambda i: (0,) * a.ndim)
    return pl.pallas_call(
        _hy_filter_kernel, grid=(length // ROW_TILE,),
        in_specs=[pl.BlockSpec((ROW_TILE, LANE), lambda i: (i, 0))] + [full(p) for p in params],
        out_specs=[pl.BlockSpec((ROW_TILE, n_out), lambda i: (i, 0)),
                   pl.BlockSpec((8, n_out), lambda i: (0, 0))],
        out_shape=[jax.ShapeDtypeStruct((length, n_out), F32), jax.ShapeDtypeStruct((8, n_out), F32)],
        compiler_params=_cparams(1), name="hyena_filter",
    )(emb, *params)


SUB = 8


def _dft_a_kernel(x_ref, c_ref, s_ref, re_ref, im_ref):
    for j in range(SUB):
        xb = x_ref[:, j, :].astype(BF16)
        re_ref[:, j, :] = _dot(c_ref[...], xb)
        im_ref[:, j, :] = _dot(s_ref[...], xb)


def _dft_a_call(x, cmat, smat, n1):
    length, ch = x.shape
    n2 = cmat.shape[0]
    n2h = n2 // 2
    tc = 512
    x4 = x.reshape(n2h, n1 // SUB, SUB, ch)
    out_sds = jax.ShapeDtypeStruct((n2, n1 // SUB, SUB, ch), F32)
    re, im = pl.pallas_call(
        _dft_a_kernel, grid=(ch // tc, n1 // SUB),
        in_specs=[pl.BlockSpec((n2h, None, SUB, tc), lambda c, i: (0, i, 0, c)),
                  pl.BlockSpec((n2, n2h), lambda c, i: (0, 0)),
                  pl.BlockSpec((n2, n2h), lambda c, i: (0, 0))],
        out_specs=[pl.BlockSpec((n2, None, SUB, tc), lambda c, i: (0, i, 0, c))] * 2,
        out_shape=[out_sds, out_sds],
        compiler_params=_cparams(2), name="dft_stage_a",
    )(x4, cmat, smat)
    return re.reshape(n2, n1, ch), im.reshape(n2, n1, ch)


def _dft_b_filter_kernel(pre, pim, fre, fim, g_ref, sp_ref, sf_ref, hre_ref, him_ref, *, n1, inv_n):
    scale = lax.rsqrt(sp_ref[0:1, :] + sf_ref[0:1, :] + NORM_EPS) * inv_n
    for j in range(SUB):
        g = g_ref[j]
        xp = _dot(g, jnp.concatenate([pre[j], pim[j]], axis=0).astype(BF16))
        xf = _dot(g, jnp.concatenate([fre[j], fim[j]], axis=0).astype(BF16))
        hre_ref[j] = (xp[:n1] + xf[:n1]) * scale
        him_ref[j] = (xp[n1:] - xf[n1:]) * scale


def _dft_b_filter_call(a_re, a_im, gmat, ssq):
    n2, n1, ch2 = a_re.shape
    ch = ch2 // 2
    tc = 256
    nct = ch // tc
    spec = lambda half: pl.BlockSpec((SUB, n1, tc), lambda k, c: (k, 0, half * nct + c))
    out_sds = jax.ShapeDtypeStruct((n2, n1, ch), F32)
    return pl.pallas_call(
        functools.partial(_dft_b_filter_kernel, n1=n1, inv_n=1.0 / (n1 * n2)), grid=(n2 // SUB, nct),
        in_specs=[spec(0), spec(0), spec(1), spec(1),
                  pl.BlockSpec((SUB, 2 * n1, 2 * n1), lambda k, c: (k, 0, 0)),
                  pl.BlockSpec((8, tc), lambda k, c: (0, c)),
                  pl.BlockSpec((8, tc), lambda k, c: (0, nct + c))],
        out_specs=[pl.BlockSpec((SUB, n1, tc), lambda k, c: (k, 0, c))] * 2,
        out_shape=[out_sds, out_sds],
        compiler_params=_cparams(2), name="dft_stage_b_filter",
    )(a_re, a_im, a_re, a_im, gmat, ssq, ssq)


def _dft_b_kernel(are, aim, hre, him, g_ref, gi_ref, bre_ref, bim_ref, *, n1):
    for j in range(SUB):
        x = _dot(g_ref[j], jnp.concatenate([are[j], aim[j]], axis=0).astype(BF16))
        xr, xi = x[:n1], x[n1:]
        hr, hi = hre[j], him[j]
        y = jnp.concatenate([xr * hr - xi * hi, xr * hi + xi * hr], axis=0).astype(BF16)
        b = _dot(gi_ref[j], y)
        bre_ref[:, j, :] = b[:n1]
        bim_ref[:, j, :] = b[n1:]


def _dft_b_call(a_re, a_im, h_re, h_im, gmat, gimat):
    n2, n1, ch = a_re.shape
    tc = 512
    slab = pl.BlockSpec((SUB, n1, tc), lambda k, c: (k, 0, c))
    gspec = pl.BlockSpec((SUB, 2 * n1, 2 * n1), lambda k, c: (k, 0, 0))
    out_sds = jax.ShapeDtypeStruct((n1, n2 // SUB, SUB, ch), F32)
    re, im = pl.pallas_call(
        functools.partial(_dft_b_kernel, n1=n1), grid=(n2 // SUB, ch // tc),
        in_specs=[slab] * 4 + [gspec, gspec],
        out_specs=[pl.BlockSpec((n1, None, SUB, tc), lambda k, c: (0, k, 0, c))] * 2,
        out_shape=[out_sds, out_sds],
        compiler_params=_cparams(2), name="dft_stage_b",
    )(a_re, a_im, h_re, h_im, gmat, gimat)
    return re.reshape(n1, n2, ch), im.reshape(n1, n2, ch)


def _dft_c_kernel(bre, bim, x0_ref, z_ref, hz_ref, bias_ref, c_ref, s_ref, o_ref):
    for j in range(SUB):
        y = _dot(c_ref[...], bre[j].astype(BF16)) + _dot(s_ref[...], bim[j].astype(BF16))
        o_ref[:, j, :] = (x0_ref[:, j, :] * (y + z_ref[:, j, :] * bias_ref[...])
                          * _silu(hz_ref[:, j, :]))


def _dft_c_call(b_re, b_im, x0, z, zcd, bias, cmat, smat):
    n1, n2, ch = b_re.shape
    n2h = n2 // 2
    length = n1 * n2h
    tc = 256
    nct = ch // tc
    view = lambda a: a.reshape(a.shape[0] // n1, n1 // SUB, SUB, a.shape[1])
    tspec = lambda cblk: pl.BlockSpec((n2h, None, SUB, tc), lambda i, c: (0, i, 0, cblk * nct + c))
    out = pl.pallas_call(
        _dft_c_kernel, grid=(n1 // SUB, nct),
        in_specs=[pl.BlockSpec((SUB, n2, tc), lambda i, c: (i, 0, c))] * 2
        + [tspec(0), tspec(0), tspec(5), pl.BlockSpec((1, tc), lambda i, c: (0, c)),
           pl.BlockSpec((n2h, n2), lambda i, c: (0, 0)), pl.BlockSpec((n2h, n2), lambda i, c: (0, 0))],
        out_specs=tspec(0),
        out_shape=jax.ShapeDtypeStruct((n2h, n1 // SUB, SUB, ch), F32),
        compiler_params=_cparams(2), name="dft_stage_c",
    )(b_re, b_im, view(x0), view(z), view(zcd), bias.reshape(1, ch), cmat, smat)
    return out.reshape(length, ch)


def _dft_constants(n1, n2):
    n = n1 * n2
    n2h = n2 // 2
    k2 = jnp.arange(n2, dtype=jnp.int32)
    ang_a = (2.0 * math.pi / n2) * ((k2[:, None] * k2[None, :n2h]) % n2).astype(F32)
    ca, sa = jnp.cos(ang_a), -jnp.sin(ang_a)
    cc, sc = jnp.cos(ang_a).T, -jnp.sin(ang_a).T
    i1 = jnp.arange(n1, dtype=jnp.int32)
    prod = (i1[None, None, :] * (n2 * i1[None, :, None] + k2[:, None, None])) % n
    th = (2.0 * math.pi / n) * prod.astype(F32)
    gr, gi = jnp.cos(th), -jnp.sin(th)
    g = jnp.concatenate([jnp.concatenate([gr, -gi], axis=2), jnp.concatenate([gi, gr], axis=2)], axis=1)
    tr, ti = jnp.swapaxes(jnp.cos(th), 1, 2), jnp.swapaxes(jnp.sin(th), 1, 2)
    ginv = jnp.concatenate([jnp.concatenate([tr, -ti], axis=2), jnp.concatenate([ti, tr], axis=2)], axis=1)
    b = lambda a: a.astype(BF16)
    return b(ca), b(sa), b(cc), b(sc), b(g), b(ginv)


def _final_kernel(hf, hb, lz, hy, w1, w2, x_ref, m_ref, g_ref, o_ref, *, d):
    a1 = ((hf[...] + hb[...]) * _silu(lz[...])).astype(BF16)
    acc = _dot(a1, w1[...]) + _dot(hy[...].astype(BF16), w2[...])
    xn = x_ref[...] + m_ref[0:1, 2 * d:] * acc
    o_ref[...] = xn * lax.rsqrt(jnp.mean(xn * xn, axis=-1, keepdims=True) + NORM_EPS) * g_ref[...]


def _final_call(hf, hb, zcd, hy, w_out, x1, mods, final_g, n_lat_tiles):
    d = w_out.shape[1]
    half = w_out.shape[0] // 2
    row = lambda i: (i, 0)
    const = lambda i: (0, 0)
    return pl.pallas_call(
        functools.partial(_final_kernel, d=d), grid=(n_lat_tiles,),
        in_specs=[pl.BlockSpec((ROW_TILE, half), row), pl.BlockSpec((ROW_TILE, half), row),
                  pl.BlockSpec((ROW_TILE, half), lambda i: (i, 1)), pl.BlockSpec((ROW_TILE, half), row),
                  pl.BlockSpec((half, d), const), pl.BlockSpec((half, d), lambda i: (1, 0)),
                  pl.BlockSpec((ROW_TILE, d), row), pl.BlockSpec((8, 3 * d), const),
                  pl.BlockSpec((1, d), const)],
        out_specs=pl.BlockSpec((ROW_TILE, d), row),
        out_shape=jax.ShapeDtypeStruct((n_lat_tiles * ROW_TILE, d), F32),
        compiler_params=_cparams(1), name="out_proj_cd_final",
    )(hf, hb, zcd, hy, w_out, w_out, x1, mods, final_g.reshape(1, d))


def kernel(x, c, ctx, c_ctx, mod_w, mod_b, norm_g, ab_w_in, ab_gla_wg2, ab_gla_bg2, ab_gla_norm, ab_gdn_conv, ab_gdn_a_log, ab_gdn_dt_bias, ab_gdn_norm, ab_w_out, cd_w_in, cd_lru_conv_w, cd_lru_conv_b, cd_lru_wa, cd_lru_ba, cd_lru_wx, cd_lru_bx, cd_lru_lambda, cd_hy_conv_w, cd_hy_conv_b, cd_hy_w1, cd_hy_b1, cd_hy_w2, cd_hy_b2, cd_hy_w3, cd_hy_freq, cd_hy_decay, cd_hy_bias, cd_w_out, final_g):
    _, n_lat, d = x.shape
    assert x.shape[0] == 1 and ctx.shape[1] == ROW_TILE and n_lat % ROW_TILE == 0
    assert mod_w.shape[0] == 2 and d == 2 * GLA_HEADS * GLA_DV
    nlt = n_lat // ROW_TILE
    x_lat, x_ctx = x[0], ctx[0]

    cvec = jnp.zeros((8, d), F32).at[0].set(c[0]).at[1].set(c_ctx)
    mods = _mod_call(cvec, mod_w, mod_b)

    gla_qk, gla_v = GLA_HEADS * GLA_DK, GLA_HEADS * GLA_DV
    gdn_qk = GDN_HEADS * GDN_DK
    sizes = (gla_qk, gla_qk, gla_v, 2 * GLA_RANK, gla_v, 3 * gdn_qk, 2 * GDN_HEADS, 2 * GDN_HEADS, gdn_qk)
    gq, gk, gv, glr, gz, dqkv, da, db, dz = jnp.split(ab_w_in[0], np.cumsum(sizes)[:-1].tolist(), axis=1)
    w_main = jnp.concatenate([gq, gk, gv, gz, dqkv, dz], axis=1).astype(BF16)
    n_small = glr.shape[1] + da.shape[1] + db.shape[1]
    w_small = jnp.concatenate([glr, da, db, jnp.zeros((d, LANE - n_small), F32)], axis=1).astype(BF16)

    h0 = _norm_mod_call(x_lat, x_ctx, mods[0], norm_g[0], nlt)
    zab = _mm_call(h0, w_main, 1024, "in_proj_ab")
    zsm = _mm_call(h0, w_small, LANE, "in_proj_ab_small")

    wg = ab_gla_wg2[0].reshape(2, GLA_RANK, GLA_HEADS, GLA_DK).transpose(0, 2, 1, 3)
    wpad = jnp.zeros((2, GLA_HEADS, LANE, GLA_DK), F32)
    for dd in range(2):
        wpad = wpad.at[dd, :, dd * GLA_RANK:(dd + 1) * GLA_RANK, :].set(wg[dd])
    gbias = ab_gla_bg2[0].reshape(2, GLA_HEADS, 1, GLA_DK)
    gla_f, gla_b = _gla_call(zab, zsm, wpad, gbias, nlt)

    qkv = _gdn_prep_call(zab, ab_gdn_conv[0], nlt)
    lane = jnp.arange(LANE)
    a_col = 2 * GLA_RANK + jnp.arange(2 * GDN_HEADS).reshape(2, GDN_HEADS)
    sel_a = jnp.broadcast_to((lane[None, None, :, None] == a_col[:, :, None, None]).astype(F32),
                             (2, GDN_HEADS, LANE, LANE))
    sel_b = jnp.broadcast_to((lane[None, None, :, None] == (a_col + 2 * GDN_HEADS)[:, :, None, None]).astype(F32),
                             (2, GDN_HEADS, LANE, LANE))
    consts = jnp.zeros((2, GDN_HEADS, 8, LANE), F32)
    consts = consts.at[:, :, 0, :].set(jnp.broadcast_to(ab_gdn_a_log[0][:, :, None], (2, GDN_HEADS, LANE)))
    consts = consts.at[:, :, 1, :].set(jnp.broadcast_to(ab_gdn_dt_bias[0][:, :, None], (2, GDN_HEADS, LANE)))
    gdn_f, gdn_b = _gdn_call(qkv, zsm, sel_a, sel_b, consts, nlt)

    ycat = _merge_ab_call(gla_f, gla_b, gdn_f, gdn_b, zab, ab_gla_norm[0], ab_gdn_norm[0])
    x1 = _out0_call(ycat, ab_w_out[0].astype(BF16), x_lat, x_ctx, mods[0], nlt)

    h1 = _norm_mod_call(x1, None, mods[1], norm_g[1], nlt)
    zcd = _mm_call(h1, cd_w_in[0].astype(BF16), 1024, "in_proj_cd")

    lru_f, lru_b = _lru_call(zcd, cd_lru_conv_w[0], cd_lru_conv_b[0], cd_lru_wa[0], cd_lru_ba[0],
                             cd_lru_wx[0], cd_lru_bx[0], cd_lru_lambda[0], n_lat)

    x0, z = _hy_prep_call(zcd, cd_hy_conv_w[0], cd_hy_conv_b[0], nlt)

    pos = jnp.arange(n_lat, dtype=F32)
    t01 = pos / max(n_lat - 1, 1)
    bands = jnp.linspace(1e-4, HY_BANDS - 1, HY_BANDS, dtype=F32)
    ang = (2.0 * math.pi / n_lat) * pos[:, None] * bands[None, :]
    emb = jnp.concatenate([t01[:, None], jnp.cos(ang), jnp.sin(ang)], axis=-1)
    emb = jnp.pad(emb, ((0, 0), (0, LANE - emb.shape[1])))
    filt, ssq = _hy_filter_call(emb, cd_hy_w1[0], cd_hy_b1[0], cd_hy_w2[0], cd_hy_b2[0], cd_hy_w3[0],
                                cd_hy_freq[0], cd_hy_decay[0])

    n1 = LANE if n_lat >= LANE * LANE else 32
    n2 = 2 * n_lat // n1
    ca, sa, cc, sc, gmat, gimat = _dft_constants(n1, n2)
    fa_re, fa_im = _dft_a_call(filt, ca, sa, n1)
    h_re, h_im = _dft_b_filter_call(fa_re, fa_im, gmat, ssq)
    za_re, za_im = _dft_a_call(z, ca, sa, n1)
    b_re, b_im = _dft_b_call(za_re, za_im, h_re, h_im, gmat, gimat)
    y_hy = _dft_c_call(b_re, b_im, x0, z, zcd, cd_hy_bias[0], cc, sc)

    out = _final_call(lru_f, lru_b, zcd, y_hy, cd_w_out[0].astype(BF16), x1, mods[1], final_g, nlt)
    return out[None]
```

```python
import functools
import math

import numpy as np
import jax
import jax.numpy as jnp
from jax import lax
from jax.experimental import pallas as pl
from jax.experimental.pallas import tpu as pltpu

F32 = jnp.float32
BF16 = jnp.bfloat16
HI = lax.Precision.HIGHEST

NORM_EPS = 1e-6
CHUNK = 64
ROW_TILE = 256
GRID_W = 64
LANE = 128
GLA_HEADS, GLA_DK, GLA_DV, GLA_RANK, GLA_GATE_NORM = 4, 128, 256, 16, 16.0
GDN_HEADS, GDN_DK = 8, 128
LRU_BLOCKS, LRU_C = 8, 8.0
HY_BANDS = 16
VMEM_LIMIT = 56 * 1024 * 1024


def _cparams(n_axes):
    return pltpu.CompilerParams(dimension_semantics=("arbitrary",) * n_axes,
                                vmem_limit_bytes=VMEM_LIMIT)


def _silu(x):
    return x * jax.nn.sigmoid(x)


def _softplus(x):
    return jnp.maximum(x, 0.0) + jnp.log1p(jnp.exp(-jnp.abs(x)))


def _dot(a, b, precision=None):
    return jnp.dot(a, b, precision=precision, preferred_element_type=F32)


def _dot_nt(a, b, precision=None):
    return lax.dot_general(a, b, (((1,), (1,)), ((), ())), precision=precision,
                           preferred_element_type=F32)


def _dot_tn(a, b, precision=None):
    return lax.dot_general(a, b, (((0,), (0,)), ((), ())), precision=precision,
                           preferred_element_type=F32)


def _bdot(a, b):
    return lax.dot_general(a, b, (((2,), (1,)), ((0,), (0,))), preferred_element_type=F32)


def _bdot_nt(a, b):
    return lax.dot_general(a, b, (((2,), (2,)), ((0,), (0,))), preferred_element_type=F32)


def _bdot_tn(a, b):
    return lax.dot_general(a, b, (((1,), (1,)), ((0,), (0,))), preferred_element_type=F32)


def _mod_kernel(c_ref, w_ref, b_ref, o_ref):
    o_ref[...] = _dot(_silu(c_ref[...]), w_ref[...], HI) + b_ref[...]


def _mod_call(cvec, mod_w, mod_b):
    depth, d, n = mod_w.shape
    tn = 512
    return pl.pallas_call(
        _mod_kernel, grid=(depth, n // tn),
        in_specs=[pl.BlockSpec((8, d), lambda l, j: (0, 0)),
                  pl.BlockSpec((None, d, tn), lambda l, j: (l, 0, j)),
                  pl.BlockSpec((None, 1, tn), lambda l, j: (l, 0, j))],
        out_specs=pl.BlockSpec((None, 8, tn), lambda l, j: (l, 0, j)),
        out_shape=jax.ShapeDtypeStruct((depth, 8, n), F32),
        compiler_params=_cparams(2), name="adaln_mod",
    )(cvec, mod_w, mod_b.reshape(depth, 1, n))


def _norm_mod_kernel(*refs, two_inputs, n_lat_tiles, d):
    if two_inputs:
        x_ref, c_ref, m_ref, g_ref, o_ref = refs
    else:
        x_ref, m_ref, g_ref, o_ref = refs
    is_ctx = pl.program_id(0) == n_lat_tiles
    xt = x_ref[...]
    if two_inputs:
        xt = jnp.where(is_ctx, c_ref[...], xt)
    y = xt * lax.rsqrt(jnp.mean(xt * xt, axis=-1, keepdims=True) + NORM_EPS) * g_ref[...]
    m = m_ref[...]
    shift = jnp.where(is_ctx, m[1:2, :d], m[0:1, :d])
    scale = jnp.where(is_ctx, m[1:2, d:2 * d], m[0:1, d:2 * d])
    o_ref[...] = (y * (1.0 + scale) + shift).astype(BF16)


def _norm_mod_call(x_lat, x_ctx, mods, g, n_lat_tiles):
    d = x_lat.shape[1]
    nt = n_lat_tiles + 1
    two = x_ctx is not None
    in_specs = [pl.BlockSpec((ROW_TILE, d), (lambda i: (jnp.minimum(i, n_lat_tiles - 1), 0)) if two
                             else (lambda i: (i, 0)))]
    args = [x_lat]
    if two:
        in_specs.append(pl.BlockSpec((ROW_TILE, d), lambda i: (0, 0)))
        args.append(x_ctx)
    in_specs += [pl.BlockSpec((8, 3 * d), lambda i: (0, 0)), pl.BlockSpec((1, d), lambda i: (0, 0))]
    args += [mods, g.reshape(1, d)]
    return pl.pallas_call(
        functools.partial(_norm_mod_kernel, two_inputs=two, n_lat_tiles=n_lat_tiles, d=d),
        grid=(nt,), in_specs=in_specs,
        out_specs=pl.BlockSpec((ROW_TILE, d), lambda i: (i, 0)),
        out_shape=jax.ShapeDtypeStruct((nt * ROW_TILE, d), BF16),
        compiler_params=_cparams(1), name="norm_mod",
    )(*args)


def _mm_kernel(a_ref, w_ref, o_ref):
    o_ref[...] = _dot(a_ref[...], w_ref[...])


def _row_tile(n_tiles, max_mult):
    k = max(m for m in range(1, max_mult + 1) if n_tiles % m == 0)
    return k * ROW_TILE


def _mm_call(a, w, tn, name):
    m, k = a.shape
    n = w.shape[1]
    tm = _row_tile(m // ROW_TILE, 5)
    return pl.pallas_call(
        _mm_kernel, grid=(n // tn, m // tm),
        in_specs=[pl.BlockSpec((tm, k), lambda j, i: (i, 0)),
                  pl.BlockSpec((k, tn), lambda j, i: (0, j))],
        out_specs=pl.BlockSpec((tm, tn), lambda j, i: (i, j)),
        out_shape=jax.ShapeDtypeStruct((m, n), F32),
        compiler_params=_cparams(2), name=name,
    )(a, w)


def _row_shift(x, k, row, prev_row, next0, next1):
    n = x.shape[0]
    if k == -1:
        return jnp.where(row == 0, prev_row, pltpu.roll(x, 1, 0))
    if k == 1:
        return jnp.where(row == n - 1, next0, pltpu.roll(x, n - 1, 0))
    r = pltpu.roll(x, n - 2, 0)
    return jnp.where(row == n - 2, next0, jnp.where(row == n - 1, next1, r))


GDN_A_LANE = 2 * GLA_RANK
GDN_B_LANE = GDN_A_LANE + 2 * GDN_HEADS


def _ab_prep_kernel(x_ref, p_ref, n_ref, w_ref, z_ref, wg_ref, gb_ref, gp_ref,
                    o_ref, cum_ref, gates_ref, gt_ref, *, n_lat_tiles):
    zs = z_ref[...]
    logit = _dot(zs, wg_ref[...], HI) + gb_ref[...]
    g = -_softplus(-logit) / GLA_GATE_NORM
    log_a = -jnp.exp(gp_ref[0:1, :]) * _softplus(zs + gp_ref[1:2, :])
    beta = jax.nn.sigmoid(zs)
    lower, _, _ = _tri_masks(True)
    upper, _, _ = _tri_masks(False)
    lower, upper = lower.astype(F32), upper.astype(F32)
    half = cum_ref.shape[1] // 2
    lane = lax.broadcasted_iota(jnp.int32, (CHUNK, LANE), 1)
    lane_t = lax.broadcasted_iota(jnp.int32, (LANE, CHUNK), 0)
    bwd_lo, bwd_hi = GDN_A_LANE + GDN_HEADS, GDN_B_LANE
    for c in range(ROW_TILE // CHUNK):
        sl = slice(c * CHUNK, (c + 1) * CHUNK)
        cum_ref[sl, :half] = _dot(lower, g[sl, :half], HI)
        cum_ref[sl, half:] = _dot(upper, g[sl, half:], HI)
        cum_f = _dot(lower, log_a[sl], HI)
        cum_b = _dot(upper, log_a[sl], HI)
        is_b = jnp.logical_and(lane >= bwd_lo, lane < bwd_hi)
        gates_ref[sl, :] = jnp.where(lane >= GDN_B_LANE, beta[sl], jnp.where(is_b, cum_b, cum_f))
        is_bt = jnp.logical_and(lane_t >= bwd_lo, lane_t < bwd_hi)
        gt_ref[c * LANE:(c + 1) * LANE, :CHUNK] = jnp.where(is_bt, cum_b.T, cum_f.T)
        gt_ref[c * LANE:(c + 1) * LANE, CHUNK:] = jnp.zeros((LANE, LANE - CHUNK), F32)

    i = pl.program_id(0)
    has_prev = jnp.logical_and(i != 0, i != n_lat_tiles)
    has_next = i < n_lat_tiles - 1
    x = x_ref[...]
    row = lax.broadcasted_iota(jnp.int32, x.shape, 0)
    prev_row = jnp.where(has_prev, p_ref[7:8, :], 0.0)
    next0 = jnp.where(has_next, n_ref[0:1, :], 0.0)
    next1 = jnp.where(has_next, n_ref[1:2, :], 0.0)
    w = w_ref[...]
    y = (w[0:1] * _row_shift(x, -1, row, prev_row, next0, next1) + w[1:2] * x
         + w[2:3] * _row_shift(x, 1, row, prev_row, next0, next1)
         + w[3:4] * _row_shift(x, 2, row, prev_row, next0, next1))
    y = _silu(y)
    qk = GDN_HEADS * GDN_DK
    for h in range(2 * GDN_HEADS):
        seg = y[:, h * GDN_DK:(h + 1) * GDN_DK]
        seg = seg * lax.rsqrt(jnp.sum(seg * seg, axis=-1, keepdims=True) + NORM_EPS)
        if h < GDN_HEADS:
            seg = seg * (GDN_DK ** -0.5)
        o_ref[:, h * GDN_DK:(h + 1) * GDN_DK] = seg
    o_ref[:, 2 * qk:] = y[:, 2 * qk:]


def _ab_prep_call(zab, zsm, conv_w, wg, gbias, gparams, n_lat_tiles):
    nt = n_lat_tiles + 1
    w = 3 * GDN_HEADS * GDN_DK
    gw = wg.shape[1]
    hb = ROW_TILE // 8
    last8 = nt * hb - 1
    nsub = ROW_TILE // CHUNK
    rows = nt * ROW_TILE
    const = lambda i: (0, 0)
    return pl.pallas_call(
        functools.partial(_ab_prep_kernel, n_lat_tiles=n_lat_tiles), grid=(nt,),
        in_specs=[pl.BlockSpec((ROW_TILE, w), lambda i: (i, 1)),
                  pl.BlockSpec((8, w), lambda i: (jnp.maximum(i * hb - 1, 0), 1)),
                  pl.BlockSpec((8, w), lambda i: (jnp.minimum((i + 1) * hb, last8), 1)),
                  pl.BlockSpec((4, w), const),
                  pl.BlockSpec((ROW_TILE, LANE), lambda i: (i, 0)),
                  pl.BlockSpec((LANE, gw), const), pl.BlockSpec((1, gw), const),
                  pl.BlockSpec((8, LANE), const)],
        out_specs=[pl.BlockSpec((ROW_TILE, w), lambda i: (i, 0)),
                   pl.BlockSpec((ROW_TILE, gw), lambda i: (i, 0)),
                   pl.BlockSpec((ROW_TILE, LANE), lambda i: (i, 0)),
                   pl.BlockSpec((nsub * LANE, LANE), lambda i: (i, 0))],
        out_shape=[jax.ShapeDtypeStruct((rows, w), F32), jax.ShapeDtypeStruct((rows, gw), F32),
                   jax.ShapeDtypeStruct((rows, LANE), F32),
                   jax.ShapeDtypeStruct((nt * nsub * LANE, LANE), F32)],
        compiler_params=_cparams(1), name="ab_prep",
    )(zab, zab, zab, conv_w, zsm, wg, gbias, gparams)


def _tile_fwd(s, n_lat_tiles):
    return jnp.where(s == 0, n_lat_tiles, s - 1)


def _tile_bwd(s, n_lat_tiles):
    return jnp.where(s == 0, n_lat_tiles, n_lat_tiles - s)


def _tri_masks(forward):
    row = lax.broadcasted_iota(jnp.int32, (CHUNK, CHUNK), 0)
    col = lax.broadcasted_iota(jnp.int32, (CHUNK, CHUNK), 1)
    incl = (col <= row) if forward else (col >= row)
    strict = (col < row) if forward else (col > row)
    return incl, strict, (row == col).astype(F32)


def _gla_kernel(qf, kf, vf, cf, qb, kb, vb, cb, of, ob, st_ref):
    @pl.when(pl.program_id(0) == 0)
    def _():
        st_ref[...] = jnp.zeros_like(st_ref)

    nsub = ROW_TILE // CHUNK
    dirs = ((qf, kf, vf, cf, of), (qb, kb, vb, cb, ob))
    for d, (q_ref, k_ref, v_ref, c_ref, o_ref) in enumerate(dirs):
        forward = d == 0
        incl, _, _ = _tri_masks(forward)
        last = CHUNK - 1 if forward else 0
        for c in (range(nsub) if forward else reversed(range(nsub))):
            sl = slice(c * CHUNK, (c + 1) * CHUNK)
            heads = lambda ref, w: jnp.stack([ref[sl, h * w:(h + 1) * w] for h in range(GLA_HEADS)])
            q, k, v, cum = heads(q_ref, GLA_DK), heads(k_ref, GLA_DK), heads(v_ref, GLA_DV), heads(c_ref, GLA_DK)
            tot = cum[:, last:last + 1]
            qd = (q * (GLA_DK ** -0.5) * jnp.exp(cum)).astype(BF16)
            ki = (k * jnp.exp(-cum)).astype(BF16)
            ke = (k * jnp.exp(tot - cum)).astype(BF16)
            vb16 = v.astype(BF16)
            sc = jnp.where(incl, _bdot_nt(qd, ki), 0.0).astype(BF16)
            st = st_ref[d]
            o = _bdot(sc, vb16) + _bdot_nt(qd, st.astype(BF16))
            for h in range(GLA_HEADS):
                o_ref[sl, h * GLA_DV:(h + 1) * GLA_DV] = o[h]
            st_ref[d] = jnp.exp(tot) * st + _bdot_tn(vb16, ke)


def _gla_call(zab, gla_cum, n_lat_tiles):
    nt = n_lat_tiles + 1
    tf = functools.partial(_tile_fwd, n_lat_tiles=n_lat_tiles)
    tb = functools.partial(_tile_bwd, n_lat_tiles=n_lat_tiles)
    qk_w, v_w = GLA_HEADS * GLA_DK, GLA_HEADS * GLA_DV

    def dir_specs(t, d):
        return [pl.BlockSpec((ROW_TILE, qk_w), lambda s: (t(s), 0)),
                pl.BlockSpec((ROW_TILE, qk_w), lambda s: (t(s), 1)),
                pl.BlockSpec((ROW_TILE, v_w), lambda s: (t(s), 1)),
                pl.BlockSpec((ROW_TILE, qk_w), lambda s: (t(s), d))]

    out_sds = jax.ShapeDtypeStruct((nt * ROW_TILE, v_w), F32)
    return pl.pallas_call(
        _gla_kernel, grid=(nt,),
        in_specs=dir_specs(tf, 0) + dir_specs(tb, 1),
        out_specs=[pl.BlockSpec((ROW_TILE, v_w), lambda s: (tf(s), 0)),
                   pl.BlockSpec((ROW_TILE, v_w), lambda s: (tb(s), 0))],
        out_shape=[out_sds, out_sds],
        scratch_shapes=[pltpu.VMEM((2, GLA_HEADS, GLA_DV, GLA_DK), F32)],
        compiler_params=_cparams(1), name="gla_scan",
    )(zab, zab, zab, gla_cum, zab, zab, zab, gla_cum)


INV_BASE = 8


def _unit_triangular_inverse_minus_eye(a):
    n = a.shape[-1]
    row = lax.broadcasted_iota(jnp.int32, (n, n), 0)
    col = lax.broadcasted_iota(jnp.int32, (n, n), 1)

    def same_block(b):
        s = int(math.log2(b))
        return lax.shift_right_logical(row, s) == lax.shift_right_logical(col, s)

    bdot = lambda p, r: _bdot(p.astype(BF16), r.astype(BF16))
    q = jnp.where(same_block(INV_BASE), a, 0.0)
    x = bdot(q, q)
    m = 2
    while 2 * m < INV_BASE:
        r = bdot(jnp.concatenate([q, x], axis=1), x)
        q, x = q + x + r[:, :n], r[:, n:]
        m *= 2
    q = q + x + bdot(q, x)
    b = INV_BASE
    while b < n:
        a_b = jnp.where(jnp.logical_and(same_block(2 * b), jnp.logical_not(same_block(b))), a, 0.0)
        y = a_b + bdot(a_b, q)
        q = q + y + bdot(q, y)
        b *= 2
    return q


def _gdn_kernel(xf, gf, tf_ref, xb, gb, tb_ref, of, ob, s_ref):
    @pl.when(pl.program_id(0) == 0)
    def _():
        s_ref[...] = jnp.zeros_like(s_ref)

    nsub = ROW_TILE // CHUNK
    qk_w = GDN_HEADS * GDN_DK
    dirs = ((xf, gf, tf_ref, of), (xb, gb, tb_ref, ob))
    for d, (x_ref, g_ref, t_ref, o_ref) in enumerate(dirs):
        forward = d == 0
        incl, strict, _ = _tri_masks(forward)
        last = CHUNK - 1 if forward else 0
        for c in (range(nsub) if forward else reversed(range(nsub))):
            sl = slice(c * CHUNK, (c + 1) * CHUNK)
            hs = range(GDN_HEADS)
            a0 = GDN_A_LANE + d * GDN_HEADS
            b0 = GDN_B_LANE + d * GDN_HEADS
            heads = lambda off: jnp.stack([x_ref[sl, off + h * GDN_DK:off + (h + 1) * GDN_DK] for h in hs])
            q, k, v = heads(0), heads(qk_w), heads(2 * qk_w)
            cum = jnp.stack([g_ref[sl, a0 + h:a0 + h + 1] for h in hs])
            beta = jnp.stack([g_ref[sl, b0 + h:b0 + h + 1] for h in hs])
            cum_row = jnp.stack([t_ref[c * LANE + a0 + h:c * LANE + a0 + h + 1, :CHUNK] for h in hs])
            tot = cum[:, last:last + 1]
            decay = jnp.exp(jnp.where(incl, cum - cum_row, -jnp.inf))
            k16 = k.astype(BF16)
            qk_kk = _bdot_nt(jnp.concatenate([q, k], axis=1).astype(BF16), k16)
            a = jnp.where(strict, -(qk_kk[:, CHUNK:] * decay * beta), 0.0)
            t_m1 = _unit_triangular_inverse_minus_eye(a)
            e_cum = jnp.exp(cum)
            rhs = jnp.concatenate([v * beta, k * (beta * e_cum)], axis=2)
            uw = rhs + _bdot(t_m1.astype(BF16), rhs.astype(BF16))
            u, w = uw[:, :, :GDN_DK], uw[:, :, GDN_DK:]
            sc = jnp.where(incl, qk_kk[:, :CHUNK] * decay, 0.0).astype(BF16)
            ke = (k * jnp.exp(tot - cum)).astype(BF16)
            s = s_ref[d]
            ws_qs = _bdot(jnp.concatenate([w, q * e_cum], axis=1).astype(BF16), s.astype(BF16))
            vn16 = (u - ws_qs[:, :CHUNK]).astype(BF16)
            o = ws_qs[:, CHUNK:] + _bdot(sc, vn16)
            for h in hs:
                o_ref[sl, h * GDN_DK:(h + 1) * GDN_DK] = o[h]
            s_ref[d] = jnp.exp(tot) * s + _bdot_tn(ke, vn16)


def _gdn_call(qkv, gates, gates_t, n_lat_tiles):
    nt = n_lat_tiles + 1
    tf = functools.partial(_tile_fwd, n_lat_tiles=n_lat_tiles)
    tb = functools.partial(_tile_bwd, n_lat_tiles=n_lat_tiles)
    w = qkv.shape[1]
    out_w = GDN_HEADS * GDN_DK
    tl = (ROW_TILE // CHUNK) * LANE

    def dir_specs(t):
        return [pl.BlockSpec((ROW_TILE, w), lambda s: (t(s), 0)),
                pl.BlockSpec((ROW_TILE, LANE), lambda s: (t(s), 0)),
                pl.BlockSpec((tl, LANE), lambda s: (t(s), 0))]

    out_sds = jax.ShapeDtypeStruct((nt * ROW_TILE, out_w), F32)
    return pl.pallas_call(
        _gdn_kernel, grid=(nt,),
        in_specs=dir_specs(tf) + dir_specs(tb),
        out_specs=[pl.BlockSpec((ROW_TILE, out_w), lambda s: (tf(s), 0)),
                   pl.BlockSpec((ROW_TILE, out_w), lambda s: (tb(s), 0))],
        out_shape=[out_sds, out_sds],
        scratch_shapes=[pltpu.VMEM((2, GDN_HEADS, GDN_DK, GDN_DK), F32)],
        compiler_params=_cparams(1), name="gdn_scan",
    )(qkv, gates, gates_t, qkv, gates, gates_t)


def _merge_ab_kernel(gf, gb, df, db, gz, dz, gn, dn, o_ref):
    gla_w = GLA_HEADS * GLA_DV
    og = gf[...] + gb[...]
    zg = gz[...]
    for h in range(GLA_HEADS):
        sl = slice(h * GLA_DV, (h + 1) * GLA_DV)
        seg = og[:, sl]
        y = seg * lax.rsqrt(jnp.mean(seg * seg, axis=-1, keepdims=True) + NORM_EPS) * gn[...]
        o_ref[:, sl] = (y * _silu(zg[:, sl])).astype(BF16)
    od = df[...] + db[...]
    zd = dz[...]
    for h in range(GDN_HEADS):
        sl = slice(h * GDN_DK, (h + 1) * GDN_DK)
        seg = od[:, sl]
        y = seg * lax.rsqrt(jnp.mean(seg * seg, axis=-1, keepdims=True) + NORM_EPS) * dn[...]
        o_ref[:, gla_w + h * GDN_DK:gla_w + (h + 1) * GDN_DK] = (y * _silu(zd[:, sl])).astype(BF16)


def _out0_kernel(gf, gb, df, db, gz, dz, gn, dn, w_ref, x_ref, c_ref, m_ref, o_ref, a_scr,
                 *, n_lat_tiles, d):
    is_ctx = pl.program_id(0) == n_lat_tiles
    _merge_ab_kernel(gf, gb, df, db, gz, dz, gn, dn, a_scr)
    acc = _dot(a_scr[...], w_ref[...])
    m = m_ref[...]
    gate = jnp.where(is_ctx, m[1:2, 2 * d:], m[0:1, 2 * d:])
    o_ref[...] = jnp.where(is_ctx, c_ref[...], x_ref[...]) + gate * acc


def _out0_call(gla_f, gla_b, gdn_f, gdn_b, zab, gla_norm, gdn_norm, w_out, x_lat, x_ctx, mods, n_lat_tiles):
    m = gla_f.shape[0]
    w = GLA_HEADS * GLA_DV
    k, d = w_out.shape
    row = lambda i: (i, 0)
    const = lambda i: (0, 0)
    return pl.pallas_call(
        functools.partial(_out0_kernel, n_lat_tiles=n_lat_tiles, d=d), grid=(m // ROW_TILE,),
        in_specs=[pl.BlockSpec((ROW_TILE, w), row)] * 4
        + [pl.BlockSpec((ROW_TILE, w), lambda i: (i, 2)), pl.BlockSpec((ROW_TILE, w), lambda i: (i, 6)),
           pl.BlockSpec((1, GLA_DV), const), pl.BlockSpec((1, GDN_DK), const),
           pl.BlockSpec((k, d), const),
           pl.BlockSpec((ROW_TILE, d), lambda i: (jnp.minimum(i, n_lat_tiles - 1), 0)),
           pl.BlockSpec((ROW_TILE, d), const),
           pl.BlockSpec((8, 3 * d), const)],
        out_specs=pl.BlockSpec((ROW_TILE, d), row),
        out_shape=jax.ShapeDtypeStruct((m, d), F32),
        scratch_shapes=[pltpu.VMEM((ROW_TILE, k), BF16)],
        compiler_params=_cparams(1), name="out_proj_ab",
    )(gla_f, gla_b, gdn_f, gdn_b, zab, zab, gla_norm.reshape(1, -1), gdn_norm.reshape(1, -1),
      w_out, x_lat, x_ctx, mods)


def _lru_gates(seg, wa, ba, wx, bx, lam):
    seg16 = seg.astype(BF16)
    r = jax.nn.sigmoid(_dot(seg16, wa) + ba)
    gi = jax.nn.sigmoid(_dot(seg16, wx) + bx)
    log_a = -LRU_C * _softplus(-lam) * r
    a = jnp.exp(log_a)
    return a, jnp.sqrt(jnp.tanh(-log_a) * (a * a + 1.0)) * (gi * seg)


def _lru_ctx_kernel(x_ref, cw_ref, cb_ref, wa_ref, ba_ref, wx_ref, bx_ref, lam_ref, o_ref, a_scr, d_scr):
    x = x_ref[...]
    rows, width = x.shape
    blk = width // LRU_BLOCKS
    cw = cw_ref[...]
    row = lax.broadcasted_iota(jnp.int32, x.shape, 0)
    zero = jnp.zeros((1, width), F32)
    xc = (cw[0:1] * _row_shift(x, -1, row, zero, zero, zero) + cw[1:2] * x
          + cw[2:3] * _row_shift(x, 1, row, zero, zero, zero)
          + cw[3:4] * _row_shift(x, 2, row, zero, zero, zero) + cb_ref[...])
    for d in range(2):
        for b in range(LRU_BLOCKS):
            sl = slice(b * blk, (b + 1) * blk)
            a, drv = _lru_gates(xc[:, sl], wa_ref[d, b], ba_ref[d, :, sl], wx_ref[d, b],
                                bx_ref[d, :, sl], lam_ref[d, :, sl])
            a_scr[d, :, sl] = a
            d_scr[d, :, sl] = drv

    def body(t, carry):
        hf, hb = carry
        hf = a_scr[0, pl.ds(t, 1), :] * hf + d_scr[0, pl.ds(t, 1), :]
        tb = rows - 1 - t
        hb = a_scr[1, pl.ds(tb, 1), :] * hb + d_scr[1, pl.ds(tb, 1), :]
        return hf, hb

    hf, hb = lax.fori_loop(0, rows, body, (zero, zero), unroll=8)
    o_ref[...] = jnp.zeros_like(o_ref)
    o_ref[0:1, :] = hf
    o_ref[1:2, :] = hb


def _lru_lat_kernel(xf, pf, nf, xb, pb, nb, h0_ref, cw_ref, cb_ref, wa_ref, ba_ref, wx_ref, bx_ref,
                    lam_ref, of, ob, a_scr, d_scr, hl_scr, pc_scr, carry, *, n_colblocks):
    ci, s = pl.program_id(0), pl.program_id(1)

    @pl.when(s == 0)
    def _():
        carry[...] = h0_ref[...]

    cw = cw_ref[...]
    dirs = ((xf, pf, nf, of, s), (xb, pb, nb, ob, n_colblocks - 1 - s))
    for d, (x_ref, p_ref, n_ref, o_ref, cblk) in enumerate(dirs):
        forward = d == 0
        x3 = x_ref[...]
        rows, ncol, tc = x3.shape
        has_prev, has_next = cblk > 0, cblk < n_colblocks - 1
        sub = lax.broadcasted_iota(jnp.int32, (ncol, tc), 0)
        prev_last = jnp.where(has_prev, p_ref[7, ncol - 1:ncol, :], 0.0)
        next_r0 = jnp.where(has_next, n_ref[0, 0:1, :], 0.0)
        next_r1 = jnp.where(has_next, n_ref[1, 0:1, :], 0.0)
        first_m1 = jnp.where(sub == 0, prev_last, pltpu.roll(x3[rows - 1], 1, 0))
        last_p1 = jnp.where(sub == ncol - 1, next_r0, pltpu.roll(x3[0], ncol - 1, 0))
        last_p2 = jnp.where(sub == ncol - 1, next_r1, pltpu.roll(x3[1], ncol - 1, 0))
        xm1 = jnp.concatenate([first_m1[None], x3[:rows - 1]], axis=0)
        xp1 = jnp.concatenate([x3[1:], last_p1[None]], axis=0)
        xp2 = jnp.concatenate([x3[2:], last_p1[None], last_p2[None]], axis=0)
        xc = cw[0:1] * xm1 + cw[1:2] * x3 + cw[2:3] * xp1 + cw[3:4] * xp2 + cb_ref[...]
        a, drv = _lru_gates(xc.reshape(rows * ncol, tc), wa_ref[d, ci], ba_ref[d], wx_ref[d, ci],
                            bx_ref[d], lam_ref[d])
        a_scr[d] = a.reshape(rows, ncol, tc)
        d_scr[d] = drv.reshape(rows, ncol, tc)

        def body(i, hp):
            h, p = hp
            r = i if forward else rows - 1 - i
            ar = a_scr[d, r]
            h = ar * h + d_scr[d, r]
            p = ar * p
            hl_scr[d, r] = h
            pc_scr[d, r] = p
            return h, p

        lax.fori_loop(0, rows, body, (jnp.zeros((ncol, tc), F32), jnp.ones((ncol, tc), F32)), unroll=8)

        end = rows - 1 if forward else 0
        hl_end, pc_end = hl_scr[d, end], pc_scr[d, end]
        h = carry[d:d + 1, :]
        h_in = jnp.zeros((ncol, tc), F32)
        for j in (range(ncol) if forward else reversed(range(ncol))):
            h_in = jnp.where(sub == j, h, h_in)
            h = hl_end[j:j + 1, :] + pc_end[j:j + 1, :] * h
        carry[d:d + 1, :] = h
        o_ref[...] = hl_scr[d] + pc_scr[d] * h_in[None]


def _lru_call(zcd, conv_w, conv_b, wa, ba, wx, bx, lam, n_lat):
    m, ncols_z = zcd.shape
    width = conv_w.shape[1]
    n_cols = GRID_W
    rows = n_lat // n_cols
    tc = width // LRU_BLOCKS
    ncb = n_cols // 8
    full = lambda a: pl.BlockSpec(a.shape, lambda *_: (0,) * a.ndim)
    wa16, wx16 = wa.astype(BF16), wx.astype(BF16)
    cb2, ba3, bx3, lam3 = (conv_b.reshape(1, width), ba.reshape(2, 1, width), bx.reshape(2, 1, width),
                           lam.reshape(2, 1, width))

    params = (conv_w, cb2, wa16, ba3, wx16, bx3, lam3)
    h0 = pl.pallas_call(
        _lru_ctx_kernel, grid=(1,),
        in_specs=[pl.BlockSpec((ROW_TILE, width), lambda i: (n_lat // ROW_TILE, 0))] + [full(p) for p in params],
        out_specs=pl.BlockSpec((8, width), lambda i: (0, 0)),
        out_shape=jax.ShapeDtypeStruct((8, width), F32),
        scratch_shapes=[pltpu.VMEM((2, ROW_TILE, width), F32)] * 2,
        compiler_params=_cparams(1), name="rglru_ctx",
    )(zcd, *params)

    z3 = zcd.reshape(m // n_cols, n_cols, ncols_z)
    hb8 = rows // 8
    bf = lambda s: s
    bb = lambda s: ncb - 1 - s

    def dir_specs(blk):
        return [pl.BlockSpec((rows, 8, tc), lambda c, s: (0, blk(s), c)),
                pl.BlockSpec((8, 8, tc), lambda c, s: (hb8 - 1, jnp.maximum(blk(s) - 1, 0), c)),
                pl.BlockSpec((8, 8, tc), lambda c, s: (0, jnp.minimum(blk(s) + 1, ncb - 1), c))]

    vec = lambda lead: pl.BlockSpec((lead, 1, tc), lambda c, s: (0, 0, c))
    out_sds = jax.ShapeDtypeStruct((rows, n_cols, width), F32)
    hf, hb = pl.pallas_call(
        functools.partial(_lru_lat_kernel, n_colblocks=ncb), grid=(width // tc, ncb),
        in_specs=dir_specs(bf) + dir_specs(bb)
        + [pl.BlockSpec((8, tc), lambda c, s: (0, c)),
           pl.BlockSpec((4, tc), lambda c, s: (0, c)), pl.BlockSpec((1, tc), lambda c, s: (0, c)),
           full(wa16), vec(2), full(wx16), vec(2), vec(2)],
        out_specs=[pl.BlockSpec((rows, 8, tc), lambda c, s: (0, bf(s), c)),
                   pl.BlockSpec((rows, 8, tc), lambda c, s: (0, bb(s), c))],
        out_shape=[out_sds, out_sds],
        scratch_shapes=[pltpu.VMEM((2, rows, 8, tc), F32)] * 4 + [pltpu.VMEM((8, tc), F32)],
        compiler_params=_cparams(2), name="rglru_scan",
    )(z3, z3, z3, z3, z3, z3, h0, conv_w, cb2, wa16, ba3, wx16, bx3, lam3)
    return hf.reshape(n_lat, width), hb.reshape(n_lat, width)


def _hy_prep_kernel(*refs, n_lat_tiles):
    xs, (w_ref, b_ref, x0_ref, z_ref) = refs[:9], refs[9:]
    i = pl.program_id(0)
    has_prev = i > 0
    has_next = i < n_lat_tiles - 1
    outs = []
    width = x0_ref.shape[1]
    for j in range(3):
        x_ref, p_ref, n_ref = xs[3 * j:3 * j + 3]
        x = x_ref[...]
        row = lax.broadcasted_iota(jnp.int32, x.shape, 0)
        prev_row = jnp.where(has_prev, p_ref[7:8, :], 0.0)
        next0 = jnp.where(has_next, n_ref[0:1, :], 0.0)
        sl = slice(j * width, (j + 1) * width)
        outs.append(w_ref[0:1, sl] * _row_shift(x, -1, row, prev_row, next0, next0)
                    + w_ref[1:2, sl] * x
                    + w_ref[2:3, sl] * _row_shift(x, 1, row, prev_row, next0, next0)
                    + b_ref[:, sl])
    x0_ref[...] = outs[0]
    z_ref[...] = outs[1] * outs[2]


def _hy_prep_call(zcd, conv_w, conv_b, n_lat_tiles):
    width = conv_w.shape[1] // 3
    hb = ROW_TILE // 8
    last8 = n_lat_tiles * hb - 1
    in_specs, args = [], []
    for j in range(3):
        cbk = 2 + j
        in_specs += [pl.BlockSpec((ROW_TILE, width), lambda i, cbk=cbk: (i, cbk)),
                     pl.BlockSpec((8, width), lambda i, cbk=cbk: (jnp.maximum(i * hb - 1, 0), cbk)),
                     pl.BlockSpec((8, width), lambda i, cbk=cbk: (jnp.minimum((i + 1) * hb, last8), cbk))]
        args += [zcd, zcd, zcd]
    in_specs += [pl.BlockSpec((3, 3 * width), lambda i: (0, 0)), pl.BlockSpec((1, 3 * width), lambda i: (0, 0))]
    out_sds = jax.ShapeDtypeStruct((n_lat_tiles * ROW_TILE, width), F32)
    return pl.pallas_call(
        functools.partial(_hy_prep_kernel, n_lat_tiles=n_lat_tiles), grid=(n_lat_tiles,),
        in_specs=in_specs,
        out_specs=[pl.BlockSpec((ROW_TILE, width), lambda i: (i, 0))] * 2,
        out_shape=[out_sds, out_sds],
        compiler_params=_cparams(1), name="hyena_prep",
    )(*args, conv_w, conv_b.reshape(1, -1))


def _hy_filter_kernel(e_ref, w1, b1, w2, b2, w3, fr, dec, f_ref, ssq_ref):
    i = pl.program_id(0)
    e = e_ref[...]
    hid = jnp.sin(fr[...] * (_dot(e, w1[...], HI) + b1[...]))
    hid = jnp.sin(fr[...] * (_dot(hid, w2[...], HI) + b2[...]))
    filt = _dot(hid.astype(BF16), w3[...]) * jnp.exp(-e[:, 0:1] * dec[...])
    half = filt.shape[1] // 2
    row = lax.broadcasted_iota(jnp.int32, filt.shape, 0)
    col = lax.broadcasted_iota(jnp.int32, filt.shape, 1)
    unused = jnp.logical_and(jnp.logical_and(i == 0, row == 0), col >= half)
    filt = jnp.where(unused, 0.0, filt)
    f_ref[...] = filt

    @pl.when(i == 0)
    def _():
        ssq_ref[...] = jnp.zeros_like(ssq_ref)

    ssq_ref[...] = ssq_ref[...] + jnp.sum(filt * filt, axis=0, keepdims=True)


def _hy_filter_call(emb, w1, b1, w2, b2, w3, freq, decay):
    length = emb.shape[0]
    n_out = w3.shape[1]
    hid = w1.shape[1]
    pad_r = lambda a: jnp.pad(a, ((0, LANE - a.shape[0]), (0, 0)))
    pad_c = lambda a: jnp.pad(a, ((0, 0), (0, LANE - a.shape[1])))
    params = (pad_c(pad_r(w1)), pad_c(b1.reshape(1, hid)), pad_c(pad_r(w2)), pad_c(b2.reshape(1, hid)),
              pad_r(w3).astype(BF16), pad_c(freq.reshape(1, hid)), decay.reshape(1, n_out))
    full = lambda a: pl.BlockSpec(a.shape, lambda i: (0,) * a.ndim)
    return pl.pallas_call(
        _hy_filter_kernel, grid=(length // ROW_TILE,),
        in_specs=[pl.BlockSpec((ROW_TILE, LANE), lambda i: (i, 0))] + [full(p) for p in params],
        out_specs=[pl.BlockSpec((ROW_TILE, n_out), lambda i: (i, 0)),
                   pl.BlockSpec((8, n_out), lambda i: (0, 0))],
        out_shape=[jax.ShapeDtypeStruct((length, n_out), F32), jax.ShapeDtypeStruct((8, n_out), F32)],
        compiler_params=_cparams(1), name="hyena_filter",
    )(emb, *params)


SUB = 8


def _dft_a_kernel(x_ref, c_ref, s_ref, re_ref, im_ref):
    for j in range(SUB):
        xb = x_ref[:, j, :].astype(BF16)
        re_ref[:, j, :] = _dot(c_ref[...], xb)
        im_ref[:, j, :] = _dot(s_ref[...], xb)


def _dft_a_call(x, cmat, smat, n1):
    length, ch = x.shape
    n2, n2h = cmat.shape
    tc = 512
    x4 = x.reshape(n2h, n1 // SUB, SUB, ch)
    out_sds = jax.ShapeDtypeStruct((n2, n1 // SUB, SUB, ch), F32)
    re, im = pl.pallas_call(
        _dft_a_kernel, grid=(ch // tc, n1 // SUB),
        in_specs=[pl.BlockSpec((n2h, None, SUB, tc), lambda c, i: (0, i, 0, c)),
                  pl.BlockSpec((n2, n2h), lambda c, i: (0, 0)),
                  pl.BlockSpec((n2, n2h), lambda c, i: (0, 0))],
        out_specs=[pl.BlockSpec((n2, None, SUB, tc), lambda c, i: (0, i, 0, c))] * 2,
        out_shape=[out_sds, out_sds],
        compiler_params=_cparams(2), name="dft_stage_a",
    )(x4, cmat, smat)
    return re.reshape(n2, n1, ch), im.reshape(n2, n1, ch)


def _dft_b_filter_kernel(pre, pim, fre, fim, g_ref, sp_ref, sf_ref, hre_ref, him_ref, *, n1, inv_n):
    scale = lax.rsqrt(sp_ref[0:1, :] + sf_ref[0:1, :] + NORM_EPS) * inv_n
    for j in range(SUB):
        g = g_ref[j]
        xp = _dot(g, jnp.concatenate([pre[j], pim[j]], axis=0).astype(BF16))
        xf = _dot(g, jnp.concatenate([fre[j], fim[j]], axis=0).astype(BF16))
        hre_ref[j] = (xp[:n1] + xf[:n1]) * scale
        him_ref[j] = (xp[n1:] - xf[n1:]) * scale


def _dft_b_filter_call(a_re, a_im, gmat, ssq):
    n2, n1, ch2 = a_re.shape
    ch = ch2 // 2
    tc = 256
    nct = ch // tc
    spec = lambda half: pl.BlockSpec((SUB, n1, tc), lambda k, c: (k, 0, half * nct + c))
    out_sds = jax.ShapeDtypeStruct((n2, n1, ch), F32)
    return pl.pallas_call(
        functools.partial(_dft_b_filter_kernel, n1=n1, inv_n=1.0 / (n1 * n2)), grid=(n2 // SUB, nct),
        in_specs=[spec(0), spec(0), spec(1), spec(1),
                  pl.BlockSpec((SUB, 2 * n1, 2 * n1), lambda k, c: (k, 0, 0)),
                  pl.BlockSpec((8, tc), lambda k, c: (0, c)),
                  pl.BlockSpec((8, tc), lambda k, c: (0, nct + c))],
        out_specs=[pl.BlockSpec((SUB, n1, tc), lambda k, c: (k, 0, c))] * 2,
        out_shape=[out_sds, out_sds],
        compiler_params=_cparams(2), name="dft_stage_b_filter",
    )(a_re, a_im, a_re, a_im, gmat, ssq, ssq)


def _dft_b_kernel(are, aim, hre, him, g_ref, gi_ref, bre_ref, bim_ref, *, n1):
    for j in range(SUB):
        x = _dot(g_ref[j], jnp.concatenate([are[j], aim[j]], axis=0).astype(BF16))
        xr, xi = x[:n1], x[n1:]
        hr, hi = hre[j], him[j]
        y = jnp.concatenate([xr * hr - xi * hi, xr * hi + xi * hr], axis=0).astype(BF16)
        b = _dot(gi_ref[j], y)
        bre_ref[:, j, :] = b[:n1]
        bim_ref[:, j, :] = b[n1:]


def _dft_b_call(a_re, a_im, h_re, h_im, gmat, gimat):
    n2, n1, ch = a_re.shape
    tc = 512
    slab = pl.BlockSpec((SUB, n1, tc), lambda k, c: (k, 0, c))
    gspec = pl.BlockSpec((SUB, 2 * n1, 2 * n1), lambda k, c: (k, 0, 0))
    out_sds = jax.ShapeDtypeStruct((n1, n2 // SUB, SUB, ch), F32)
    re, im = pl.pallas_call(
        functools.partial(_dft_b_kernel, n1=n1), grid=(n2 // SUB, ch // tc),
        in_specs=[slab] * 4 + [gspec, gspec],
        out_specs=[pl.BlockSpec((n1, None, SUB, tc), lambda k, c: (0, k, 0, c))] * 2,
        out_shape=[out_sds, out_sds],
        compiler_params=_cparams(2), name="dft_stage_b",
    )(a_re, a_im, h_re, h_im, gmat, gimat)
    return re.reshape(n1, n2, ch), im.reshape(n1, n2, ch)


def _dft_c_kernel(bre, bim, x0_ref, z_ref, hz_ref, bias_ref, c_ref, s_ref, o_ref):
    for j in range(SUB):
        y = _dot(c_ref[...], bre[j].astype(BF16)) + _dot(s_ref[...], bim[j].astype(BF16))
        o_ref[:, j, :] = (x0_ref[:, j, :] * (y + z_ref[:, j, :] * bias_ref[...])
                          * _silu(hz_ref[:, j, :]))


def _dft_c_call(b_re, b_im, x0, z, zcd, bias, cmat, smat):
    n1, n2, ch = b_re.shape
    n2h = n2
    length = n1 * n2h
    tc = 256
    nct = ch // tc
    view = lambda a: a.reshape(a.shape[0] // n1, n1 // SUB, SUB, a.shape[1])
    tspec = lambda cblk: pl.BlockSpec((n2h, None, SUB, tc), lambda i, c: (0, i, 0, cblk * nct + c))
    out = pl.pallas_call(
        _dft_c_kernel, grid=(n1 // SUB, nct),
        in_specs=[pl.BlockSpec((SUB, n2, tc), lambda i, c: (i, 0, c))] * 2
        + [tspec(0), tspec(0), tspec(5), pl.BlockSpec((1, tc), lambda i, c: (0, c)),
           pl.BlockSpec((n2h, n2), lambda i, c: (0, 0)), pl.BlockSpec((n2h, n2), lambda i, c: (0, 0))],
        out_specs=tspec(0),
        out_shape=jax.ShapeDtypeStruct((n2h, n1 // SUB, SUB, ch), F32),
        compiler_params=_cparams(2), name="dft_stage_c",
    )(b_re, b_im, view(x0), view(z), view(zcd), bias.reshape(1, ch), cmat, smat)
    return out.reshape(length, ch)


def _dft_constants(n1, n2):
    n = n1 * n2
    n2h = n2 // 2
    k2 = jnp.arange(n2h, dtype=jnp.int32)
    ang_a = (math.pi / n2) * ((k2[None, :] * (2 * k2[:, None] + 1)) % (2 * n2)).astype(F32)
    ca, sa = jnp.cos(ang_a), -jnp.sin(ang_a)
    cc, sc = jnp.cos(ang_a).T, -jnp.sin(ang_a).T
    i1 = jnp.arange(n1, dtype=jnp.int32)
    ang_f = (2.0 * math.pi / n1) * ((i1[:, None] * i1[None, :]) % n1).astype(F32)
    ang_t = (math.pi / n) * ((2 * k2[:, None] + 1) * i1[None, :]).astype(F32)
    fr, fi = jnp.cos(ang_f)[None], -jnp.sin(ang_f)[None]
    wr, wi = jnp.cos(ang_t)[:, None, :], -jnp.sin(ang_t)[:, None, :]
    gr, gi = fr * wr - fi * wi, fr * wi + fi * wr
    g = jnp.concatenate([jnp.concatenate([gr, -gi], axis=2), jnp.concatenate([gi, gr], axis=2)], axis=1)
    tr, ti = jnp.swapaxes(gr, 1, 2), -jnp.swapaxes(gi, 1, 2)
    ginv = jnp.concatenate([jnp.concatenate([tr, -ti], axis=2), jnp.concatenate([ti, tr], axis=2)], axis=1)
    b = lambda a: a.astype(BF16)
    return b(ca), b(sa), b(cc), b(sc), b(g), b(ginv)


def _final_kernel(hf, hb, lz, hy, w1, w2, x_ref, m_ref, g_ref, o_ref, *, d):
    a1 = ((hf[...] + hb[...]) * _silu(lz[...])).astype(BF16)
    acc = _dot(a1, w1[...]) + _dot(hy[...].astype(BF16), w2[...])
    xn = x_ref[...] + m_ref[0:1, 2 * d:] * acc
    o_ref[...] = xn * lax.rsqrt(jnp.mean(xn * xn, axis=-1, keepdims=True) + NORM_EPS) * g_ref[...]


def _final_call(hf, hb, zcd, hy, w_out, x1, mods, final_g, n_lat_tiles):
    d = w_out.shape[1]
    half = w_out.shape[0] // 2
    row = lambda i: (i, 0)
    const = lambda i: (0, 0)
    return pl.pallas_call(
        functools.partial(_final_kernel, d=d), grid=(n_lat_tiles,),
        in_specs=[pl.BlockSpec((ROW_TILE, half), row), pl.BlockSpec((ROW_TILE, half), row),
                  pl.BlockSpec((ROW_TILE, half), lambda i: (i, 1)), pl.BlockSpec((ROW_TILE, half), row),
                  pl.BlockSpec((half, d), const), pl.BlockSpec((half, d), lambda i: (1, 0)),
                  pl.BlockSpec((ROW_TILE, d), row), pl.BlockSpec((8, 3 * d), const),
                  pl.BlockSpec((1, d), const)],
        out_specs=pl.BlockSpec((ROW_TILE, d), row),
        out_shape=jax.ShapeDtypeStruct((n_lat_tiles * ROW_TILE, d), F32),
        compiler_params=_cparams(1), name="out_proj_cd_final",
    )(hf, hb, zcd, hy, w_out, w_out, x1, mods, final_g.reshape(1, d))


def kernel(x, c, ctx, c_ctx, mod_w, mod_b, norm_g, ab_w_in, ab_gla_wg2, ab_gla_bg2, ab_gla_norm, ab_gdn_conv, ab_gdn_a_log, ab_gdn_dt_bias, ab_gdn_norm, ab_w_out, cd_w_in, cd_lru_conv_w, cd_lru_conv_b, cd_lru_wa, cd_lru_ba, cd_lru_wx, cd_lru_bx, cd_lru_lambda, cd_hy_conv_w, cd_hy_conv_b, cd_hy_w1, cd_hy_b1, cd_hy_w2, cd_hy_b2, cd_hy_w3, cd_hy_freq, cd_hy_decay, cd_hy_bias, cd_w_out, final_g):
    _, n_lat, d = x.shape
    assert x.shape[0] == 1 and ctx.shape[1] == ROW_TILE and n_lat % ROW_TILE == 0
    assert mod_w.shape[0] == 2 and d == 2 * GLA_HEADS * GLA_DV
    nlt = n_lat // ROW_TILE
    x_lat, x_ctx = x[0], ctx[0]

    cvec = jnp.zeros((8, d), F32).at[0].set(c[0]).at[1].set(c_ctx)
    mods = _mod_call(cvec, mod_w, mod_b)

    gla_qk, gla_v = GLA_HEADS * GLA_DK, GLA_HEADS * GLA_DV
    gdn_qk = GDN_HEADS * GDN_DK
    sizes = (gla_qk, gla_qk, gla_v, 2 * GLA_RANK, gla_v, 3 * gdn_qk, 2 * GDN_HEADS, 2 * GDN_HEADS, gdn_qk)
    gq, gk, gv, glr, gz, dqkv, da, db, dz = jnp.split(ab_w_in[0], np.cumsum(sizes)[:-1].tolist(), axis=1)
    w_main = jnp.concatenate([gq, gk, gv, gz, dqkv, dz], axis=1).astype(BF16)
    n_small = glr.shape[1] + da.shape[1] + db.shape[1]
    w_small = jnp.concatenate([glr, da, db, jnp.zeros((d, LANE - n_small), F32)], axis=1).astype(BF16)

    h0 = _norm_mod_call(x_lat, x_ctx, mods[0], norm_g[0], nlt)
    zab = _mm_call(h0, w_main, 1024, "in_proj_ab")
    zsm = _mm_call(h0, w_small, LANE, "in_proj_ab_small")

    gw = GLA_HEADS * GLA_DK
    wg = jnp.zeros((LANE, 2 * gw), F32)
    for dd in range(2):
        wg = wg.at[dd * GLA_RANK:(dd + 1) * GLA_RANK, dd * gw:(dd + 1) * gw].set(ab_gla_wg2[0, dd])
    gbias = ab_gla_bg2[0].reshape(1, 2 * gw)
    gparams = jnp.zeros((8, LANE), F32)
    gparams = gparams.at[0, GDN_A_LANE:GDN_B_LANE].set(ab_gdn_a_log[0].reshape(-1))
    gparams = gparams.at[1, GDN_A_LANE:GDN_B_LANE].set(ab_gdn_dt_bias[0].reshape(-1))
    qkv, gla_cum, gates, gates_t = _ab_prep_call(zab, zsm, ab_gdn_conv[0], wg, gbias, gparams, nlt)
    gla_f, gla_b = _gla_call(zab, gla_cum, nlt)
    gdn_f, gdn_b = _gdn_call(qkv, gates, gates_t, nlt)

    x1 = _out0_call(gla_f, gla_b, gdn_f, gdn_b, zab, ab_gla_norm[0], ab_gdn_norm[0],
                    ab_w_out[0].astype(BF16), x_lat, x_ctx, mods[0], nlt)

    h1 = _norm_mod_call(x1, None, mods[1], norm_g[1], nlt)
    zcd = _mm_call(h1, cd_w_in[0].astype(BF16), 1024, "in_proj_cd")

    lru_f, lru_b = _lru_call(zcd, cd_lru_conv_w[0], cd_lru_conv_b[0], cd_lru_wa[0], cd_lru_ba[0],
                             cd_lru_wx[0], cd_lru_bx[0], cd_lru_lambda[0], n_lat)

    x0, z = _hy_prep_call(zcd, cd_hy_conv_w[0], cd_hy_conv_b[0], nlt)

    pos = jnp.arange(n_lat, dtype=F32)
    t01 = pos / max(n_lat - 1, 1)
    bands = jnp.linspace(1e-4, HY_BANDS - 1, HY_BANDS, dtype=F32)
    ang = (2.0 * math.pi / n_lat) * pos[:, None] * bands[None, :]
    emb = jnp.concatenate([t01[:, None], jnp.cos(ang), jnp.sin(ang)], axis=-1)
    emb = jnp.pad(emb, ((0, 0), (0, LANE - emb.shape[1])))
    filt, ssq = _hy_filter_call(emb, cd_hy_w1[0], cd_hy_b1[0], cd_hy_w2[0], cd_hy_b2[0], cd_hy_w3[0],
                                cd_hy_freq[0], cd_hy_decay[0])

    n1 = LANE if n_lat >= LANE * LANE else 32
    n2 = 2 * n_lat // n1
    ca, sa, cc, sc, gmat, gimat = _dft_constants(n1, n2)
    fa_re, fa_im = _dft_a_call(filt, ca, sa, n1)
    h_re, h_im = _dft_b_filter_call(fa_re, fa_im, gmat, ssq)
    za_re, za_im = _dft_a_call(z, ca, sa, n1)
    b_re, b_im = _dft_b_call(za_re, za_im, h_re, h_im, gmat, gimat)
    y_hy = _dft_c_call(b_re, b_im, x0, z, zcd, cd_hy_bias[0], cc, sc)

    out = _final_call(lru_f, lru_b, zcd, y_hy, cd_w_out[0].astype(BF16), x1, mods[1], final_g, nlt)
    return out[None]
```

```python
import functools
import math

import numpy as np
import jax
import jax.numpy as jnp
from jax import lax
from jax.experimental import pallas as pl
from jax.experimental.pallas import tpu as pltpu

F32 = jnp.float32
BF16 = jnp.bfloat16
HI = lax.Precision.HIGHEST

NORM_EPS = 1e-6
CHUNK = 64
ROW_TILE = 256
GRID_W = 64
LANE = 128
GLA_HEADS, GLA_DK, GLA_DV, GLA_RANK, GLA_GATE_NORM = 4, 128, 256, 16, 16.0
GDN_HEADS, GDN_DK = 8, 128
LRU_BLOCKS, LRU_C = 8, 8.0
HY_BANDS = 16
VMEM_LIMIT = 56 * 1024 * 1024


def _cparams(n_axes):
    return pltpu.CompilerParams(dimension_semantics=("arbitrary",) * n_axes,
                                vmem_limit_bytes=VMEM_LIMIT)


def _silu(x):
    return x * jax.nn.sigmoid(x)


def _softplus(x):
    return jnp.maximum(x, 0.0) + jnp.log1p(jnp.exp(-jnp.abs(x)))


def _dot(a, b, precision=None):
    return jnp.dot(a, b, precision=precision, preferred_element_type=F32)


def _dot_nt(a, b, precision=None):
    return lax.dot_general(a, b, (((1,), (1,)), ((), ())), precision=precision,
                           preferred_element_type=F32)


def _dot_tn(a, b, precision=None):
    return lax.dot_general(a, b, (((0,), (0,)), ((), ())), precision=precision,
                           preferred_element_type=F32)


def _bdot(a, b):
    return lax.dot_general(a, b, (((2,), (1,)), ((0,), (0,))), preferred_element_type=F32)


def _bdot_nt(a, b):
    return lax.dot_general(a, b, (((2,), (2,)), ((0,), (0,))), preferred_element_type=F32)


def _bdot_tn(a, b):
    return lax.dot_general(a, b, (((1,), (1,)), ((0,), (0,))), preferred_element_type=F32)


def _mod_kernel(c_ref, w_ref, b_ref, o_ref):
    o_ref[...] = _dot(_silu(c_ref[...]), w_ref[...], HI) + b_ref[...]


def _mod_call(cvec, mod_w, mod_b):
    depth, d, n = mod_w.shape
    tn = 512
    return pl.pallas_call(
        _mod_kernel, grid=(depth, n // tn),
        in_specs=[pl.BlockSpec((8, d), lambda l, j: (0, 0)),
                  pl.BlockSpec((None, d, tn), lambda l, j: (l, 0, j)),
                  pl.BlockSpec((None, 1, tn), lambda l, j: (l, 0, j))],
        out_specs=pl.BlockSpec((None, 8, tn), lambda l, j: (l, 0, j)),
        out_shape=jax.ShapeDtypeStruct((depth, 8, n), F32),
        compiler_params=_cparams(2), name="adaln_mod",
    )(cvec, mod_w, mod_b.reshape(depth, 1, n))


def _norm_mod_kernel(*refs, two_inputs, n_lat_tiles, d):
    if two_inputs:
        x_ref, c_ref, m_ref, g_ref, o_ref = refs
    else:
        x_ref, m_ref, g_ref, o_ref = refs
    is_ctx = pl.program_id(0) == n_lat_tiles
    xt = x_ref[...]
    if two_inputs:
        xt = jnp.where(is_ctx, c_ref[...], xt)
    y = xt * lax.rsqrt(jnp.mean(xt * xt, axis=-1, keepdims=True) + NORM_EPS) * g_ref[...]
    m = m_ref[...]
    shift = jnp.where(is_ctx, m[1:2, :d], m[0:1, :d])
    scale = jnp.where(is_ctx, m[1:2, d:2 * d], m[0:1, d:2 * d])
    o_ref[...] = (y * (1.0 + scale) + shift).astype(BF16)


def _norm_mod_call(x_lat, x_ctx, mods, g, n_lat_tiles):
    d = x_lat.shape[1]
    nt = n_lat_tiles + 1
    two = x_ctx is not None
    in_specs = [pl.BlockSpec((ROW_TILE, d), (lambda i: (jnp.minimum(i, n_lat_tiles - 1), 0)) if two
                             else (lambda i: (i, 0)))]
    args = [x_lat]
    if two:
        in_specs.append(pl.BlockSpec((ROW_TILE, d), lambda i: (0, 0)))
        args.append(x_ctx)
    in_specs += [pl.BlockSpec((8, 3 * d), lambda i: (0, 0)), pl.BlockSpec((1, d), lambda i: (0, 0))]
    args += [mods, g.reshape(1, d)]
    return pl.pallas_call(
        functools.partial(_norm_mod_kernel, two_inputs=two, n_lat_tiles=n_lat_tiles, d=d),
        grid=(nt,), in_specs=in_specs,
        out_specs=pl.BlockSpec((ROW_TILE, d), lambda i: (i, 0)),
        out_shape=jax.ShapeDtypeStruct((nt * ROW_TILE, d), BF16),
        compiler_params=_cparams(1), name="norm_mod",
    )(*args)


def _mm_kernel(a_ref, w_ref, o_ref):
    o_ref[...] = _dot(a_ref[...], w_ref[...])


def _row_tile(n_tiles, max_mult):
    k = max(m for m in range(1, max_mult + 1) if n_tiles % m == 0)
    return k * ROW_TILE


def _mm_call(a, w, tn, name):
    m, k = a.shape
    n = w.shape[1]
    tm = _row_tile(m // ROW_TILE, 5)
    return pl.pallas_call(
        _mm_kernel, grid=(n // tn, m // tm),
        in_specs=[pl.BlockSpec((tm, k), lambda j, i: (i, 0)),
                  pl.BlockSpec((k, tn), lambda j, i: (0, j))],
        out_specs=pl.BlockSpec((tm, tn), lambda j, i: (i, j)),
        out_shape=jax.ShapeDtypeStruct((m, n), F32),
        compiler_params=_cparams(2), name=name,
    )(a, w)


def _row_shift(x, k, row, prev_row, next0, next1):
    n = x.shape[0]
    if k == -1:
        return jnp.where(row == 0, prev_row, pltpu.roll(x, 1, 0))
    if k == 1:
        return jnp.where(row == n - 1, next0, pltpu.roll(x, n - 1, 0))
    r = pltpu.roll(x, n - 2, 0)
    return jnp.where(row == n - 2, next0, jnp.where(row == n - 1, next1, r))


GDN_A_LANE = 2 * GLA_RANK
GDN_B_LANE = GDN_A_LANE + 2 * GDN_HEADS


def _ab_prep_kernel(x_ref, p_ref, n_ref, w_ref, z_ref, wg_ref, gb_ref, gp_ref,
                    o_ref, cum_ref, gates_ref, gt_ref, *, n_lat_tiles):
    zs = z_ref[...]
    logit = _dot(zs.astype(BF16), wg_ref[...]) + gb_ref[...]
    g = -_softplus(-logit) / GLA_GATE_NORM
    log_a = -jnp.exp(gp_ref[0:1, :]) * _softplus(zs + gp_ref[1:2, :])
    beta = jax.nn.sigmoid(zs)
    lower, _, _ = _tri_masks(True)
    upper, _, _ = _tri_masks(False)
    lower, upper = lower.astype(F32), upper.astype(F32)
    half = cum_ref.shape[1] // 2
    lane = lax.broadcasted_iota(jnp.int32, (CHUNK, LANE), 1)
    lane_t = lax.broadcasted_iota(jnp.int32, (LANE, CHUNK), 0)
    bwd_lo, bwd_hi = GDN_A_LANE + GDN_HEADS, GDN_B_LANE
    for c in range(ROW_TILE // CHUNK):
        sl = slice(c * CHUNK, (c + 1) * CHUNK)
        cum_ref[sl, :half] = _dot(lower, g[sl, :half], HI)
        cum_ref[sl, half:] = _dot(upper, g[sl, half:], HI)
        cum_f = _dot(lower, log_a[sl], HI)
        cum_b = _dot(upper, log_a[sl], HI)
        is_b = jnp.logical_and(lane >= bwd_lo, lane < bwd_hi)
        gates_ref[sl, :] = jnp.where(lane >= GDN_B_LANE, beta[sl], jnp.where(is_b, cum_b, cum_f))
        is_bt = jnp.logical_and(lane_t >= bwd_lo, lane_t < bwd_hi)
        gt_ref[c * LANE:(c + 1) * LANE, :CHUNK] = jnp.where(is_bt, cum_b.T, cum_f.T)
        gt_ref[c * LANE:(c + 1) * LANE, CHUNK:] = jnp.zeros((LANE, LANE - CHUNK), F32)

    i = pl.program_id(0)
    has_prev = jnp.logical_and(i != 0, i != n_lat_tiles)
    has_next = i < n_lat_tiles - 1
    x = x_ref[...]
    row = lax.broadcasted_iota(jnp.int32, x.shape, 0)
    prev_row = jnp.where(has_prev, p_ref[7:8, :], 0.0)
    next0 = jnp.where(has_next, n_ref[0:1, :], 0.0)
    next1 = jnp.where(has_next, n_ref[1:2, :], 0.0)
    w = w_ref[...]
    y = (w[0:1] * _row_shift(x, -1, row, prev_row, next0, next1) + w[1:2] * x
         + w[2:3] * _row_shift(x, 1, row, prev_row, next0, next1)
         + w[3:4] * _row_shift(x, 2, row, prev_row, next0, next1))
    y = _silu(y)
    qk = GDN_HEADS * GDN_DK
    for h in range(2 * GDN_HEADS):
        seg = y[:, h * GDN_DK:(h + 1) * GDN_DK]
        seg = seg * lax.rsqrt(jnp.sum(seg * seg, axis=-1, keepdims=True) + NORM_EPS)
        if h < GDN_HEADS:
            seg = seg * (GDN_DK ** -0.5)
        o_ref[:, h * GDN_DK:(h + 1) * GDN_DK] = seg
    o_ref[:, 2 * qk:] = y[:, 2 * qk:]


def _ab_prep_call(zab, zsm, conv_w, wg, gbias, gparams, n_lat_tiles):
    nt = n_lat_tiles + 1
    w = 3 * GDN_HEADS * GDN_DK
    gw = wg.shape[1]
    hb = ROW_TILE // 8
    last8 = nt * hb - 1
    nsub = ROW_TILE // CHUNK
    rows = nt * ROW_TILE
    const = lambda i: (0, 0)
    return pl.pallas_call(
        functools.partial(_ab_prep_kernel, n_lat_tiles=n_lat_tiles), grid=(nt,),
        in_specs=[pl.BlockSpec((ROW_TILE, w), lambda i: (i, 1)),
                  pl.BlockSpec((8, w), lambda i: (jnp.maximum(i * hb - 1, 0), 1)),
                  pl.BlockSpec((8, w), lambda i: (jnp.minimum((i + 1) * hb, last8), 1)),
                  pl.BlockSpec((4, w), const),
                  pl.BlockSpec((ROW_TILE, LANE), lambda i: (i, 0)),
                  pl.BlockSpec((LANE, gw), const), pl.BlockSpec((1, gw), const),
                  pl.BlockSpec((8, LANE), const)],
        out_specs=[pl.BlockSpec((ROW_TILE, w), lambda i: (i, 0)),
                   pl.BlockSpec((ROW_TILE, gw), lambda i: (i, 0)),
                   pl.BlockSpec((ROW_TILE, LANE), lambda i: (i, 0)),
                   pl.BlockSpec((nsub * LANE, LANE), lambda i: (i, 0))],
        out_shape=[jax.ShapeDtypeStruct((rows, w), F32), jax.ShapeDtypeStruct((rows, gw), F32),
                   jax.ShapeDtypeStruct((rows, LANE), F32),
                   jax.ShapeDtypeStruct((nt * nsub * LANE, LANE), F32)],
        compiler_params=_cparams(1), name="ab_prep",
    )(zab, zab, zab, conv_w, zsm, wg, gbias, gparams)


def _tile_fwd(s, n_lat_tiles):
    return jnp.where(s == 0, n_lat_tiles, s - 1)


def _tile_bwd(s, n_lat_tiles):
    return jnp.where(s == 0, n_lat_tiles, n_lat_tiles - s)


def _tri_masks(forward):
    row = lax.broadcasted_iota(jnp.int32, (CHUNK, CHUNK), 0)
    col = lax.broadcasted_iota(jnp.int32, (CHUNK, CHUNK), 1)
    incl = (col <= row) if forward else (col >= row)
    strict = (col < row) if forward else (col > row)
    return incl, strict, (row == col).astype(F32)


def _gla_kernel(qf, kf, vf, cf, qb, kb, vb, cb, of, ob, st_ref):
    @pl.when(pl.program_id(0) == 0)
    def _():
        st_ref[...] = jnp.zeros_like(st_ref)

    nsub = ROW_TILE // CHUNK
    dirs = ((qf, kf, vf, cf, of), (qb, kb, vb, cb, ob))
    for d, (q_ref, k_ref, v_ref, c_ref, o_ref) in enumerate(dirs):
        forward = d == 0
        incl, _, _ = _tri_masks(forward)
        last = CHUNK - 1 if forward else 0
        for c in (range(nsub) if forward else reversed(range(nsub))):
            sl = slice(c * CHUNK, (c + 1) * CHUNK)
            heads = lambda ref, w: jnp.stack([ref[sl, h * w:(h + 1) * w] for h in range(GLA_HEADS)])
            q, k, v, cum = heads(q_ref, GLA_DK), heads(k_ref, GLA_DK), heads(v_ref, GLA_DV), heads(c_ref, GLA_DK)
            tot = cum[:, last:last + 1]
            qd = (q * (GLA_DK ** -0.5) * jnp.exp(cum)).astype(BF16)
            ki = (k * jnp.exp(-cum)).astype(BF16)
            ke = (k * jnp.exp(tot - cum)).astype(BF16)
            vb16 = v.astype(BF16)
            sc = jnp.where(incl, _bdot_nt(qd, ki), 0.0).astype(BF16)
            st = st_ref[d]
            o = _bdot(sc, vb16) + _bdot_nt(qd, st.astype(BF16))
            for h in range(GLA_HEADS):
                o_ref[sl, h * GLA_DV:(h + 1) * GLA_DV] = o[h]
            st_ref[d] = jnp.exp(tot) * st + _bdot_tn(vb16, ke)


def _gla_call(zab, gla_cum, n_lat_tiles):
    nt = n_lat_tiles + 1
    tf = functools.partial(_tile_fwd, n_lat_tiles=n_lat_tiles)
    tb = functools.partial(_tile_bwd, n_lat_tiles=n_lat_tiles)
    qk_w, v_w = GLA_HEADS * GLA_DK, GLA_HEADS * GLA_DV

    def dir_specs(t, d):
        return [pl.BlockSpec((ROW_TILE, qk_w), lambda s: (t(s), 0)),
                pl.BlockSpec((ROW_TILE, qk_w), lambda s: (t(s), 1)),
                pl.BlockSpec((ROW_TILE, v_w), lambda s: (t(s), 1)),
                pl.BlockSpec((ROW_TILE, qk_w), lambda s: (t(s), d))]

    out_sds = jax.ShapeDtypeStruct((nt * ROW_TILE, v_w), F32)
    return pl.pallas_call(
        _gla_kernel, grid=(nt,),
        in_specs=dir_specs(tf, 0) + dir_specs(tb, 1),
        out_specs=[pl.BlockSpec((ROW_TILE, v_w), lambda s: (tf(s), 0)),
                   pl.BlockSpec((ROW_TILE, v_w), lambda s: (tb(s), 0))],
        out_shape=[out_sds, out_sds],
        scratch_shapes=[pltpu.VMEM((2, GLA_HEADS, GLA_DV, GLA_DK), F32)],
        compiler_params=_cparams(1), name="gla_scan",
    )(zab, zab, zab, gla_cum, zab, zab, zab, gla_cum)


INV_BASE = 8


def _unit_triangular_inverse_minus_eye(a):
    n = a.shape[-1]
    row = lax.broadcasted_iota(jnp.int32, (n, n), 0)
    col = lax.broadcasted_iota(jnp.int32, (n, n), 1)

    def same_block(b):
        s = int(math.log2(b))
        return lax.shift_right_logical(row, s) == lax.shift_right_logical(col, s)

    bdot = lambda p, r: _bdot(p.astype(BF16), r.astype(BF16))
    q = jnp.where(same_block(INV_BASE), a, 0.0)
    x = bdot(q, q)
    m = 2
    while 2 * m < INV_BASE:
        r = bdot(jnp.concatenate([q, x], axis=1), x)
        q, x = q + x + r[:, :n], r[:, n:]
        m *= 2
    q = q + x + bdot(q, x)
    b = INV_BASE
    while b < n:
        a_b = jnp.where(jnp.logical_and(same_block(2 * b), jnp.logical_not(same_block(b))), a, 0.0)
        y = a_b + bdot(a_b, q)
        q = q + y + bdot(q, y)
        b *= 2
    return q


def _gdn_kernel(xf, gf, tf_ref, xb, gb, tb_ref, of, ob, s_ref):
    @pl.when(pl.program_id(0) == 0)
    def _():
        s_ref[...] = jnp.zeros_like(s_ref)

    nsub = ROW_TILE // CHUNK
    qk_w = GDN_HEADS * GDN_DK
    nh = GDN_HEADS
    hs = range(nh)
    bidx = lax.broadcasted_iota(jnp.int32, (2 * nh, CHUNK, CHUNK), 0)
    row = lax.broadcasted_iota(jnp.int32, (2 * nh, CHUNK, CHUNK), 1)
    col = lax.broadcasted_iota(jnp.int32, (2 * nh, CHUNK, CHUNK), 2)
    ahead = jnp.where(bidx < nh, col - row, row - col)
    incl = ahead <= 0
    strict = ahead < 0
    fwd1 = lax.broadcasted_iota(jnp.int32, (2 * nh, 1, 1), 0) < nh
    for c in range(nsub):
        slf = slice(c * CHUNK, (c + 1) * CHUNK)
        cb = nsub - 1 - c
        slb = slice(cb * CHUNK, (cb + 1) * CHUNK)
        heads = lambda off: jnp.stack(
            [xf[slf, off + h * GDN_DK:off + (h + 1) * GDN_DK] for h in hs]
            + [xb[slb, off + h * GDN_DK:off + (h + 1) * GDN_DK] for h in hs])
        q, k, v = heads(0), heads(qk_w), heads(2 * qk_w)
        lanes = lambda l0: jnp.stack([gf[slf, l0 + h:l0 + h + 1] for h in hs]
                                     + [gb[slb, l0 + nh + h:l0 + nh + h + 1] for h in hs])
        cum, beta = lanes(GDN_A_LANE), lanes(GDN_B_LANE)
        cum_row = jnp.stack(
            [tf_ref[c * LANE + GDN_A_LANE + h:c * LANE + GDN_A_LANE + h + 1, :CHUNK] for h in hs]
            + [tb_ref[cb * LANE + GDN_A_LANE + nh + h:cb * LANE + GDN_A_LANE + nh + h + 1, :CHUNK] for h in hs])
        tot = jnp.where(fwd1, cum[:, CHUNK - 1:CHUNK], cum[:, 0:1])
        decay = jnp.exp(jnp.where(incl, cum - cum_row, -jnp.inf))
        k16 = k.astype(BF16)
        qk_kk = _bdot_nt(jnp.concatenate([q, k], axis=1).astype(BF16), k16)
        a = jnp.where(strict, -(qk_kk[:, CHUNK:] * decay * beta), 0.0)
        t_m1 = _unit_triangular_inverse_minus_eye(a)
        e_cum = jnp.exp(cum)
        rhs = jnp.concatenate([v * beta, k * (beta * e_cum)], axis=2)
        uw = rhs + _bdot(t_m1.astype(BF16), rhs.astype(BF16))
        u, w = uw[:, :, :GDN_DK], uw[:, :, GDN_DK:]
        sc = jnp.where(incl, qk_kk[:, :CHUNK] * decay, 0.0).astype(BF16)
        ke = (k * jnp.exp(tot - cum)).astype(BF16)
        s = s_ref[...]
        ws_qs = _bdot(jnp.concatenate([w, q * e_cum], axis=1).astype(BF16), s.astype(BF16))
        vn16 = (u - ws_qs[:, :CHUNK]).astype(BF16)
        o = ws_qs[:, CHUNK:] + _bdot(sc, vn16)
        for h in hs:
            of[slf, h * GDN_DK:(h + 1) * GDN_DK] = o[h]
            ob[slb, h * GDN_DK:(h + 1) * GDN_DK] = o[nh + h]
        s_ref[...] = jnp.exp(tot) * s + _bdot_tn(ke, vn16)


def _gdn_call(qkv, gates, gates_t, n_lat_tiles):
    nt = n_lat_tiles + 1
    tf = functools.partial(_tile_fwd, n_lat_tiles=n_lat_tiles)
    tb = functools.partial(_tile_bwd, n_lat_tiles=n_lat_tiles)
    w = qkv.shape[1]
    out_w = GDN_HEADS * GDN_DK
    tl = (ROW_TILE // CHUNK) * LANE

    def dir_specs(t):
        return [pl.BlockSpec((ROW_TILE, w), lambda s: (t(s), 0)),
                pl.BlockSpec((ROW_TILE, LANE), lambda s: (t(s), 0)),
                pl.BlockSpec((tl, LANE), lambda s: (t(s), 0))]

    out_sds = jax.ShapeDtypeStruct((nt * ROW_TILE, out_w), F32)
    return pl.pallas_call(
        _gdn_kernel, grid=(nt,),
        in_specs=dir_specs(tf) + dir_specs(tb),
        out_specs=[pl.BlockSpec((ROW_TILE, out_w), lambda s: (tf(s), 0)),
                   pl.BlockSpec((ROW_TILE, out_w), lambda s: (tb(s), 0))],
        out_shape=[out_sds, out_sds],
        scratch_shapes=[pltpu.VMEM((2 * GDN_HEADS, GDN_DK, GDN_DK), F32)],
        compiler_params=_cparams(1), name="gdn_scan",
    )(qkv, gates, gates_t, qkv, gates, gates_t)


def _merge_ab_kernel(gf, gb, df, db, gz, dz, gn, dn, o_ref):
    gla_w = GLA_HEADS * GLA_DV
    og = gf[...] + gb[...]
    zg = gz[...]
    for h in range(GLA_HEADS):
        sl = slice(h * GLA_DV, (h + 1) * GLA_DV)
        seg = og[:, sl]
        y = seg * lax.rsqrt(jnp.mean(seg * seg, axis=-1, keepdims=True) + NORM_EPS) * gn[...]
        o_ref[:, sl] = (y * _silu(zg[:, sl])).astype(BF16)
    od = df[...] + db[...]
    zd = dz[...]
    for h in range(GDN_HEADS):
        sl = slice(h * GDN_DK, (h + 1) * GDN_DK)
        seg = od[:, sl]
        y = seg * lax.rsqrt(jnp.mean(seg * seg, axis=-1, keepdims=True) + NORM_EPS) * dn[...]
        o_ref[:, gla_w + h * GDN_DK:gla_w + (h + 1) * GDN_DK] = (y * _silu(zd[:, sl])).astype(BF16)


def _out0_kernel(gf, gb, df, db, gz, dz, gn, dn, w_ref, x_ref, c_ref, m_ref, m1_ref, g1_ref,
                 o_ref, h1_ref, a_scr, *, n_lat_tiles, d):
    is_ctx = pl.program_id(0) == n_lat_tiles
    _merge_ab_kernel(gf, gb, df, db, gz, dz, gn, dn, a_scr)
    acc = _dot(a_scr[...], w_ref[...])
    m = m_ref[...]
    gate = jnp.where(is_ctx, m[1:2, 2 * d:], m[0:1, 2 * d:])
    xn = jnp.where(is_ctx, c_ref[...], x_ref[...]) + gate * acc
    o_ref[...] = xn
    m1 = m1_ref[...]
    y = xn * lax.rsqrt(jnp.mean(xn * xn, axis=-1, keepdims=True) + NORM_EPS) * g1_ref[...]
    shift = jnp.where(is_ctx, m1[1:2, :d], m1[0:1, :d])
    scale = jnp.where(is_ctx, m1[1:2, d:2 * d], m1[0:1, d:2 * d])
    h1_ref[...] = (y * (1.0 + scale) + shift).astype(BF16)


def _out0_call(gla_f, gla_b, gdn_f, gdn_b, zab, gla_norm, gdn_norm, w_out, x_lat, x_ctx, mods,
               mods_next, g_next, n_lat_tiles):
    m = gla_f.shape[0]
    w = GLA_HEADS * GLA_DV
    k, d = w_out.shape
    row = lambda i: (i, 0)
    const = lambda i: (0, 0)
    return pl.pallas_call(
        functools.partial(_out0_kernel, n_lat_tiles=n_lat_tiles, d=d), grid=(m // ROW_TILE,),
        in_specs=[pl.BlockSpec((ROW_TILE, w), row)] * 4
        + [pl.BlockSpec((ROW_TILE, w), lambda i: (i, 2)), pl.BlockSpec((ROW_TILE, w), lambda i: (i, 6)),
           pl.BlockSpec((1, GLA_DV), const), pl.BlockSpec((1, GDN_DK), const),
           pl.BlockSpec((k, d), const),
           pl.BlockSpec((ROW_TILE, d), lambda i: (jnp.minimum(i, n_lat_tiles - 1), 0)),
           pl.BlockSpec((ROW_TILE, d), const),
           pl.BlockSpec((8, 3 * d), const), pl.BlockSpec((8, 3 * d), const), pl.BlockSpec((1, d), const)],
        out_specs=[pl.BlockSpec((ROW_TILE, d), row), pl.BlockSpec((ROW_TILE, d), row)],
        out_shape=[jax.ShapeDtypeStruct((m, d), F32), jax.ShapeDtypeStruct((m, d), BF16)],
        scratch_shapes=[pltpu.VMEM((ROW_TILE, k), BF16)],
        compiler_params=_cparams(1), name="out_proj_ab",
    )(gla_f, gla_b, gdn_f, gdn_b, zab, zab, gla_norm.reshape(1, -1), gdn_norm.reshape(1, -1),
      w_out, x_lat, x_ctx, mods, mods_next, g_next.reshape(1, d))


def _lru_gates(seg, wa, ba, wx, bx, lam):
    seg16 = seg.astype(BF16)
    r = jax.nn.sigmoid(_dot(seg16, wa) + ba)
    gi = jax.nn.sigmoid(_dot(seg16, wx) + bx)
    log_a = -LRU_C * _softplus(-lam) * r
    a = jnp.exp(log_a)
    return a, jnp.sqrt(jnp.tanh(-log_a) * (a * a + 1.0)) * (gi * seg)


def _lru_ctx_kernel(x_ref, cw_ref, cb_ref, wa_ref, ba_ref, wx_ref, bx_ref, lam_ref, o_ref, a_scr, d_scr):
    x = x_ref[...]
    rows, width = x.shape
    blk = width // LRU_BLOCKS
    cw = cw_ref[...]
    row = lax.broadcasted_iota(jnp.int32, x.shape, 0)
    zero = jnp.zeros((1, width), F32)
    xc = (cw[0:1] * _row_shift(x, -1, row, zero, zero, zero) + cw[1:2] * x
          + cw[2:3] * _row_shift(x, 1, row, zero, zero, zero)
          + cw[3:4] * _row_shift(x, 2, row, zero, zero, zero) + cb_ref[...])
    for d in range(2):
        for b in range(LRU_BLOCKS):
            sl = slice(b * blk, (b + 1) * blk)
            a, drv = _lru_gates(xc[:, sl], wa_ref[d, b], ba_ref[d, :, sl], wx_ref[d, b],
                                bx_ref[d, :, sl], lam_ref[d, :, sl])
            a_scr[d, :, sl] = a
            d_scr[d, :, sl] = drv

    def body(t, carry):
        hf, hb = carry
        hf = a_scr[0, pl.ds(t, 1), :] * hf + d_scr[0, pl.ds(t, 1), :]
        tb = rows - 1 - t
        hb = a_scr[1, pl.ds(tb, 1), :] * hb + d_scr[1, pl.ds(tb, 1), :]
        return hf, hb

    hf, hb = lax.fori_loop(0, rows, body, (zero, zero), unroll=8)
    o_ref[...] = jnp.zeros_like(o_ref)
    o_ref[0:1, :] = hf
    o_ref[1:2, :] = hb


def _lru_lat_kernel(xf, pf, nf, xb, pb, nb, h0_ref, cw_ref, cb_ref, wa_ref, ba_ref, wx_ref, bx_ref,
                    lam_ref, of, ob, a_scr, d_scr, hl_scr, pc_scr, carry, *, n_colblocks):
    ci, s = pl.program_id(0), pl.program_id(1)

    @pl.when(s == 0)
    def _():
        carry[...] = h0_ref[...]

    cw = cw_ref[...]
    dirs = ((xf, pf, nf, of, s), (xb, pb, nb, ob, n_colblocks - 1 - s))
    for d, (x_ref, p_ref, n_ref, o_ref, cblk) in enumerate(dirs):
        forward = d == 0
        x3 = x_ref[...]
        rows, ncol, tc = x3.shape
        has_prev, has_next = cblk > 0, cblk < n_colblocks - 1
        sub = lax.broadcasted_iota(jnp.int32, (ncol, tc), 0)
        prev_last = jnp.where(has_prev, p_ref[7, ncol - 1:ncol, :], 0.0)
        next_r0 = jnp.where(has_next, n_ref[0, 0:1, :], 0.0)
        next_r1 = jnp.where(has_next, n_ref[1, 0:1, :], 0.0)
        first_m1 = jnp.where(sub == 0, prev_last, pltpu.roll(x3[rows - 1], 1, 0))
        last_p1 = jnp.where(sub == ncol - 1, next_r0, pltpu.roll(x3[0], ncol - 1, 0))
        last_p2 = jnp.where(sub == ncol - 1, next_r1, pltpu.roll(x3[1], ncol - 1, 0))
        xm1 = jnp.concatenate([first_m1[None], x3[:rows - 1]], axis=0)
        xp1 = jnp.concatenate([x3[1:], last_p1[None]], axis=0)
        xp2 = jnp.concatenate([x3[2:], last_p1[None], last_p2[None]], axis=0)
        xc = cw[0:1] * xm1 + cw[1:2] * x3 + cw[2:3] * xp1 + cw[3:4] * xp2 + cb_ref[...]
        a, drv = _lru_gates(xc.reshape(rows * ncol, tc), wa_ref[d, ci], ba_ref[d], wx_ref[d, ci],
                            bx_ref[d], lam_ref[d])
        a_scr[d] = a.reshape(rows, ncol, tc)
        d_scr[d] = drv.reshape(rows, ncol, tc)

        def body(i, hp):
            h, p = hp
            r = i if forward else rows - 1 - i
            ar = a_scr[d, r]
            h = ar * h + d_scr[d, r]
            p = ar * p
            hl_scr[d, r] = h
            pc_scr[d, r] = p
            return h, p

        lax.fori_loop(0, rows, body, (jnp.zeros((ncol, tc), F32), jnp.ones((ncol, tc), F32)), unroll=8)

        end = rows - 1 if forward else 0
        hl_end, pc_end = hl_scr[d, end], pc_scr[d, end]
        h = carry[d:d + 1, :]
        h_in = jnp.zeros((ncol, tc), F32)
        for j in (range(ncol) if forward else reversed(range(ncol))):
            h_in = jnp.where(sub == j, h, h_in)
            h = hl_end[j:j + 1, :] + pc_end[j:j + 1, :] * h
        carry[d:d + 1, :] = h
        o_ref[...] = hl_scr[d] + pc_scr[d] * h_in[None]


def _lru_call(zcd, conv_w, conv_b, wa, ba, wx, bx, lam, n_lat):
    m, ncols_z = zcd.shape
    width = conv_w.shape[1]
    n_cols = GRID_W
    rows = n_lat // n_cols
    tc = width // LRU_BLOCKS
    ncb = n_cols // 8
    full = lambda a: pl.BlockSpec(a.shape, lambda *_: (0,) * a.ndim)
    wa16, wx16 = wa.astype(BF16), wx.astype(BF16)
    cb2, ba3, bx3, lam3 = (conv_b.reshape(1, width), ba.reshape(2, 1, width), bx.reshape(2, 1, width),
                           lam.reshape(2, 1, width))

    params = (conv_w, cb2, wa16, ba3, wx16, bx3, lam3)
    h0 = pl.pallas_call(
        _lru_ctx_kernel, grid=(1,),
        in_specs=[pl.BlockSpec((ROW_TILE, width), lambda i: (n_lat // ROW_TILE, 0))] + [full(p) for p in params],
        out_specs=pl.BlockSpec((8, width), lambda i: (0, 0)),
        out_shape=jax.ShapeDtypeStruct((8, width), F32),
        scratch_shapes=[pltpu.VMEM((2, ROW_TILE, width), F32)] * 2,
        compiler_params=_cparams(1), name="rglru_ctx",
    )(zcd, *params)

    z3 = zcd.reshape(m // n_cols, n_cols, ncols_z)
    hb8 = rows // 8
    bf = lambda s: s
    bb = lambda s: ncb - 1 - s

    def dir_specs(blk):
        return [pl.BlockSpec((rows, 8, tc), lambda c, s: (0, blk(s), c)),
                pl.BlockSpec((8, 8, tc), lambda c, s: (hb8 - 1, jnp.maximum(blk(s) - 1, 0), c)),
                pl.BlockSpec((8, 8, tc), lambda c, s: (0, jnp.minimum(blk(s) + 1, ncb - 1), c))]

    vec = lambda lead: pl.BlockSpec((lead, 1, tc), lambda c, s: (0, 0, c))
    out_sds = jax.ShapeDtypeStruct((rows, n_cols, width), F32)
    hf, hb = pl.pallas_call(
        functools.partial(_lru_lat_kernel, n_colblocks=ncb), grid=(width // tc, ncb),
        in_specs=dir_specs(bf) + dir_specs(bb)
        + [pl.BlockSpec((8, tc), lambda c, s: (0, c)),
           pl.BlockSpec((4, tc), lambda c, s: (0, c)), pl.BlockSpec((1, tc), lambda c, s: (0, c)),
           full(wa16), vec(2), full(wx16), vec(2), vec(2)],
        out_specs=[pl.BlockSpec((rows, 8, tc), lambda c, s: (0, bf(s), c)),
                   pl.BlockSpec((rows, 8, tc), lambda c, s: (0, bb(s), c))],
        out_shape=[out_sds, out_sds],
        scratch_shapes=[pltpu.VMEM((2, rows, 8, tc), F32)] * 4 + [pltpu.VMEM((8, tc), F32)],
        compiler_params=_cparams(2), name="rglru_scan",
    )(z3, z3, z3, z3, z3, z3, h0, conv_w, cb2, wa16, ba3, wx16, bx3, lam3)
    return hf.reshape(n_lat, width), hb.reshape(n_lat, width)


def _hy_prep_kernel(*refs, n_lat_tiles):
    xs, (w_ref, b_ref, x0_ref, z_ref) = refs[:9], refs[9:]
    i = pl.program_id(0)
    has_prev = i > 0
    has_next = i < n_lat_tiles - 1
    outs = []
    width = x0_ref.shape[1]
    for j in range(3):
        x_ref, p_ref, n_ref = xs[3 * j:3 * j + 3]
        x = x_ref[...]
        row = lax.broadcasted_iota(jnp.int32, x.shape, 0)
        prev_row = jnp.where(has_prev, p_ref[7:8, :], 0.0)
        next0 = jnp.where(has_next, n_ref[0:1, :], 0.0)
        sl = slice(j * width, (j + 1) * width)
        outs.append(w_ref[0:1, sl] * _row_shift(x, -1, row, prev_row, next0, next0)
                    + w_ref[1:2, sl] * x
                    + w_ref[2:3, sl] * _row_shift(x, 1, row, prev_row, next0, next0)
                    + b_ref[:, sl])
    x0_ref[...] = outs[0]
    z_ref[...] = outs[1] * outs[2]


def _hy_prep_call(zcd, conv_w, conv_b, n_lat_tiles):
    width = conv_w.shape[1] // 3
    hb = ROW_TILE // 8
    last8 = n_lat_tiles * hb - 1
    in_specs, args = [], []
    for j in range(3):
        cbk = 2 + j
        in_specs += [pl.BlockSpec((ROW_TILE, width), lambda i, cbk=cbk: (i, cbk)),
                     pl.BlockSpec((8, width), lambda i, cbk=cbk: (jnp.maximum(i * hb - 1, 0), cbk)),
                     pl.BlockSpec((8, width), lambda i, cbk=cbk: (jnp.minimum((i + 1) * hb, last8), cbk))]
        args += [zcd, zcd, zcd]
    in_specs += [pl.BlockSpec((3, 3 * width), lambda i: (0, 0)), pl.BlockSpec((1, 3 * width), lambda i: (0, 0))]
    out_sds = jax.ShapeDtypeStruct((n_lat_tiles * ROW_TILE, width), F32)
    return pl.pallas_call(
        functools.partial(_hy_prep_kernel, n_lat_tiles=n_lat_tiles), grid=(n_lat_tiles,),
        in_specs=in_specs,
        out_specs=[pl.BlockSpec((ROW_TILE, width), lambda i: (i, 0))] * 2,
        out_shape=[out_sds, out_sds],
        compiler_params=_cparams(1), name="hyena_prep",
    )(*args, conv_w, conv_b.reshape(1, -1))


def _hy_filter_kernel(e_ref, w1, b1, w2, b2, w3, fr, dec, f_ref, ssq_ref):
    i = pl.program_id(0)
    e = e_ref[...]
    hid = jnp.sin(fr[...] * (_dot(e, w1[...], HI) + b1[...]))
    hid = jnp.sin(fr[...] * (_dot(hid, w2[...], HI) + b2[...]))
    filt = _dot(hid.astype(BF16), w3[...]) * jnp.exp(-e[:, 0:1] * dec[...])
    half = filt.shape[1] // 2
    row = lax.broadcasted_iota(jnp.int32, filt.shape, 0)
    col = lax.broadcasted_iota(jnp.int32, filt.shape, 1)
    unused = jnp.logical_and(jnp.logical_and(i == 0, row == 0), col >= half)
    filt = jnp.where(unused, 0.0, filt)
    f_ref[...] = filt

    @pl.when(i == 0)
    def _():
        ssq_ref[...] = jnp.zeros_like(ssq_ref)

    ssq_ref[...] = ssq_ref[...] + jnp.sum(filt * filt, axis=0, keepdims=True)


def _hy_filter_call(emb, w1, b1, w2, b2, w3, freq, decay):
    length = emb.shape[0]
    n_out = w3.shape[1]
    hid = w1.shape[1]
    pad_r = lambda a: jnp.pad(a, ((0, LANE - a.shape[0]), (0, 0)))
    pad_c = lambda a: jnp.pad(a, ((0, 0), (0, LANE - a.shape[1])))
    params = (pad_c(pad_r(w1)), pad_c(b1.reshape(1, hid)), pad_c(pad_r(w2)), pad_c(b2.reshape(1, hid)),
              pad_r(w3).astype(BF16), pad_c(freq.reshape(1, hid)), decay.reshape(1, n_out))
    full = lambda a: pl.BlockSpec(a.shape, lambda i: (0,) * a.ndim)
    return pl.pallas_call(
        _hy_filter_kernel, grid=(length // ROW_TILE,),
        in_specs=[pl.BlockSpec((ROW_TILE, LANE), lambda i: (i, 0))] + [full(p) for p in params],
        out_specs=[pl.BlockSpec((ROW_TILE, n_out), lambda i: (i, 0)),
                   pl.BlockSpec((8, n_out), lambda i: (0, 0))],
        out_shape=[jax.ShapeDtypeStruct((length, n_out), F32), jax.ShapeDtypeStruct((8, n_out), F32)],
        compiler_params=_cparams(1), name="hyena_filter",
    )(emb, *params)


SUB = 8


def _dft_a_kernel(x_ref, c_ref, s_ref, re_ref, im_ref):
    xt = pltpu.einshape("kjc->jkc", x_ref[...])
    res_re, res_im = [], []
    for j in range(SUB):
        xb = xt[j].astype(BF16)
        res_re.append(_dot(c_ref[...], xb))
        res_im.append(_dot(s_ref[...], xb))
    re_ref[...] = pltpu.einshape("jkc->kjc", jnp.stack(res_re))
    im_ref[...] = pltpu.einshape("jkc->kjc", jnp.stack(res_im))


def _dft_a_call(x, cmat, smat, n1):
    length, ch = x.shape
    n2, n2h = cmat.shape
    tc = 512
    x4 = x.reshape(n2h, n1 // SUB, SUB, ch)
    out_sds = jax.ShapeDtypeStruct((n2, n1 // SUB, SUB, ch), F32)
    re, im = pl.pallas_call(
        _dft_a_kernel, grid=(ch // tc, n1 // SUB),
        in_specs=[pl.BlockSpec((n2h, None, SUB, tc), lambda c, i: (0, i, 0, c)),
                  pl.BlockSpec((n2, n2h), lambda c, i: (0, 0)),
                  pl.BlockSpec((n2, n2h), lambda c, i: (0, 0))],
        out_specs=[pl.BlockSpec((n2, None, SUB, tc), lambda c, i: (0, i, 0, c))] * 2,
        out_shape=[out_sds, out_sds],
        compiler_params=_cparams(2), name="dft_stage_a",
    )(x4, cmat, smat)
    return re.reshape(n2, n1, ch), im.reshape(n2, n1, ch)


def _dft_b_filter_kernel(pre, pim, fre, fim, g_ref, sp_ref, sf_ref, hre_ref, him_ref, *, n1, inv_n):
    scale = lax.rsqrt(sp_ref[0:1, :] + sf_ref[0:1, :] + NORM_EPS) * inv_n
    for j in range(SUB):
        g = g_ref[j]
        xp = _dot(g, jnp.concatenate([pre[j], pim[j]], axis=0).astype(BF16))
        xf = _dot(g, jnp.concatenate([fre[j], fim[j]], axis=0).astype(BF16))
        hre_ref[j] = (xp[:n1] + xf[:n1]) * scale
        him_ref[j] = (xp[n1:] - xf[n1:]) * scale


def _dft_b_filter_call(a_re, a_im, gmat, ssq):
    n2, n1, ch2 = a_re.shape
    ch = ch2 // 2
    tc = 256
    nct = ch // tc
    spec = lambda half: pl.BlockSpec((SUB, n1, tc), lambda k, c: (k, 0, half * nct + c))
    out_sds = jax.ShapeDtypeStruct((n2, n1, ch), F32)
    return pl.pallas_call(
        functools.partial(_dft_b_filter_kernel, n1=n1, inv_n=1.0 / (n1 * n2)), grid=(n2 // SUB, nct),
        in_specs=[spec(0), spec(0), spec(1), spec(1),
                  pl.BlockSpec((SUB, 2 * n1, 2 * n1), lambda k, c: (k, 0, 0)),
                  pl.BlockSpec((8, tc), lambda k, c: (0, c)),
                  pl.BlockSpec((8, tc), lambda k, c: (0, nct + c))],
        out_specs=[pl.BlockSpec((SUB, n1, tc), lambda k, c: (k, 0, c))] * 2,
        out_shape=[out_sds, out_sds],
        compiler_params=_cparams(2), name="dft_stage_b_filter",
    )(a_re, a_im, a_re, a_im, gmat, ssq, ssq)


def _dft_b_kernel(are, aim, hre, him, g_ref, gi_ref, bre_ref, bim_ref, *, n1):
    for j in range(SUB):
        x = _dot(g_ref[j], jnp.concatenate([are[j], aim[j]], axis=0).astype(BF16))
        xr, xi = x[:n1], x[n1:]
        hr, hi = hre[j], him[j]
        y = jnp.concatenate([xr * hr - xi * hi, xr * hi + xi * hr], axis=0).astype(BF16)
        b = _dot(gi_ref[j], y)
        bre_ref[:, j, :] = b[:n1]
        bim_ref[:, j, :] = b[n1:]


def _dft_b_call(a_re, a_im, h_re, h_im, gmat, gimat):
    n2, n1, ch = a_re.shape
    tc = 512
    slab = pl.BlockSpec((SUB, n1, tc), lambda k, c: (k, 0, c))
    gspec = pl.BlockSpec((SUB, 2 * n1, 2 * n1), lambda k, c: (k, 0, 0))
    out_sds = jax.ShapeDtypeStruct((n1, n2 // SUB, SUB, ch), F32)
    re, im = pl.pallas_call(
        functools.partial(_dft_b_kernel, n1=n1), grid=(n2 // SUB, ch // tc),
        in_specs=[slab] * 4 + [gspec, gspec],
        out_specs=[pl.BlockSpec((n1, None, SUB, tc), lambda k, c: (0, k, 0, c))] * 2,
        out_shape=[out_sds, out_sds],
        compiler_params=_cparams(2), name="dft_stage_b",
    )(a_re, a_im, h_re, h_im, gmat, gimat)
    return re.reshape(n1, n2, ch), im.reshape(n1, n2, ch)


def _dft_c_kernel(bre, bim, x0_ref, z_ref, hz_ref, bias_ref, c_ref, s_ref, o_ref):
    for j in range(SUB):
        y = _dot(c_ref[...], bre[j].astype(BF16)) + _dot(s_ref[...], bim[j].astype(BF16))
        o_ref[:, j, :] = (x0_ref[:, j, :] * (y + z_ref[:, j, :] * bias_ref[...])
                          * _silu(hz_ref[:, j, :]))


def _dft_c_call(b_re, b_im, x0, z, zcd, bias, cmat, smat):
    n1, n2, ch = b_re.shape
    n2h = n2
    length = n1 * n2h
    tc = 256
    nct = ch // tc
    view = lambda a: a.reshape(a.shape[0] // n1, n1 // SUB, SUB, a.shape[1])
    tspec = lambda cblk: pl.BlockSpec((n2h, None, SUB, tc), lambda i, c: (0, i, 0, cblk * nct + c))
    out = pl.pallas_call(
        _dft_c_kernel, grid=(n1 // SUB, nct),
        in_specs=[pl.BlockSpec((SUB, n2, tc), lambda i, c: (i, 0, c))] * 2
        + [tspec(0), tspec(0), tspec(5), pl.BlockSpec((1, tc), lambda i, c: (0, c)),
           pl.BlockSpec((n2h, n2), lambda i, c: (0, 0)), pl.BlockSpec((n2h, n2), lambda i, c: (0, 0))],
        out_specs=tspec(0),
        out_shape=jax.ShapeDtypeStruct((n2h, n1 // SUB, SUB, ch), F32),
        compiler_params=_cparams(2), name="dft_stage_c",
    )(b_re, b_im, view(x0), view(z), view(zcd), bias.reshape(1, ch), cmat, smat)
    return out.reshape(length, ch)


def _dft_constants(n1, n2):
    n = n1 * n2
    n2h = n2 // 2
    k2 = jnp.arange(n2h, dtype=jnp.int32)
    ang_a = (math.pi / n2) * ((k2[None, :] * (2 * k2[:, None] + 1)) % (2 * n2)).astype(F32)
    ca, sa = jnp.cos(ang_a), -jnp.sin(ang_a)
    cc, sc = jnp.cos(ang_a).T, -jnp.sin(ang_a).T
    i1 = jnp.arange(n1, dtype=jnp.int32)
    ang_f = (2.0 * math.pi / n1) * ((i1[:, None] * i1[None, :]) % n1).astype(F32)
    ang_t = (math.pi / n) * ((2 * k2[:, None] + 1) * i1[None, :]).astype(F32)
    fr, fi = jnp.cos(ang_f)[None], -jnp.sin(ang_f)[None]
    wr, wi = jnp.cos(ang_t)[:, None, :], -jnp.sin(ang_t)[:, None, :]
    gr, gi = fr * wr - fi * wi, fr * wi + fi * wr
    g = jnp.concatenate([jnp.concatenate([gr, -gi], axis=2), jnp.concatenate([gi, gr], axis=2)], axis=1)
    tr, ti = jnp.swapaxes(gr, 1, 2), -jnp.swapaxes(gi, 1, 2)
    ginv = jnp.concatenate([jnp.concatenate([tr, -ti], axis=2), jnp.concatenate([ti, tr], axis=2)], axis=1)
    b = lambda a: a.astype(BF16)
    return b(ca), b(sa), b(cc), b(sc), b(g), b(ginv)


def _final_kernel(hf, hb, lz, hy, w1, w2, x_ref, m_ref, g_ref, o_ref, *, d):
    a1 = ((hf[...] + hb[...]) * _silu(lz[...])).astype(BF16)
    acc = _dot(a1, w1[...]) + _dot(hy[...].astype(BF16), w2[...])
    xn = x_ref[...] + m_ref[0:1, 2 * d:] * acc
    o_ref[...] = xn * lax.rsqrt(jnp.mean(xn * xn, axis=-1, keepdims=True) + NORM_EPS) * g_ref[...]


def _final_call(hf, hb, zcd, hy, w_out, x1, mods, final_g, n_lat_tiles):
    d = w_out.shape[1]
    half = w_out.shape[0] // 2
    row = lambda i: (i, 0)
    const = lambda i: (0, 0)
    tm = _row_tile(n_lat_tiles, 2)
    once = dict(pipeline_mode=pl.Buffered(1))
    return pl.pallas_call(
        functools.partial(_final_kernel, d=d), grid=(n_lat_tiles * ROW_TILE // tm,),
        in_specs=[pl.BlockSpec((tm, half), row), pl.BlockSpec((tm, half), row),
                  pl.BlockSpec((tm, half), lambda i: (i, 1)), pl.BlockSpec((tm, half), row),
                  pl.BlockSpec((half, d), const, **once), pl.BlockSpec((half, d), lambda i: (1, 0), **once),
                  pl.BlockSpec((tm, d), row), pl.BlockSpec((8, 3 * d), const),
                  pl.BlockSpec((1, d), const)],
        out_specs=pl.BlockSpec((tm, d), row),
        out_shape=jax.ShapeDtypeStruct((n_lat_tiles * ROW_TILE, d), F32),
        compiler_params=_cparams(1), name="out_proj_cd_final",
    )(hf, hb, zcd, hy, w_out, w_out, x1, mods, final_g.reshape(1, d))


def kernel(x, c, ctx, c_ctx, mod_w, mod_b, norm_g, ab_w_in, ab_gla_wg2, ab_gla_bg2, ab_gla_norm, ab_gdn_conv, ab_gdn_a_log, ab_gdn_dt_bias, ab_gdn_norm, ab_w_out, cd_w_in, cd_lru_conv_w, cd_lru_conv_b, cd_lru_wa, cd_lru_ba, cd_lru_wx, cd_lru_bx, cd_lru_lambda, cd_hy_conv_w, cd_hy_conv_b, cd_hy_w1, cd_hy_b1, cd_hy_w2, cd_hy_b2, cd_hy_w3, cd_hy_freq, cd_hy_decay, cd_hy_bias, cd_w_out, final_g):
    _, n_lat, d = x.shape
    assert x.shape[0] == 1 and ctx.shape[1] == ROW_TILE and n_lat % ROW_TILE == 0
    assert mod_w.shape[0] == 2 and d == 2 * GLA_HEADS * GLA_DV
    nlt = n_lat // ROW_TILE
    x_lat, x_ctx = x[0], ctx[0]

    cvec = jnp.zeros((8, d), F32).at[0].set(c[0]).at[1].set(c_ctx)
    mods = _mod_call(cvec, mod_w, mod_b)

    gla_qk, gla_v = GLA_HEADS * GLA_DK, GLA_HEADS * GLA_DV
    gdn_qk = GDN_HEADS * GDN_DK
    sizes = (gla_qk, gla_qk, gla_v, 2 * GLA_RANK, gla_v, 3 * gdn_qk, 2 * GDN_HEADS, 2 * GDN_HEADS, gdn_qk)
    gq, gk, gv, glr, gz, dqkv, da, db, dz = jnp.split(ab_w_in[0], np.cumsum(sizes)[:-1].tolist(), axis=1)
    w_main = jnp.concatenate([gq, gk, gv, gz, dqkv, dz], axis=1).astype(BF16)
    n_small = glr.shape[1] + da.shape[1] + db.shape[1]
    w_small = jnp.concatenate([glr, da, db, jnp.zeros((d, LANE - n_small), F32)], axis=1).astype(BF16)

    h0 = _norm_mod_call(x_lat, x_ctx, mods[0], norm_g[0], nlt)
    zab = _mm_call(h0, w_main, 1024, "in_proj_ab")
    zsm = _mm_call(h0, w_small, LANE, "in_proj_ab_small")

    gw = GLA_HEADS * GLA_DK
    wg = jnp.zeros((LANE, 2 * gw), F32)
    for dd in range(2):
        wg = wg.at[dd * GLA_RANK:(dd + 1) * GLA_RANK, dd * gw:(dd + 1) * gw].set(ab_gla_wg2[0, dd])
    gbias = ab_gla_bg2[0].reshape(1, 2 * gw)
    gparams = jnp.zeros((8, LANE), F32)
    gparams = gparams.at[0, GDN_A_LANE:GDN_B_LANE].set(ab_gdn_a_log[0].reshape(-1))
    gparams = gparams.at[1, GDN_A_LANE:GDN_B_LANE].set(ab_gdn_dt_bias[0].reshape(-1))
    qkv, gla_cum, gates, gates_t = _ab_prep_call(zab, zsm, ab_gdn_conv[0], wg.astype(BF16), gbias, gparams, nlt)
    gla_f, gla_b = _gla_call(zab, gla_cum, nlt)
    gdn_f, gdn_b = _gdn_call(qkv, gates, gates_t, nlt)

    x1, h1 = _out0_call(gla_f, gla_b, gdn_f, gdn_b, zab, ab_gla_norm[0], ab_gdn_norm[0],
                        ab_w_out[0].astype(BF16), x_lat, x_ctx, mods[0], mods[1], norm_g[1], nlt)

    zcd = _mm_call(h1, cd_w_in[0].astype(BF16), 1024, "in_proj_cd")

    lru_f, lru_b = _lru_call(zcd, cd_lru_conv_w[0], cd_lru_conv_b[0], cd_lru_wa[0], cd_lru_ba[0],
                             cd_lru_wx[0], cd_lru_bx[0], cd_lru_lambda[0], n_lat)

    x0, z = _hy_prep_call(zcd, cd_hy_conv_w[0], cd_hy_conv_b[0], nlt)

    pos = jnp.arange(n_lat, dtype=F32)
    t01 = pos / max(n_lat - 1, 1)
    bands = jnp.linspace(1e-4, HY_BANDS - 1, HY_BANDS, dtype=F32)
    ang = (2.0 * math.pi / n_lat) * pos[:, None] * bands[None, :]
    emb = jnp.concatenate([t01[:, None], jnp.cos(ang), jnp.sin(ang)], axis=-1)
    emb = jnp.pad(emb, ((0, 0), (0, LANE - emb.shape[1])))
    filt, ssq = _hy_filter_call(emb, cd_hy_w1[0], cd_hy_b1[0], cd_hy_w2[0], cd_hy_b2[0], cd_hy_w3[0],
                                cd_hy_freq[0], cd_hy_decay[0])

    n1 = LANE if n_lat >= LANE * LANE else 32
    n2 = 2 * n_lat // n1
    ca, sa, cc, sc, gmat, gimat = _dft_constants(n1, n2)
    fa_re, fa_im = _dft_a_call(filt, ca, sa, n1)
    h_re, h_im = _dft_b_filter_call(fa_re, fa_im, gmat, ssq)
    za_re, za_im = _dft_a_call(z, ca, sa, n1)
    b_re, b_im = _dft_b_call(za_re, za_im, h_re, h_im, gmat, gimat)
    y_hy = _dft_c_call(b_re, b_im, x0, z, zcd, cd_hy_bias[0], cc, sc)

    out = _final_call(lru_f, lru_b, zcd, y_hy, cd_w_out[0].astype(BF16), x1, mods[1], final_g, nlt)
    return out[None]
```

```python
import functools
import math

import numpy as np
import jax
import jax.numpy as jnp
from jax import lax
from jax.experimental import pallas as pl
from jax.experimental.pallas import tpu as pltpu

F32 = jnp.float32
BF16 = jnp.bfloat16
HI = lax.Precision.HIGHEST

NORM_EPS = 1e-6
CHUNK = 64
ROW_TILE = 256
GRID_W = 64
LANE = 128
GLA_HEADS, GLA_DK, GLA_DV, GLA_RANK, GLA_GATE_NORM = 4, 128, 256, 16, 16.0
GDN_HEADS, GDN_DK = 8, 128
LRU_BLOCKS, LRU_C = 8, 8.0
HY_BANDS = 16
VMEM_LIMIT = 56 * 1024 * 1024


def _cparams(n_axes):
    return pltpu.CompilerParams(dimension_semantics=("arbitrary",) * n_axes,
                                vmem_limit_bytes=VMEM_LIMIT)


def _silu(x):
    return x * jax.nn.sigmoid(x)


def _softplus(x):
    return jnp.maximum(x, 0.0) + jnp.log1p(jnp.exp(-jnp.abs(x)))


def _dot(a, b, precision=None):
    return jnp.dot(a, b, precision=precision, preferred_element_type=F32)


def _dot_nt(a, b, precision=None):
    return lax.dot_general(a, b, (((1,), (1,)), ((), ())), precision=precision,
                           preferred_element_type=F32)


def _dot_tn(a, b, precision=None):
    return lax.dot_general(a, b, (((0,), (0,)), ((), ())), precision=precision,
                           preferred_element_type=F32)


def _mask_dot_f32(mask16, x):
    hi = x.astype(BF16)
    r = x - hi.astype(F32)
    mid = r.astype(BF16)
    lo = (r - mid.astype(F32)).astype(BF16)
    return _dot(mask16, hi) + _dot(mask16, mid) + _dot(mask16, lo)


def _bdot(a, b):
    return lax.dot_general(a, b, (((2,), (1,)), ((0,), (0,))), preferred_element_type=F32)


def _bdot_nt(a, b):
    return lax.dot_general(a, b, (((2,), (2,)), ((0,), (0,))), preferred_element_type=F32)


def _bdot_tn(a, b):
    return lax.dot_general(a, b, (((1,), (1,)), ((0,), (0,))), preferred_element_type=F32)


def _mod_kernel(c_ref, w_ref, b_ref, o_ref):
    o_ref[...] = _dot(_silu(c_ref[...]), w_ref[...], HI) + b_ref[...]


def _mod_call(cvec, mod_w, mod_b):
    depth, d, n = mod_w.shape
    tn = 512
    return pl.pallas_call(
        _mod_kernel, grid=(depth, n // tn),
        in_specs=[pl.BlockSpec((8, d), lambda l, j: (0, 0)),
                  pl.BlockSpec((None, d, tn), lambda l, j: (l, 0, j)),
                  pl.BlockSpec((None, 1, tn), lambda l, j: (l, 0, j))],
        out_specs=pl.BlockSpec((None, 8, tn), lambda l, j: (l, 0, j)),
        out_shape=jax.ShapeDtypeStruct((depth, 8, n), F32),
        compiler_params=_cparams(2), name="adaln_mod",
    )(cvec, mod_w, mod_b.reshape(depth, 1, n))


def _norm_mod_kernel(*refs, two_inputs, n_lat_tiles, d):
    if two_inputs:
        x_ref, c_ref, m_ref, g_ref, o_ref = refs
    else:
        x_ref, m_ref, g_ref, o_ref = refs
    is_ctx = pl.program_id(0) == n_lat_tiles
    xt = x_ref[...]
    if two_inputs:
        xt = jnp.where(is_ctx, c_ref[...], xt)
    y = xt * lax.rsqrt(jnp.mean(xt * xt, axis=-1, keepdims=True) + NORM_EPS) * g_ref[...]
    m = m_ref[...]
    shift = jnp.where(is_ctx, m[1:2, :d], m[0:1, :d])
    scale = jnp.where(is_ctx, m[1:2, d:2 * d], m[0:1, d:2 * d])
    o_ref[...] = (y * (1.0 + scale) + shift).astype(BF16)


def _norm_mod_call(x_lat, x_ctx, mods, g, n_lat_tiles):
    d = x_lat.shape[1]
    nt = n_lat_tiles + 1
    two = x_ctx is not None
    in_specs = [pl.BlockSpec((ROW_TILE, d), (lambda i: (jnp.minimum(i, n_lat_tiles - 1), 0)) if two
                             else (lambda i: (i, 0)))]
    args = [x_lat]
    if two:
        in_specs.append(pl.BlockSpec((ROW_TILE, d), lambda i: (0, 0)))
        args.append(x_ctx)
    in_specs += [pl.BlockSpec((8, 3 * d), lambda i: (0, 0)), pl.BlockSpec((1, d), lambda i: (0, 0))]
    args += [mods, g.reshape(1, d)]
    return pl.pallas_call(
        functools.partial(_norm_mod_kernel, two_inputs=two, n_lat_tiles=n_lat_tiles, d=d),
        grid=(nt,), in_specs=in_specs,
        out_specs=pl.BlockSpec((ROW_TILE, d), lambda i: (i, 0)),
        out_shape=jax.ShapeDtypeStruct((nt * ROW_TILE, d), BF16),
        compiler_params=_cparams(1), name="norm_mod",
    )(*args)


def _mm_kernel(a_ref, w_ref, o_ref):
    o_ref[...] = _dot(a_ref[...], w_ref[...])


def _row_tile(n_tiles, max_mult):
    k = max(m for m in range(1, max_mult + 1) if n_tiles % m == 0)
    return k * ROW_TILE


def _mm_call(a, w, tn, name):
    m, k = a.shape
    n = w.shape[1]
    tm = _row_tile(m // ROW_TILE, 5)
    return pl.pallas_call(
        _mm_kernel, grid=(n // tn, m // tm),
        in_specs=[pl.BlockSpec((tm, k), lambda j, i: (i, 0)),
                  pl.BlockSpec((k, tn), lambda j, i: (0, j))],
        out_specs=pl.BlockSpec((tm, tn), lambda j, i: (i, j)),
        out_shape=jax.ShapeDtypeStruct((m, n), F32),
        compiler_params=_cparams(2), name=name,
    )(a, w)


def _row_shift(x, k, row, prev_row, next0, next1):
    n = x.shape[0]
    if k == -1:
        return jnp.where(row == 0, prev_row, pltpu.roll(x, 1, 0))
    if k == 1:
        return jnp.where(row == n - 1, next0, pltpu.roll(x, n - 1, 0))
    r = pltpu.roll(x, n - 2, 0)
    return jnp.where(row == n - 2, next0, jnp.where(row == n - 1, next1, r))


GDN_A_LANE = 2 * GLA_RANK
GDN_B_LANE = GDN_A_LANE + 2 * GDN_HEADS


def _ab_prep_kernel(x_ref, p_ref, n_ref, w_ref, z_ref, wg_ref, gb_ref, gp_ref,
                    o_ref, cum_ref, gates_ref, gt_ref, *, n_lat_tiles):
    zs = z_ref[...]
    logit = _dot(zs.astype(BF16), wg_ref[...]) + gb_ref[...]
    g = -_softplus(-logit) / GLA_GATE_NORM
    log_a = -jnp.exp(gp_ref[0:1, :]) * _softplus(zs + gp_ref[1:2, :])
    beta = jax.nn.sigmoid(zs)
    lower, _, _ = _tri_masks(True)
    upper, _, _ = _tri_masks(False)
    lower, upper = lower.astype(BF16), upper.astype(BF16)
    half = cum_ref.shape[1] // 2
    lane = lax.broadcasted_iota(jnp.int32, (CHUNK, LANE), 1)
    lane_t = lax.broadcasted_iota(jnp.int32, (LANE, CHUNK), 0)
    bwd_lo, bwd_hi = GDN_A_LANE + GDN_HEADS, GDN_B_LANE
    for c in range(ROW_TILE // CHUNK):
        sl = slice(c * CHUNK, (c + 1) * CHUNK)
        cum_ref[sl, :half] = _mask_dot_f32(lower, g[sl, :half])
        cum_ref[sl, half:] = _mask_dot_f32(upper, g[sl, half:])
        cum_f = _mask_dot_f32(lower, log_a[sl])
        cum_b = _mask_dot_f32(upper, log_a[sl])
        is_b = jnp.logical_and(lane >= bwd_lo, lane < bwd_hi)
        gates_ref[sl, :] = jnp.where(lane >= GDN_B_LANE, beta[sl], jnp.where(is_b, cum_b, cum_f))
        is_bt = jnp.logical_and(lane_t >= bwd_lo, lane_t < bwd_hi)
        gt_ref[c * LANE:(c + 1) * LANE, :CHUNK] = jnp.where(is_bt, cum_b.T, cum_f.T)
        gt_ref[c * LANE:(c + 1) * LANE, CHUNK:] = jnp.zeros((LANE, LANE - CHUNK), F32)

    i = pl.program_id(0)
    has_prev = jnp.logical_and(i != 0, i != n_lat_tiles)
    has_next = i < n_lat_tiles - 1
    x = x_ref[...]
    row = lax.broadcasted_iota(jnp.int32, x.shape, 0)
    prev_row = jnp.where(has_prev, p_ref[7:8, :], 0.0)
    next0 = jnp.where(has_next, n_ref[0:1, :], 0.0)
    next1 = jnp.where(has_next, n_ref[1:2, :], 0.0)
    w = w_ref[...]
    y = (w[0:1] * _row_shift(x, -1, row, prev_row, next0, next1) + w[1:2] * x
         + w[2:3] * _row_shift(x, 1, row, prev_row, next0, next1)
         + w[3:4] * _row_shift(x, 2, row, prev_row, next0, next1))
    y = _silu(y)
    qk = GDN_HEADS * GDN_DK
    for h in range(2 * GDN_HEADS):
        seg = y[:, h * GDN_DK:(h + 1) * GDN_DK]
        seg = seg * lax.rsqrt(jnp.sum(seg * seg, axis=-1, keepdims=True) + NORM_EPS)
        if h < GDN_HEADS:
            seg = seg * (GDN_DK ** -0.5)
        o_ref[:, h * GDN_DK:(h + 1) * GDN_DK] = seg
    o_ref[:, 2 * qk:] = y[:, 2 * qk:]


def _ab_prep_call(zab, zsm, conv_w, wg, gbias, gparams, n_lat_tiles):
    nt = n_lat_tiles + 1
    w = 3 * GDN_HEADS * GDN_DK
    gw = wg.shape[1]
    hb = ROW_TILE // 8
    last8 = nt * hb - 1
    nsub = ROW_TILE // CHUNK
    rows = nt * ROW_TILE
    const = lambda i: (0, 0)
    return pl.pallas_call(
        functools.partial(_ab_prep_kernel, n_lat_tiles=n_lat_tiles), grid=(nt,),
        in_specs=[pl.BlockSpec((ROW_TILE, w), lambda i: (i, 1)),
                  pl.BlockSpec((8, w), lambda i: (jnp.maximum(i * hb - 1, 0), 1)),
                  pl.BlockSpec((8, w), lambda i: (jnp.minimum((i + 1) * hb, last8), 1)),
                  pl.BlockSpec((4, w), const),
                  pl.BlockSpec((ROW_TILE, LANE), lambda i: (i, 0)),
                  pl.BlockSpec((LANE, gw), const), pl.BlockSpec((1, gw), const),
                  pl.BlockSpec((8, LANE), const)],
        out_specs=[pl.BlockSpec((ROW_TILE, w), lambda i: (i, 0)),
                   pl.BlockSpec((ROW_TILE, gw), lambda i: (i, 0)),
                   pl.BlockSpec((ROW_TILE, LANE), lambda i: (i, 0)),
                   pl.BlockSpec((nsub * LANE, LANE), lambda i: (i, 0))],
        out_shape=[jax.ShapeDtypeStruct((rows, w), F32), jax.ShapeDtypeStruct((rows, gw), F32),
                   jax.ShapeDtypeStruct((rows, LANE), F32),
                   jax.ShapeDtypeStruct((nt * nsub * LANE, LANE), F32)],
        compiler_params=_cparams(1), name="ab_prep",
    )(zab, zab, zab, conv_w, zsm, wg, gbias, gparams)


def _tile_fwd(s, n_lat_tiles):
    return jnp.where(s == 0, n_lat_tiles, s - 1)


def _tile_bwd(s, n_lat_tiles):
    return jnp.where(s == 0, n_lat_tiles, n_lat_tiles - s)


def _tri_masks(forward):
    row = lax.broadcasted_iota(jnp.int32, (CHUNK, CHUNK), 0)
    col = lax.broadcasted_iota(jnp.int32, (CHUNK, CHUNK), 1)
    incl = (col <= row) if forward else (col >= row)
    strict = (col < row) if forward else (col > row)
    return incl, strict, (row == col).astype(F32)


def _gla_kernel(qf, kf, vf, cf, qb, kb, vb, cb, of, ob, st_ref):
    @pl.when(pl.program_id(0) == 0)
    def _():
        st_ref[...] = jnp.zeros_like(st_ref)

    nsub = ROW_TILE // CHUNK
    nh = GLA_HEADS
    hs = range(nh)
    bidx = lax.broadcasted_iota(jnp.int32, (2 * nh, CHUNK, CHUNK), 0)
    row = lax.broadcasted_iota(jnp.int32, (2 * nh, CHUNK, CHUNK), 1)
    col = lax.broadcasted_iota(jnp.int32, (2 * nh, CHUNK, CHUNK), 2)
    incl = jnp.where(bidx < nh, col - row, row - col) <= 0
    fwd1 = lax.broadcasted_iota(jnp.int32, (2 * nh, 1, 1), 0) < nh
    for c in range(nsub):
        slf = slice(c * CHUNK, (c + 1) * CHUNK)
        cb_ = nsub - 1 - c
        slb = slice(cb_ * CHUNK, (cb_ + 1) * CHUNK)
        heads = lambda rf, rb, w: jnp.stack([rf[slf, h * w:(h + 1) * w] for h in hs]
                                            + [rb[slb, h * w:(h + 1) * w] for h in hs])
        q, k, v, cum = (heads(qf, qb, GLA_DK), heads(kf, kb, GLA_DK), heads(vf, vb, GLA_DV),
                        heads(cf, cb, GLA_DK))
        tot = jnp.where(fwd1, cum[:, CHUNK - 1:CHUNK], cum[:, 0:1])
        qd = (q * (GLA_DK ** -0.5) * jnp.exp(cum)).astype(BF16)
        ki = (k * jnp.exp(-cum)).astype(BF16)
        ke = (k * jnp.exp(tot - cum)).astype(BF16)
        vb16 = v.astype(BF16)
        sc = jnp.where(incl, _bdot_nt(qd, ki), 0.0).astype(BF16)
        st = st_ref[...]
        o = (_bdot(sc, vb16) + _bdot_nt(qd, st.astype(BF16))).astype(of.dtype)
        for h in hs:
            of[slf, h * GLA_DV:(h + 1) * GLA_DV] = o[h]
            ob[slb, h * GLA_DV:(h + 1) * GLA_DV] = o[nh + h]
        st_ref[...] = jnp.exp(tot) * st + _bdot_tn(vb16, ke)


def _gla_call(zab, gla_cum, n_lat_tiles):
    nt = n_lat_tiles + 1
    tf = functools.partial(_tile_fwd, n_lat_tiles=n_lat_tiles)
    tb = functools.partial(_tile_bwd, n_lat_tiles=n_lat_tiles)
    qk_w, v_w = GLA_HEADS * GLA_DK, GLA_HEADS * GLA_DV

    def dir_specs(t, d):
        return [pl.BlockSpec((ROW_TILE, qk_w), lambda s: (t(s), 0)),
                pl.BlockSpec((ROW_TILE, qk_w), lambda s: (t(s), 1)),
                pl.BlockSpec((ROW_TILE, v_w), lambda s: (t(s), 1)),
                pl.BlockSpec((ROW_TILE, qk_w), lambda s: (t(s), d))]

    out_sds = jax.ShapeDtypeStruct((nt * ROW_TILE, v_w), BF16)
    return pl.pallas_call(
        _gla_kernel, grid=(nt,),
        in_specs=dir_specs(tf, 0) + dir_specs(tb, 1),
        out_specs=[pl.BlockSpec((ROW_TILE, v_w), lambda s: (tf(s), 0)),
                   pl.BlockSpec((ROW_TILE, v_w), lambda s: (tb(s), 0))],
        out_shape=[out_sds, out_sds],
        scratch_shapes=[pltpu.VMEM((2 * GLA_HEADS, GLA_DV, GLA_DK), F32)],
        compiler_params=_cparams(1), name="gla_scan",
    )(zab, zab, zab, gla_cum, zab, zab, zab, gla_cum)


INV_BASE = 8


def _unit_triangular_inverse_minus_eye(a):
    n = a.shape[-1]
    row = lax.broadcasted_iota(jnp.int32, (n, n), 0)
    col = lax.broadcasted_iota(jnp.int32, (n, n), 1)

    def same_block(b):
        s = int(math.log2(b))
        return lax.shift_right_logical(row, s) == lax.shift_right_logical(col, s)

    bdot = lambda p, r: _bdot(p.astype(BF16), r.astype(BF16))
    q = jnp.where(same_block(INV_BASE), a, 0.0)
    x = bdot(q, q)
    m = 2
    while 2 * m < INV_BASE:
        r = bdot(jnp.concatenate([q, x], axis=1), x)
        q, x = q + x + r[:, :n], r[:, n:]
        m *= 2
    q = q + x + bdot(q, x)
    b = INV_BASE
    while b < n:
        a_b = jnp.where(jnp.logical_and(same_block(2 * b), jnp.logical_not(same_block(b))), a, 0.0)
        y = a_b + bdot(a_b, q)
        q = q + y + bdot(q, y)
        b *= 2
    return q


def _gdn_kernel(xf, gf, tf_ref, xb, gb, tb_ref, of, ob, s_ref):
    @pl.when(pl.program_id(0) == 0)
    def _():
        s_ref[...] = jnp.zeros_like(s_ref)

    nsub = ROW_TILE // CHUNK
    qk_w = GDN_HEADS * GDN_DK
    nh = GDN_HEADS
    hs = range(nh)
    bidx = lax.broadcasted_iota(jnp.int32, (2 * nh, CHUNK, CHUNK), 0)
    row = lax.broadcasted_iota(jnp.int32, (2 * nh, CHUNK, CHUNK), 1)
    col = lax.broadcasted_iota(jnp.int32, (2 * nh, CHUNK, CHUNK), 2)
    ahead = jnp.where(bidx < nh, col - row, row - col)
    incl = ahead <= 0
    strict = ahead < 0
    fwd1 = lax.broadcasted_iota(jnp.int32, (2 * nh, 1, 1), 0) < nh
    for c in range(nsub):
        slf = slice(c * CHUNK, (c + 1) * CHUNK)
        cb = nsub - 1 - c
        slb = slice(cb * CHUNK, (cb + 1) * CHUNK)
        heads = lambda off: jnp.stack(
            [xf[slf, off + h * GDN_DK:off + (h + 1) * GDN_DK] for h in hs]
            + [xb[slb, off + h * GDN_DK:off + (h + 1) * GDN_DK] for h in hs])
        q, k, v = heads(0), heads(qk_w), heads(2 * qk_w)
        lanes = lambda l0: jnp.stack([gf[slf, l0 + h:l0 + h + 1] for h in hs]
                                     + [gb[slb, l0 + nh + h:l0 + nh + h + 1] for h in hs])
        cum, beta = lanes(GDN_A_LANE), lanes(GDN_B_LANE)
        cum_row = jnp.stack(
            [tf_ref[c * LANE + GDN_A_LANE + h:c * LANE + GDN_A_LANE + h + 1, :CHUNK] for h in hs]
            + [tb_ref[cb * LANE + GDN_A_LANE + nh + h:cb * LANE + GDN_A_LANE + nh + h + 1, :CHUNK] for h in hs])
        tot = jnp.where(fwd1, cum[:, CHUNK - 1:CHUNK], cum[:, 0:1])
        decay = jnp.exp(jnp.where(incl, cum - cum_row, -jnp.inf))
        k16 = k.astype(BF16)
        qk_kk = _bdot_nt(jnp.concatenate([q, k], axis=1).astype(BF16), k16)
        a = jnp.where(strict, -(qk_kk[:, CHUNK:] * decay * beta), 0.0)
        t_m1 = _unit_triangular_inverse_minus_eye(a)
        e_cum = jnp.exp(cum)
        rhs = jnp.concatenate([v * beta, k * (beta * e_cum)], axis=2)
        uw = rhs + _bdot(t_m1.astype(BF16), rhs.astype(BF16))
        u, w = uw[:, :, :GDN_DK], uw[:, :, GDN_DK:]
        sc = jnp.where(incl, qk_kk[:, :CHUNK] * decay, 0.0).astype(BF16)
        ke = (k * jnp.exp(tot - cum)).astype(BF16)
        s = s_ref[...]
        ws_qs = _bdot(jnp.concatenate([w, q * e_cum], axis=1).astype(BF16), s.astype(BF16))
        vn16 = (u - ws_qs[:, :CHUNK]).astype(BF16)
        o = (ws_qs[:, CHUNK:] + _bdot(sc, vn16)).astype(of.dtype)
        for h in hs:
            of[slf, h * GDN_DK:(h + 1) * GDN_DK] = o[h]
            ob[slb, h * GDN_DK:(h + 1) * GDN_DK] = o[nh + h]
        s_ref[...] = jnp.exp(tot) * s + _bdot_tn(ke, vn16)


def _gdn_call(qkv, gates, gates_t, n_lat_tiles):
    nt = n_lat_tiles + 1
    tf = functools.partial(_tile_fwd, n_lat_tiles=n_lat_tiles)
    tb = functools.partial(_tile_bwd, n_lat_tiles=n_lat_tiles)
    w = qkv.shape[1]
    out_w = GDN_HEADS * GDN_DK
    tl = (ROW_TILE // CHUNK) * LANE

    def dir_specs(t):
        return [pl.BlockSpec((ROW_TILE, w), lambda s: (t(s), 0)),
                pl.BlockSpec((ROW_TILE, LANE), lambda s: (t(s), 0)),
                pl.BlockSpec((tl, LANE), lambda s: (t(s), 0))]

    out_sds = jax.ShapeDtypeStruct((nt * ROW_TILE, out_w), BF16)
    return pl.pallas_call(
        _gdn_kernel, grid=(nt,),
        in_specs=dir_specs(tf) + dir_specs(tb),
        out_specs=[pl.BlockSpec((ROW_TILE, out_w), lambda s: (tf(s), 0)),
                   pl.BlockSpec((ROW_TILE, out_w), lambda s: (tb(s), 0))],
        out_shape=[out_sds, out_sds],
        scratch_shapes=[pltpu.VMEM((2 * GDN_HEADS, GDN_DK, GDN_DK), F32)],
        compiler_params=_cparams(1), name="gdn_scan",
    )(qkv, gates, gates_t, qkv, gates, gates_t)


def _merge_ab_kernel(gf, gb, df, db, gz, dz, gn, dn, o_ref):
    gla_w = GLA_HEADS * GLA_DV
    og = gf[...].astype(F32) + gb[...].astype(F32)
    zg = gz[...]
    for h in range(GLA_HEADS):
        sl = slice(h * GLA_DV, (h + 1) * GLA_DV)
        seg = og[:, sl]
        y = seg * lax.rsqrt(jnp.mean(seg * seg, axis=-1, keepdims=True) + NORM_EPS) * gn[...]
        o_ref[:, sl] = (y * _silu(zg[:, sl])).astype(BF16)
    od = df[...].astype(F32) + db[...].astype(F32)
    zd = dz[...]
    for h in range(GDN_HEADS):
        sl = slice(h * GDN_DK, (h + 1) * GDN_DK)
        seg = od[:, sl]
        y = seg * lax.rsqrt(jnp.mean(seg * seg, axis=-1, keepdims=True) + NORM_EPS) * dn[...]
        o_ref[:, gla_w + h * GDN_DK:gla_w + (h + 1) * GDN_DK] = (y * _silu(zd[:, sl])).astype(BF16)


def _out0_kernel(gf, gb, df, db, gz, dz, gn, dn, w_ref, x_ref, c_ref, m_ref, m1_ref, g1_ref,
                 o_ref, h1_ref, a_scr, *, n_lat_tiles, d):
    is_ctx = pl.program_id(0) == n_lat_tiles
    _merge_ab_kernel(gf, gb, df, db, gz, dz, gn, dn, a_scr)
    acc = _dot(a_scr[...], w_ref[...])
    m = m_ref[...]
    gate = jnp.where(is_ctx, m[1:2, 2 * d:], m[0:1, 2 * d:])
    xn = jnp.where(is_ctx, c_ref[...], x_ref[...]) + gate * acc
    o_ref[...] = xn
    m1 = m1_ref[...]
    y = xn * lax.rsqrt(jnp.mean(xn * xn, axis=-1, keepdims=True) + NORM_EPS) * g1_ref[...]
    shift = jnp.where(is_ctx, m1[1:2, :d], m1[0:1, :d])
    scale = jnp.where(is_ctx, m1[1:2, d:2 * d], m1[0:1, d:2 * d])
    h1_ref[...] = (y * (1.0 + scale) + shift).astype(BF16)


def _out0_call(gla_f, gla_b, gdn_f, gdn_b, zab, gla_norm, gdn_norm, w_out, x_lat, x_ctx, mods,
               mods_next, g_next, n_lat_tiles):
    m = gla_f.shape[0]
    w = GLA_HEADS * GLA_DV
    k, d = w_out.shape
    row = lambda i: (i, 0)
    const = lambda i: (0, 0)
    return pl.pallas_call(
        functools.partial(_out0_kernel, n_lat_tiles=n_lat_tiles, d=d), grid=(m // ROW_TILE,),
        in_specs=[pl.BlockSpec((ROW_TILE, w), row)] * 4
        + [pl.BlockSpec((ROW_TILE, w), lambda i: (i, 2)), pl.BlockSpec((ROW_TILE, w), lambda i: (i, 6)),
           pl.BlockSpec((1, GLA_DV), const), pl.BlockSpec((1, GDN_DK), const),
           pl.BlockSpec((k, d), const),
           pl.BlockSpec((ROW_TILE, d), lambda i: (jnp.minimum(i, n_lat_tiles - 1), 0)),
           pl.BlockSpec((ROW_TILE, d), const),
           pl.BlockSpec((8, 3 * d), const), pl.BlockSpec((8, 3 * d), const), pl.BlockSpec((1, d), const)],
        out_specs=[pl.BlockSpec((ROW_TILE, d), row), pl.BlockSpec((ROW_TILE, d), row)],
        out_shape=[jax.ShapeDtypeStruct((m, d), F32), jax.ShapeDtypeStruct((m, d), BF16)],
        scratch_shapes=[pltpu.VMEM((ROW_TILE, k), BF16)],
        compiler_params=_cparams(1), name="out_proj_ab",
    )(gla_f, gla_b, gdn_f, gdn_b, zab, zab, gla_norm.reshape(1, -1), gdn_norm.reshape(1, -1),
      w_out, x_lat, x_ctx, mods, mods_next, g_next.reshape(1, d))


def _lru_gates(seg, wa, ba, wx, bx, lam):
    seg16 = seg.astype(BF16)
    sigmoid = lambda t: 0.5 * jnp.tanh(0.5 * t) + 0.5
    r = sigmoid(_dot(seg16, wa) + ba)
    gi = sigmoid(_dot(seg16, wx) + bx)
    log_a = -LRU_C * _softplus(-lam) * r
    a = jnp.exp(log_a)
    return a, jnp.sqrt(jnp.tanh(-log_a) * (a * a + 1.0)) * (gi * seg)


def _lru_ctx_kernel(x_ref, cw_ref, cb_ref, wa_ref, ba_ref, wx_ref, bx_ref, lam_ref, o_ref, a_scr, d_scr):
    x = x_ref[...]
    rows, width = x.shape
    blk = width // LRU_BLOCKS
    cw = cw_ref[...]
    row = lax.broadcasted_iota(jnp.int32, x.shape, 0)
    zero = jnp.zeros((1, width), F32)
    xc = (cw[0:1] * _row_shift(x, -1, row, zero, zero, zero) + cw[1:2] * x
          + cw[2:3] * _row_shift(x, 1, row, zero, zero, zero)
          + cw[3:4] * _row_shift(x, 2, row, zero, zero, zero) + cb_ref[...])
    for d in range(2):
        for b in range(LRU_BLOCKS):
            sl = slice(b * blk, (b + 1) * blk)
            a, drv = _lru_gates(xc[:, sl], wa_ref[d, b], ba_ref[d, :, sl], wx_ref[d, b],
                                bx_ref[d, :, sl], lam_ref[d, :, sl])
            a_scr[d, :, sl] = a
            d_scr[d, :, sl] = drv

    def body(t, carry):
        hf, hb = carry
        hf = a_scr[0, pl.ds(t, 1), :] * hf + d_scr[0, pl.ds(t, 1), :]
        tb = rows - 1 - t
        hb = a_scr[1, pl.ds(tb, 1), :] * hb + d_scr[1, pl.ds(tb, 1), :]
        return hf, hb

    hf, hb = lax.fori_loop(0, rows, body, (zero, zero), unroll=8)
    o_ref[...] = jnp.zeros_like(o_ref)
    o_ref[0:1, :] = hf
    o_ref[1:2, :] = hb


def _lru_lat_kernel(xf, pf, nf, xb, pb, nb, h0_ref, cw_ref, cb_ref, wa_ref, ba_ref, wx_ref, bx_ref,
                    lam_ref, of, ob, a_scr, d_scr, hl_scr, pc_scr, carry, *, n_colblocks):
    ci, s = pl.program_id(0), pl.program_id(1)

    @pl.when(s == 0)
    def _():
        carry[...] = h0_ref[...]

    cw = cw_ref[...]
    dirs = ((xf, pf, nf, of, s), (xb, pb, nb, ob, n_colblocks - 1 - s))
    for d, (x_ref, p_ref, n_ref, o_ref, cblk) in enumerate(dirs):
        forward = d == 0
        x3 = x_ref[...]
        rows, ncol, tc = x3.shape
        has_prev, has_next = cblk > 0, cblk < n_colblocks - 1
        sub = lax.broadcasted_iota(jnp.int32, (ncol, tc), 0)
        prev_last = jnp.where(has_prev, p_ref[7, ncol - 1:ncol, :], 0.0)
        next_r0 = jnp.where(has_next, n_ref[0, 0:1, :], 0.0)
        next_r1 = jnp.where(has_next, n_ref[1, 0:1, :], 0.0)
        first_m1 = jnp.where(sub == 0, prev_last, pltpu.roll(x3[rows - 1], 1, 0))
        last_p1 = jnp.where(sub == ncol - 1, next_r0, pltpu.roll(x3[0], ncol - 1, 0))
        last_p2 = jnp.where(sub == ncol - 1, next_r1, pltpu.roll(x3[1], ncol - 1, 0))
        xm1 = jnp.concatenate([first_m1[None], x3[:rows - 1]], axis=0)
        xp1 = jnp.concatenate([x3[1:], last_p1[None]], axis=0)
        xp2 = jnp.concatenate([x3[2:], last_p1[None], last_p2[None]], axis=0)
        xc = cw[0:1] * xm1 + cw[1:2] * x3 + cw[2:3] * xp1 + cw[3:4] * xp2 + cb_ref[...]
        a, drv = _lru_gates(xc.reshape(rows * ncol, tc), wa_ref[d, ci], ba_ref[d], wx_ref[d, ci],
                            bx_ref[d], lam_ref[d])
        a_scr[d] = a.reshape(rows, ncol, tc)
        d_scr[d] = drv.reshape(rows, ncol, tc)

    def body(i, carry_):
        out = []
        for d, r in ((0, i), (1, rows - 1 - i)):
            h, p = carry_[2 * d], carry_[2 * d + 1]
            ar = a_scr[d, r]
            h = ar * h + d_scr[d, r]
            p = ar * p
            hl_scr[d, r] = h
            pc_scr[d, r] = p
            out += [h, p]
        return tuple(out)

    zero, one = jnp.zeros((ncol, tc), F32), jnp.ones((ncol, tc), F32)
    lax.fori_loop(0, rows, body, (zero, one, zero, one), unroll=8)

    for d, (x_ref, p_ref, n_ref, o_ref, cblk) in enumerate(dirs):
        forward = d == 0
        end = rows - 1 if forward else 0
        hl_end, pc_end = hl_scr[d, end], pc_scr[d, end]
        h = carry[d:d + 1, :]
        h_in = jnp.zeros((ncol, tc), F32)
        for j in (range(ncol) if forward else reversed(range(ncol))):
            h_in = jnp.where(sub == j, h, h_in)
            h = hl_end[j:j + 1, :] + pc_end[j:j + 1, :] * h
        carry[d:d + 1, :] = h
        o_ref[...] = hl_scr[d] + pc_scr[d] * h_in[None]


def _lru_call(zcd, conv_w, conv_b, wa, ba, wx, bx, lam, n_lat):
    m, ncols_z = zcd.shape
    width = conv_w.shape[1]
    n_cols = GRID_W
    rows = n_lat // n_cols
    tc = width // LRU_BLOCKS
    ncb = n_cols // 8
    full = lambda a: pl.BlockSpec(a.shape, lambda *_: (0,) * a.ndim)
    wa16, wx16 = wa.astype(BF16), wx.astype(BF16)
    cb2, ba3, bx3, lam3 = (conv_b.reshape(1, width), ba.reshape(2, 1, width), bx.reshape(2, 1, width),
                           lam.reshape(2, 1, width))

    params = (conv_w, cb2, wa16, ba3, wx16, bx3, lam3)
    h0 = pl.pallas_call(
        _lru_ctx_kernel, grid=(1,),
        in_specs=[pl.BlockSpec((ROW_TILE, width), lambda i: (n_lat // ROW_TILE, 0))] + [full(p) for p in params],
        out_specs=pl.BlockSpec((8, width), lambda i: (0, 0)),
        out_shape=jax.ShapeDtypeStruct((8, width), F32),
        scratch_shapes=[pltpu.VMEM((2, ROW_TILE, width), F32)] * 2,
        compiler_params=_cparams(1), name="rglru_ctx",
    )(zcd, *params)

    z3 = zcd.reshape(m // n_cols, n_cols, ncols_z)
    hb8 = rows // 8
    bf = lambda s: s
    bb = lambda s: ncb - 1 - s

    def dir_specs(blk):
        return [pl.BlockSpec((rows, 8, tc), lambda c, s: (0, blk(s), c)),
                pl.BlockSpec((8, 8, tc), lambda c, s: (hb8 - 1, jnp.maximum(blk(s) - 1, 0), c)),
                pl.BlockSpec((8, 8, tc), lambda c, s: (0, jnp.minimum(blk(s) + 1, ncb - 1), c))]

    vec = lambda lead: pl.BlockSpec((lead, 1, tc), lambda c, s: (0, 0, c))
    out_sds = jax.ShapeDtypeStruct((rows, n_cols, width), F32)
    hf, hb = pl.pallas_call(
        functools.partial(_lru_lat_kernel, n_colblocks=ncb), grid=(width // tc, ncb),
        in_specs=dir_specs(bf) + dir_specs(bb)
        + [pl.BlockSpec((8, tc), lambda c, s: (0, c)),
           pl.BlockSpec((4, tc), lambda c, s: (0, c)), pl.BlockSpec((1, tc), lambda c, s: (0, c)),
           full(wa16), vec(2), full(wx16), vec(2), vec(2)],
        out_specs=[pl.BlockSpec((rows, 8, tc), lambda c, s: (0, bf(s), c)),
                   pl.BlockSpec((rows, 8, tc), lambda c, s: (0, bb(s), c))],
        out_shape=[out_sds, out_sds],
        scratch_shapes=[pltpu.VMEM((2, rows, 8, tc), F32)] * 4 + [pltpu.VMEM((8, tc), F32)],
        compiler_params=_cparams(2), name="rglru_scan",
    )(z3, z3, z3, z3, z3, z3, h0, conv_w, cb2, wa16, ba3, wx16, bx3, lam3)
    return hf.reshape(n_lat, width), hb.reshape(n_lat, width)


def _hy_prep_kernel(*refs, n_lat_tiles):
    xs, (w_ref, b_ref, x0_ref, z_ref) = refs[:9], refs[9:]
    i = pl.program_id(0)
    has_prev = i > 0
    has_next = i < n_lat_tiles - 1
    outs = []
    width = x0_ref.shape[1]
    for j in range(3):
        x_ref, p_ref, n_ref = xs[3 * j:3 * j + 3]
        x = x_ref[...]
        row = lax.broadcasted_iota(jnp.int32, x.shape, 0)
        prev_row = jnp.where(has_prev, p_ref[7:8, :], 0.0)
        next0 = jnp.where(has_next, n_ref[0:1, :], 0.0)
        sl = slice(j * width, (j + 1) * width)
        outs.append(w_ref[0:1, sl] * _row_shift(x, -1, row, prev_row, next0, next0)
                    + w_ref[1:2, sl] * x
                    + w_ref[2:3, sl] * _row_shift(x, 1, row, prev_row, next0, next0)
                    + b_ref[:, sl])
    x0_ref[...] = outs[0]
    z_ref[...] = outs[1] * outs[2]


def _hy_prep_call(zcd, conv_w, conv_b, n_lat_tiles):
    width = conv_w.shape[1] // 3
    hb = ROW_TILE // 8
    last8 = n_lat_tiles * hb - 1
    in_specs, args = [], []
    for j in range(3):
        cbk = 2 + j
        in_specs += [pl.BlockSpec((ROW_TILE, width), lambda i, cbk=cbk: (i, cbk)),
                     pl.BlockSpec((8, width), lambda i, cbk=cbk: (jnp.maximum(i * hb - 1, 0), cbk)),
                     pl.BlockSpec((8, width), lambda i, cbk=cbk: (jnp.minimum((i + 1) * hb, last8), cbk))]
        args += [zcd, zcd, zcd]
    in_specs += [pl.BlockSpec((3, 3 * width), lambda i: (0, 0)), pl.BlockSpec((1, 3 * width), lambda i: (0, 0))]
    out_sds = jax.ShapeDtypeStruct((n_lat_tiles * ROW_TILE, width), F32)
    return pl.pallas_call(
        functools.partial(_hy_prep_kernel, n_lat_tiles=n_lat_tiles), grid=(n_lat_tiles,),
        in_specs=in_specs,
        out_specs=[pl.BlockSpec((ROW_TILE, width), lambda i: (i, 0))] * 2,
        out_shape=[out_sds, out_sds],
        compiler_params=_cparams(1), name="hyena_prep",
    )(*args, conv_w, conv_b.reshape(1, -1))


def _hy_filter_kernel(e_ref, w1, b1, w2, b2, w3, fr, dec, f_ref, ssq_ref):
    i = pl.program_id(0)
    e = e_ref[...]
    hid = jnp.sin(fr[...] * (_dot(e, w1[...], HI) + b1[...]))
    hid = jnp.sin(fr[...] * (_dot(hid, w2[...], HI) + b2[...]))
    filt = _dot(hid.astype(BF16), w3[...]) * jnp.exp(-e[:, 0:1] * dec[...])
    half = filt.shape[1] // 2
    row = lax.broadcasted_iota(jnp.int32, filt.shape, 0)
    col = lax.broadcasted_iota(jnp.int32, filt.shape, 1)
    unused = jnp.logical_and(jnp.logical_and(i == 0, row == 0), col >= half)
    filt = jnp.where(unused, 0.0, filt)
    f_ref[...] = filt

    @pl.when(i == 0)
    def _():
        ssq_ref[...] = jnp.zeros_like(ssq_ref)

    ssq_ref[...] = ssq_ref[...] + jnp.sum(filt * filt, axis=0, keepdims=True)


def _hy_filter_call(emb, w1, b1, w2, b2, w3, freq, decay):
    length = emb.shape[0]
    n_out = w3.shape[1]
    hid = w1.shape[1]
    pad_r = lambda a: jnp.pad(a, ((0, LANE - a.shape[0]), (0, 0)))
    pad_c = lambda a: jnp.pad(a, ((0, 0), (0, LANE - a.shape[1])))
    params = (pad_c(pad_r(w1)), pad_c(b1.reshape(1, hid)), pad_c(pad_r(w2)), pad_c(b2.reshape(1, hid)),
              pad_r(w3).astype(BF16), pad_c(freq.reshape(1, hid)), decay.reshape(1, n_out))
    full = lambda a: pl.BlockSpec(a.shape, lambda i: (0,) * a.ndim)
    return pl.pallas_call(
        _hy_filter_kernel, grid=(length // ROW_TILE,),
        in_specs=[pl.BlockSpec((ROW_TILE, LANE), lambda i: (i, 0))] + [full(p) for p in params],
        out_specs=[pl.BlockSpec((ROW_TILE, n_out), lambda i: (i, 0)),
                   pl.BlockSpec((8, n_out), lambda i: (0, 0))],
        out_shape=[jax.ShapeDtypeStruct((length, n_out), F32), jax.ShapeDtypeStruct((8, n_out), F32)],
        compiler_params=_cparams(1), name="hyena_filter",
    )(emb, *params)


SUB = 8


def _dft_a_kernel(x_ref, c_ref, s_ref, re_ref, im_ref):
    xt = pltpu.einshape("kjc->jkc", x_ref[...])
    res_re, res_im = [], []
    for j in range(SUB):
        xb = xt[j].astype(BF16)
        res_re.append(_dot(c_ref[...], xb))
        res_im.append(_dot(s_ref[...], xb))
    re_ref[...] = pltpu.einshape("jkc->kjc", jnp.stack(res_re))
    im_ref[...] = pltpu.einshape("jkc->kjc", jnp.stack(res_im))


def _dft_a_call(x, cmat, smat, n1):
    length, ch = x.shape
    n2, n2h = cmat.shape
    tc = 512
    x4 = x.reshape(n2h, n1 // SUB, SUB, ch)
    out_sds = jax.ShapeDtypeStruct((n2, n1 // SUB, SUB, ch), F32)
    re, im = pl.pallas_call(
        _dft_a_kernel, grid=(ch // tc, n1 // SUB),
        in_specs=[pl.BlockSpec((n2h, None, SUB, tc), lambda c, i: (0, i, 0, c)),
                  pl.BlockSpec((n2, n2h), lambda c, i: (0, 0)),
                  pl.BlockSpec((n2, n2h), lambda c, i: (0, 0))],
        out_specs=[pl.BlockSpec((n2, None, SUB, tc), lambda c, i: (0, i, 0, c))] * 2,
        out_shape=[out_sds, out_sds],
        compiler_params=_cparams(2), name="dft_stage_a",
    )(x4, cmat, smat)
    return re.reshape(n2, n1, ch), im.reshape(n2, n1, ch)


def _dft_b_filter_kernel(pre, pim, fre, fim, g_ref, sp_ref, sf_ref, hre_ref, him_ref, *, n1, inv_n):
    scale = lax.rsqrt(sp_ref[0:1, :] + sf_ref[0:1, :] + NORM_EPS) * inv_n
    for j in range(SUB):
        g = g_ref[j]
        xp = _dot(g, jnp.concatenate([pre[j], pim[j]], axis=0).astype(BF16))
        xf = _dot(g, jnp.concatenate([fre[j], fim[j]], axis=0).astype(BF16))
        hre_ref[j] = (xp[:n1] + xf[:n1]) * scale
        him_ref[j] = (xp[n1:] - xf[n1:]) * scale


def _dft_b_filter_call(a_re, a_im, gmat, ssq):
    n2, n1, ch2 = a_re.shape
    ch = ch2 // 2
    tc = 256
    nct = ch // tc
    spec = lambda half: pl.BlockSpec((SUB, n1, tc), lambda k, c: (k, 0, half * nct + c))
    out_sds = jax.ShapeDtypeStruct((n2, n1, ch), F32)
    return pl.pallas_call(
        functools.partial(_dft_b_filter_kernel, n1=n1, inv_n=1.0 / (n1 * n2)), grid=(n2 // SUB, nct),
        in_specs=[spec(0), spec(0), spec(1), spec(1),
                  pl.BlockSpec((SUB, 2 * n1, 2 * n1), lambda k, c: (k, 0, 0)),
                  pl.BlockSpec((8, tc), lambda k, c: (0, c)),
                  pl.BlockSpec((8, tc), lambda k, c: (0, nct + c))],
        out_specs=[pl.BlockSpec((SUB, n1, tc), lambda k, c: (k, 0, c))] * 2,
        out_shape=[out_sds, out_sds],
        compiler_params=_cparams(2), name="dft_stage_b_filter",
    )(a_re, a_im, a_re, a_im, gmat, ssq, ssq)


def _dft_b_kernel(are, aim, hre, him, g_ref, gi_ref, bre_ref, bim_ref, *, n1):
    for j in range(SUB):
        x = _dot(g_ref[j], jnp.concatenate([are[j], aim[j]], axis=0).astype(BF16))
        xr, xi = x[:n1], x[n1:]
        hr, hi = hre[j], him[j]
        y = jnp.concatenate([xr * hr - xi * hi, xr * hi + xi * hr], axis=0).astype(BF16)
        b = _dot(gi_ref[j], y)
        bre_ref[:, j, :] = b[:n1]
        bim_ref[:, j, :] = b[n1:]


def _dft_b_call(a_re, a_im, h_re, h_im, gmat, gimat):
    n2, n1, ch = a_re.shape
    tc = 512
    slab = pl.BlockSpec((SUB, n1, tc), lambda k, c: (k, 0, c))
    gspec = pl.BlockSpec((SUB, 2 * n1, 2 * n1), lambda k, c: (k, 0, 0))
    out_sds = jax.ShapeDtypeStruct((n1, n2 // SUB, SUB, ch), F32)
    re, im = pl.pallas_call(
        functools.partial(_dft_b_kernel, n1=n1), grid=(n2 // SUB, ch // tc),
        in_specs=[slab] * 4 + [gspec, gspec],
        out_specs=[pl.BlockSpec((n1, None, SUB, tc), lambda k, c: (0, k, 0, c))] * 2,
        out_shape=[out_sds, out_sds],
        compiler_params=_cparams(2), name="dft_stage_b",
    )(a_re, a_im, h_re, h_im, gmat, gimat)
    return re.reshape(n1, n2, ch), im.reshape(n1, n2, ch)


def _dft_c_kernel(bre, bim, x0_ref, z_ref, hz_ref, bias_ref, c_ref, s_ref, o_ref):
    ys = [_dot(c_ref[...], bre[j].astype(BF16)) + _dot(s_ref[...], bim[j].astype(BF16)) for j in range(SUB)]
    y = pltpu.einshape("jkc->kjc", jnp.stack(ys))
    o_ref[...] = x0_ref[...] * (y + z_ref[...] * bias_ref[...]) * _silu(hz_ref[...])


def _dft_c_call(b_re, b_im, x0, z, zcd, bias, cmat, smat):
    n1, n2, ch = b_re.shape
    n2h = n2
    length = n1 * n2h
    tc = 256
    nct = ch // tc
    view = lambda a: a.reshape(a.shape[0] // n1, n1 // SUB, SUB, a.shape[1])
    tspec = lambda cblk: pl.BlockSpec((n2h, None, SUB, tc), lambda i, c: (0, i, 0, cblk * nct + c))
    out = pl.pallas_call(
        _dft_c_kernel, grid=(n1 // SUB, nct),
        in_specs=[pl.BlockSpec((SUB, n2, tc), lambda i, c: (i, 0, c))] * 2
        + [tspec(0), tspec(0), tspec(5), pl.BlockSpec((1, tc), lambda i, c: (0, c)),
           pl.BlockSpec((n2h, n2), lambda i, c: (0, 0)), pl.BlockSpec((n2h, n2), lambda i, c: (0, 0))],
        out_specs=tspec(0),
        out_shape=jax.ShapeDtypeStruct((n2h, n1 // SUB, SUB, ch), F32),
        compiler_params=_cparams(2), name="dft_stage_c",
    )(b_re, b_im, view(x0), view(z), view(zcd), bias.reshape(1, ch), cmat, smat)
    return out.reshape(length, ch)


def _dft_constants(n1, n2):
    n = n1 * n2
    n2h = n2 // 2
    k2 = jnp.arange(n2h, dtype=jnp.int32)
    ang_a = (math.pi / n2) * ((k2[None, :] * (2 * k2[:, None] + 1)) % (2 * n2)).astype(F32)
    ca, sa = jnp.cos(ang_a), -jnp.sin(ang_a)
    cc, sc = jnp.cos(ang_a).T, -jnp.sin(ang_a).T
    i1 = jnp.arange(n1, dtype=jnp.int32)
    ang_f = (2.0 * math.pi / n1) * ((i1[:, None] * i1[None, :]) % n1).astype(F32)
    ang_t = (math.pi / n) * ((2 * k2[:, None] + 1) * i1[None, :]).astype(F32)
    fr, fi = jnp.cos(ang_f)[None], -jnp.sin(ang_f)[None]
    wr, wi = jnp.cos(ang_t)[:, None, :], -jnp.sin(ang_t)[:, None, :]
    gr, gi = fr * wr - fi * wi, fr * wi + fi * wr
    g = jnp.concatenate([jnp.concatenate([gr, -gi], axis=2), jnp.concatenate([gi, gr], axis=2)], axis=1)
    tr, ti = jnp.swapaxes(gr, 1, 2), -jnp.swapaxes(gi, 1, 2)
    ginv = jnp.concatenate([jnp.concatenate([tr, -ti], axis=2), jnp.concatenate([ti, tr], axis=2)], axis=1)
    b = lambda a: a.astype(BF16)
    return b(ca), b(sa), b(cc), b(sc), b(g), b(ginv)


def _final_kernel(hf, hb, lz, hy, w1, w2, x_ref, m_ref, g_ref, o_ref, *, d):
    a1 = ((hf[...] + hb[...]) * _silu(lz[...])).astype(BF16)
    acc = _dot(a1, w1[...]) + _dot(hy[...].astype(BF16), w2[...])
    xn = x_ref[...] + m_ref[0:1, 2 * d:] * acc
    o_ref[...] = xn * lax.rsqrt(jnp.mean(xn * xn, axis=-1, keepdims=True) + NORM_EPS) * g_ref[...]


def _final_call(hf, hb, zcd, hy, w_out, x1, mods, final_g, n_lat_tiles):
    d = w_out.shape[1]
    half = w_out.shape[0] // 2
    row = lambda i: (i, 0)
    const = lambda i: (0, 0)
    tm = _row_tile(n_lat_tiles, 2)
    once = dict(pipeline_mode=pl.Buffered(1))
    return pl.pallas_call(
        functools.partial(_final_kernel, d=d), grid=(n_lat_tiles * ROW_TILE // tm,),
        in_specs=[pl.BlockSpec((tm, half), row), pl.BlockSpec((tm, half), row),
                  pl.BlockSpec((tm, half), lambda i: (i, 1)), pl.BlockSpec((tm, half), row),
                  pl.BlockSpec((half, d), const, **once), pl.BlockSpec((half, d), lambda i: (1, 0), **once),
                  pl.BlockSpec((tm, d), row), pl.BlockSpec((8, 3 * d), const),
                  pl.BlockSpec((1, d), const)],
        out_specs=pl.BlockSpec((tm, d), row),
        out_shape=jax.ShapeDtypeStruct((n_lat_tiles * ROW_TILE, d), F32),
        compiler_params=_cparams(1), name="out_proj_cd_final",
    )(hf, hb, zcd, hy, w_out, w_out, x1, mods, final_g.reshape(1, d))


def kernel(x, c, ctx, c_ctx, mod_w, mod_b, norm_g, ab_w_in, ab_gla_wg2, ab_gla_bg2, ab_gla_norm, ab_gdn_conv, ab_gdn_a_log, ab_gdn_dt_bias, ab_gdn_norm, ab_w_out, cd_w_in, cd_lru_conv_w, cd_lru_conv_b, cd_lru_wa, cd_lru_ba, cd_lru_wx, cd_lru_bx, cd_lru_lambda, cd_hy_conv_w, cd_hy_conv_b, cd_hy_w1, cd_hy_b1, cd_hy_w2, cd_hy_b2, cd_hy_w3, cd_hy_freq, cd_hy_decay, cd_hy_bias, cd_w_out, final_g):
    _, n_lat, d = x.shape
    assert x.shape[0] == 1 and ctx.shape[1] == ROW_TILE and n_lat % ROW_TILE == 0
    assert mod_w.shape[0] == 2 and d == 2 * GLA_HEADS * GLA_DV
    nlt = n_lat // ROW_TILE
    x_lat, x_ctx = x[0], ctx[0]

    cvec = jnp.zeros((8, d), F32).at[0].set(c[0]).at[1].set(c_ctx)
    mods = _mod_call(cvec, mod_w, mod_b)

    gla_qk, gla_v = GLA_HEADS * GLA_DK, GLA_HEADS * GLA_DV
    gdn_qk = GDN_HEADS * GDN_DK
    sizes = (gla_qk, gla_qk, gla_v, 2 * GLA_RANK, gla_v, 3 * gdn_qk, 2 * GDN_HEADS, 2 * GDN_HEADS, gdn_qk)
    gq, gk, gv, glr, gz, dqkv, da, db, dz = jnp.split(ab_w_in[0], np.cumsum(sizes)[:-1].tolist(), axis=1)
    w_main = jnp.concatenate([gq, gk, gv, gz, dqkv, dz], axis=1).astype(BF16)
    n_small = glr.shape[1] + da.shape[1] + db.shape[1]
    w_small = jnp.concatenate([glr, da, db, jnp.zeros((d, LANE - n_small), F32)], axis=1).astype(BF16)

    h0 = _norm_mod_call(x_lat, x_ctx, mods[0], norm_g[0], nlt)
    zab = _mm_call(h0, w_main, 1024, "in_proj_ab")
    zsm = _mm_call(h0, w_small, LANE, "in_proj_ab_small")

    gw = GLA_HEADS * GLA_DK
    wg = jnp.zeros((LANE, 2 * gw), F32)
    for dd in range(2):
        wg = wg.at[dd * GLA_RANK:(dd + 1) * GLA_RANK, dd * gw:(dd + 1) * gw].set(ab_gla_wg2[0, dd])
    gbias = ab_gla_bg2[0].reshape(1, 2 * gw)
    gparams = jnp.zeros((8, LANE), F32)
    gparams = gparams.at[0, GDN_A_LANE:GDN_B_LANE].set(ab_gdn_a_log[0].reshape(-1))
    gparams = gparams.at[1, GDN_A_LANE:GDN_B_LANE].set(ab_gdn_dt_bias[0].reshape(-1))
    qkv, gla_cum, gates, gates_t = _ab_prep_call(zab, zsm, ab_gdn_conv[0], wg.astype(BF16), gbias, gparams, nlt)
    gla_f, gla_b = _gla_call(zab, gla_cum, nlt)
    gdn_f, gdn_b = _gdn_call(qkv, gates, gates_t, nlt)

    x1, h1 = _out0_call(gla_f, gla_b, gdn_f, gdn_b, zab, ab_gla_norm[0], ab_gdn_norm[0],
                        ab_w_out[0].astype(BF16), x_lat, x_ctx, mods[0], mods[1], norm_g[1], nlt)

    zcd = _mm_call(h1, cd_w_in[0].astype(BF16), 1024, "in_proj_cd")

    lru_f, lru_b = _lru_call(zcd, cd_lru_conv_w[0], cd_lru_conv_b[0], cd_lru_wa[0], cd_lru_ba[0],
                             cd_lru_wx[0], cd_lru_bx[0], cd_lru_lambda[0], n_lat)

    x0, z = _hy_prep_call(zcd, cd_hy_conv_w[0], cd_hy_conv_b[0], nlt)

    pos = jnp.arange(n_lat, dtype=F32)
    t01 = pos / max(n_lat - 1, 1)
    bands = jnp.linspace(1e-4, HY_BANDS - 1, HY_BANDS, dtype=F32)
    ang = (2.0 * math.pi / n_lat) * pos[:, None] * bands[None, :]
    emb = jnp.concatenate([t01[:, None], jnp.cos(ang), jnp.sin(ang)], axis=-1)
    emb = jnp.pad(emb, ((0, 0), (0, LANE - emb.shape[1])))
    filt, ssq = _hy_filter_call(emb, cd_hy_w1[0], cd_hy_b1[0], cd_hy_w2[0], cd_hy_b2[0], cd_hy_w3[0],
                                cd_hy_freq[0], cd_hy_decay[0])

    n1 = LANE if n_lat >= LANE * LANE else 32
    n2 = 2 * n_lat // n1
    ca, sa, cc, sc, gmat, gimat = _dft_constants(n1, n2)
    fa_re, fa_im = _dft_a_call(filt, ca, sa, n1)
    h_re, h_im = _dft_b_filter_call(fa_re, fa_im, gmat, ssq)
    za_re, za_im = _dft_a_call(z, ca, sa, n1)
    b_re, b_im = _dft_b_call(za_re, za_im, h_re, h_im, gmat, gimat)
    y_hy = _dft_c_call(b_re, b_im, x0, z, zcd, cd_hy_bias[0], cc, sc)

    out = _final_call(lru_f, lru_b, zcd, y_hy, cd_w_out[0].astype(BF16), x1, mods[1], final_g, nlt)
    return out[None]
```

```python
import functools
import math

import numpy as np
import jax
import jax.numpy as jnp
from jax import lax
from jax.experimental import pallas as pl
from jax.experimental.pallas import tpu as pltpu

F32 = jnp.float32
BF16 = jnp.bfloat16
HI = lax.Precision.HIGHEST

NORM_EPS = 1e-6
CHUNK = 64
ROW_TILE = 256
GRID_W = 64
LANE = 128
GLA_HEADS, GLA_DK, GLA_DV, GLA_RANK, GLA_GATE_NORM = 4, 128, 256, 16, 16.0
GDN_HEADS, GDN_DK = 8, 128
LRU_BLOCKS, LRU_C = 8, 8.0
HY_BANDS = 16
VMEM_LIMIT = 56 * 1024 * 1024


def _cparams(n_axes):
    return pltpu.CompilerParams(dimension_semantics=("arbitrary",) * n_axes,
                                vmem_limit_bytes=VMEM_LIMIT)


def _silu(x):
    return x * jax.nn.sigmoid(x)


def _softplus(x):
    return jnp.maximum(x, 0.0) + jnp.log1p(jnp.exp(-jnp.abs(x)))


def _dot(a, b, precision=None):
    return jnp.dot(a, b, precision=precision, preferred_element_type=F32)


def _dot_nt(a, b, precision=None):
    return lax.dot_general(a, b, (((1,), (1,)), ((), ())), precision=precision,
                           preferred_element_type=F32)


def _dot_tn(a, b, precision=None):
    return lax.dot_general(a, b, (((0,), (0,)), ((), ())), precision=precision,
                           preferred_element_type=F32)


def _mask_dot_f32(mask16, x):
    hi = x.astype(BF16)
    r = x - hi.astype(F32)
    mid = r.astype(BF16)
    lo = (r - mid.astype(F32)).astype(BF16)
    return _dot(mask16, hi) + _dot(mask16, mid) + _dot(mask16, lo)


def _bdot(a, b):
    return lax.dot_general(a, b, (((2,), (1,)), ((0,), (0,))), preferred_element_type=F32)


def _bdot_nt(a, b):
    return lax.dot_general(a, b, (((2,), (2,)), ((0,), (0,))), preferred_element_type=F32)


def _bdot_tn(a, b):
    return lax.dot_general(a, b, (((1,), (1,)), ((0,), (0,))), preferred_element_type=F32)


def _mod_kernel(c_ref, w_ref, b_ref, o_ref):
    o_ref[...] = _dot(_silu(c_ref[...]), w_ref[...], HI) + b_ref[...]


def _mod_call(cvec, mod_w, mod_b):
    depth, d, n = mod_w.shape
    tn = 512
    return pl.pallas_call(
        _mod_kernel, grid=(depth, n // tn),
        in_specs=[pl.BlockSpec((8, d), lambda l, j: (0, 0)),
                  pl.BlockSpec((None, d, tn), lambda l, j: (l, 0, j)),
                  pl.BlockSpec((None, 1, tn), lambda l, j: (l, 0, j))],
        out_specs=pl.BlockSpec((None, 8, tn), lambda l, j: (l, 0, j)),
        out_shape=jax.ShapeDtypeStruct((depth, 8, n), F32),
        compiler_params=_cparams(2), name="adaln_mod",
    )(cvec, mod_w, mod_b.reshape(depth, 1, n))


def _norm_mod_kernel(*refs, two_inputs, n_lat_tiles, d):
    if two_inputs:
        x_ref, c_ref, m_ref, g_ref, o_ref = refs
    else:
        x_ref, m_ref, g_ref, o_ref = refs
    is_ctx = pl.program_id(0) == n_lat_tiles
    xt = x_ref[...]
    if two_inputs:
        xt = jnp.where(is_ctx, c_ref[...], xt)
    y = xt * lax.rsqrt(jnp.mean(xt * xt, axis=-1, keepdims=True) + NORM_EPS) * g_ref[...]
    m = m_ref[...]
    shift = jnp.where(is_ctx, m[1:2, :d], m[0:1, :d])
    scale = jnp.where(is_ctx, m[1:2, d:2 * d], m[0:1, d:2 * d])
    o_ref[...] = (y * (1.0 + scale) + shift).astype(BF16)


def _norm_mod_call(x_lat, x_ctx, mods, g, n_lat_tiles):
    d = x_lat.shape[1]
    nt = n_lat_tiles + 1
    two = x_ctx is not None
    in_specs = [pl.BlockSpec((ROW_TILE, d), (lambda i: (jnp.minimum(i, n_lat_tiles - 1), 0)) if two
                             else (lambda i: (i, 0)))]
    args = [x_lat]
    if two:
        in_specs.append(pl.BlockSpec((ROW_TILE, d), lambda i: (0, 0)))
        args.append(x_ctx)
    in_specs += [pl.BlockSpec((8, 3 * d), lambda i: (0, 0)), pl.BlockSpec((1, d), lambda i: (0, 0))]
    args += [mods, g.reshape(1, d)]
    return pl.pallas_call(
        functools.partial(_norm_mod_kernel, two_inputs=two, n_lat_tiles=n_lat_tiles, d=d),
        grid=(nt,), in_specs=in_specs,
        out_specs=pl.BlockSpec((ROW_TILE, d), lambda i: (i, 0)),
        out_shape=jax.ShapeDtypeStruct((nt * ROW_TILE, d), BF16),
        compiler_params=_cparams(1), name="norm_mod",
    )(*args)


def _mm_kernel(a_ref, w_ref, o_ref):
    o_ref[...] = _dot(a_ref[...], w_ref[...])


def _row_tile(n_tiles, max_mult):
    k = max(m for m in range(1, max_mult + 1) if n_tiles % m == 0)
    return k * ROW_TILE


def _mm_call(a, w, tn, name):
    m, k = a.shape
    n = w.shape[1]
    tm = _row_tile(m // ROW_TILE, 5)
    return pl.pallas_call(
        _mm_kernel, grid=(n // tn, m // tm),
        in_specs=[pl.BlockSpec((tm, k), lambda j, i: (i, 0)),
                  pl.BlockSpec((k, tn), lambda j, i: (0, j))],
        out_specs=pl.BlockSpec((tm, tn), lambda j, i: (i, j)),
        out_shape=jax.ShapeDtypeStruct((m, n), F32),
        compiler_params=_cparams(2), name=name,
    )(a, w)


def _row_shift(x, k, row, prev_row, next0, next1):
    n = x.shape[0]
    if k == -1:
        return jnp.where(row == 0, prev_row, pltpu.roll(x, 1, 0))
    if k == 1:
        return jnp.where(row == n - 1, next0, pltpu.roll(x, n - 1, 0))
    r = pltpu.roll(x, n - 2, 0)
    return jnp.where(row == n - 2, next0, jnp.where(row == n - 1, next1, r))


GDN_A_LANE = 2 * GLA_RANK
GDN_B_LANE = GDN_A_LANE + 2 * GDN_HEADS


def _ab_prep_kernel(x_ref, p_ref, n_ref, w_ref, z_ref, wg_ref, gb_ref, gp_ref,
                    o_ref, cum_ref, gates_ref, gt_ref, *, n_lat_tiles):
    zs = z_ref[...]
    logit = _dot(zs.astype(BF16), wg_ref[...]) + gb_ref[...]
    g = -_softplus(-logit) / GLA_GATE_NORM
    log_a = -jnp.exp(gp_ref[0:1, :]) * _softplus(zs + gp_ref[1:2, :])
    beta = jax.nn.sigmoid(zs)
    lower, _, _ = _tri_masks(True)
    upper, _, _ = _tri_masks(False)
    lower, upper = lower.astype(BF16), upper.astype(BF16)
    half = cum_ref.shape[1] // 2
    lane = lax.broadcasted_iota(jnp.int32, (CHUNK, LANE), 1)
    lane_t = lax.broadcasted_iota(jnp.int32, (LANE, CHUNK), 0)
    bwd_lo, bwd_hi = GDN_A_LANE + GDN_HEADS, GDN_B_LANE
    for c in range(ROW_TILE // CHUNK):
        sl = slice(c * CHUNK, (c + 1) * CHUNK)
        cum_ref[sl, :half] = _mask_dot_f32(lower, g[sl, :half])
        cum_ref[sl, half:] = _mask_dot_f32(upper, g[sl, half:])
        cum_f = _mask_dot_f32(lower, log_a[sl])
        cum_b = _mask_dot_f32(upper, log_a[sl])
        is_b = jnp.logical_and(lane >= bwd_lo, lane < bwd_hi)
        gates_ref[sl, :] = jnp.where(lane >= GDN_B_LANE, beta[sl], jnp.where(is_b, cum_b, cum_f))
        is_bt = jnp.logical_and(lane_t >= bwd_lo, lane_t < bwd_hi)
        gt_ref[c * LANE:(c + 1) * LANE, :CHUNK] = jnp.where(is_bt, cum_b.T, cum_f.T)
        gt_ref[c * LANE:(c + 1) * LANE, CHUNK:] = jnp.zeros((LANE, LANE - CHUNK), F32)

    i = pl.program_id(0)
    has_prev = jnp.logical_and(i != 0, i != n_lat_tiles)
    has_next = i < n_lat_tiles - 1
    x = x_ref[...]
    row = lax.broadcasted_iota(jnp.int32, x.shape, 0)
    prev_row = jnp.where(has_prev, p_ref[7:8, :], 0.0)
    next0 = jnp.where(has_next, n_ref[0:1, :], 0.0)
    next1 = jnp.where(has_next, n_ref[1:2, :], 0.0)
    w = w_ref[...]
    y = (w[0:1] * _row_shift(x, -1, row, prev_row, next0, next1) + w[1:2] * x
         + w[2:3] * _row_shift(x, 1, row, prev_row, next0, next1)
         + w[3:4] * _row_shift(x, 2, row, prev_row, next0, next1))
    y = _silu(y)
    qk = GDN_HEADS * GDN_DK
    for h in range(2 * GDN_HEADS):
        seg = y[:, h * GDN_DK:(h + 1) * GDN_DK]
        seg = seg * lax.rsqrt(jnp.sum(seg * seg, axis=-1, keepdims=True) + NORM_EPS)
        if h < GDN_HEADS:
            seg = seg * (GDN_DK ** -0.5)
        o_ref[:, h * GDN_DK:(h + 1) * GDN_DK] = seg
    o_ref[:, 2 * qk:] = y[:, 2 * qk:]


def _ab_prep_call(zab, zsm, conv_w, wg, gbias, gparams, n_lat_tiles):
    nt = n_lat_tiles + 1
    w = 3 * GDN_HEADS * GDN_DK
    gw = wg.shape[1]
    hb = ROW_TILE // 8
    last8 = nt * hb - 1
    nsub = ROW_TILE // CHUNK
    rows = nt * ROW_TILE
    const = lambda i: (0, 0)
    return pl.pallas_call(
        functools.partial(_ab_prep_kernel, n_lat_tiles=n_lat_tiles), grid=(nt,),
        in_specs=[pl.BlockSpec((ROW_TILE, w), lambda i: (i, 1)),
                  pl.BlockSpec((8, w), lambda i: (jnp.maximum(i * hb - 1, 0), 1)),
                  pl.BlockSpec((8, w), lambda i: (jnp.minimum((i + 1) * hb, last8), 1)),
                  pl.BlockSpec((4, w), const),
                  pl.BlockSpec((ROW_TILE, LANE), lambda i: (i, 0)),
                  pl.BlockSpec((LANE, gw), const), pl.BlockSpec((1, gw), const),
                  pl.BlockSpec((8, LANE), const)],
        out_specs=[pl.BlockSpec((ROW_TILE, w), lambda i: (i, 0)),
                   pl.BlockSpec((ROW_TILE, gw), lambda i: (i, 0)),
                   pl.BlockSpec((ROW_TILE, LANE), lambda i: (i, 0)),
                   pl.BlockSpec((nsub * LANE, LANE), lambda i: (i, 0))],
        out_shape=[jax.ShapeDtypeStruct((rows, w), F32), jax.ShapeDtypeStruct((rows, gw), F32),
                   jax.ShapeDtypeStruct((rows, LANE), F32),
                   jax.ShapeDtypeStruct((nt * nsub * LANE, LANE), F32)],
        compiler_params=_cparams(1), name="ab_prep",
    )(zab, zab, zab, conv_w, zsm, wg, gbias, gparams)


def _tile_fwd(s, n_lat_tiles):
    return jnp.where(s == 0, n_lat_tiles, s - 1)


def _tile_bwd(s, n_lat_tiles):
    return jnp.where(s == 0, n_lat_tiles, n_lat_tiles - s)


def _tri_masks(forward):
    row = lax.broadcasted_iota(jnp.int32, (CHUNK, CHUNK), 0)
    col = lax.broadcasted_iota(jnp.int32, (CHUNK, CHUNK), 1)
    incl = (col <= row) if forward else (col >= row)
    strict = (col < row) if forward else (col > row)
    return incl, strict, (row == col).astype(F32)


def _gla_kernel(qf, kf, vf, cf, qb, kb, vb, cb, of, ob, st_ref):
    @pl.when(pl.program_id(0) == 0)
    def _():
        st_ref[...] = jnp.zeros_like(st_ref)

    nsub = ROW_TILE // CHUNK
    nh = GLA_HEADS
    hs = range(nh)
    bidx = lax.broadcasted_iota(jnp.int32, (2 * nh, CHUNK, CHUNK), 0)
    row = lax.broadcasted_iota(jnp.int32, (2 * nh, CHUNK, CHUNK), 1)
    col = lax.broadcasted_iota(jnp.int32, (2 * nh, CHUNK, CHUNK), 2)
    incl = jnp.where(bidx < nh, col - row, row - col) <= 0
    fwd1 = lax.broadcasted_iota(jnp.int32, (2 * nh, 1, 1), 0) < nh
    for c in range(nsub):
        slf = slice(c * CHUNK, (c + 1) * CHUNK)
        cb_ = nsub - 1 - c
        slb = slice(cb_ * CHUNK, (cb_ + 1) * CHUNK)
        heads = lambda rf, rb, w: jnp.stack([rf[slf, h * w:(h + 1) * w] for h in hs]
                                            + [rb[slb, h * w:(h + 1) * w] for h in hs])
        q, k, v, cum = (heads(qf, qb, GLA_DK), heads(kf, kb, GLA_DK), heads(vf, vb, GLA_DV),
                        heads(cf, cb, GLA_DK))
        tot = jnp.where(fwd1, cum[:, CHUNK - 1:CHUNK], cum[:, 0:1])
        qd = (q * (GLA_DK ** -0.5) * jnp.exp(cum)).astype(BF16)
        ki = (k * jnp.exp(-cum)).astype(BF16)
        ke = (k * jnp.exp(tot - cum)).astype(BF16)
        vb16 = v.astype(BF16)
        sc = jnp.where(incl, _bdot_nt(qd, ki), 0.0).astype(BF16)
        st = st_ref[...]
        o = (_bdot(sc, vb16) + _bdot_nt(qd, st.astype(BF16))).astype(of.dtype)
        for h in hs:
            of[slf, h * GLA_DV:(h + 1) * GLA_DV] = o[h]
            ob[slb, h * GLA_DV:(h + 1) * GLA_DV] = o[nh + h]
        st_ref[...] = jnp.exp(tot) * st + _bdot_tn(vb16, ke)


def _gla_call(zab, gla_cum, n_lat_tiles):
    nt = n_lat_tiles + 1
    tf = functools.partial(_tile_fwd, n_lat_tiles=n_lat_tiles)
    tb = functools.partial(_tile_bwd, n_lat_tiles=n_lat_tiles)
    qk_w, v_w = GLA_HEADS * GLA_DK, GLA_HEADS * GLA_DV

    def dir_specs(t, d):
        return [pl.BlockSpec((ROW_TILE, qk_w), lambda s: (t(s), 0)),
                pl.BlockSpec((ROW_TILE, qk_w), lambda s: (t(s), 1)),
                pl.BlockSpec((ROW_TILE, v_w), lambda s: (t(s), 1)),
                pl.BlockSpec((ROW_TILE, qk_w), lambda s: (t(s), d))]

    out_sds = jax.ShapeDtypeStruct((nt * ROW_TILE, v_w), BF16)
    return pl.pallas_call(
        _gla_kernel, grid=(nt,),
        in_specs=dir_specs(tf, 0) + dir_specs(tb, 1),
        out_specs=[pl.BlockSpec((ROW_TILE, v_w), lambda s: (tf(s), 0)),
                   pl.BlockSpec((ROW_TILE, v_w), lambda s: (tb(s), 0))],
        out_shape=[out_sds, out_sds],
        scratch_shapes=[pltpu.VMEM((2 * GLA_HEADS, GLA_DV, GLA_DK), F32)],
        compiler_params=_cparams(1), name="gla_scan",
    )(zab, zab, zab, gla_cum, zab, zab, zab, gla_cum)


INV_BASE = 8


def _unit_triangular_inverse_minus_eye(a):
    n = a.shape[-1]
    row = lax.broadcasted_iota(jnp.int32, (n, n), 0)
    col = lax.broadcasted_iota(jnp.int32, (n, n), 1)

    def same_block(b):
        s = int(math.log2(b))
        return lax.shift_right_logical(row, s) == lax.shift_right_logical(col, s)

    bdot = lambda p, r: _bdot(p.astype(BF16), r.astype(BF16))
    q = jnp.where(same_block(INV_BASE), a, 0.0)
    x = bdot(q, q)
    m = 2
    while 2 * m < INV_BASE:
        r = bdot(jnp.concatenate([q, x], axis=1), x)
        q, x = q + x + r[:, :n], r[:, n:]
        m *= 2
    q = q + x + bdot(q, x)
    b = INV_BASE
    while b < n:
        a_b = jnp.where(jnp.logical_and(same_block(2 * b), jnp.logical_not(same_block(b))), a, 0.0)
        y = a_b + bdot(a_b, q)
        q = q + y + bdot(q, y)
        b *= 2
    return q


def _gdn_kernel(xf, gf, tf_ref, xb, gb, tb_ref, of, ob, s_ref):
    @pl.when(pl.program_id(0) == 0)
    def _():
        s_ref[...] = jnp.zeros_like(s_ref)

    nsub = ROW_TILE // CHUNK
    qk_w = GDN_HEADS * GDN_DK
    nh = GDN_HEADS
    hs = range(nh)
    bidx = lax.broadcasted_iota(jnp.int32, (2 * nh, CHUNK, CHUNK), 0)
    row = lax.broadcasted_iota(jnp.int32, (2 * nh, CHUNK, CHUNK), 1)
    col = lax.broadcasted_iota(jnp.int32, (2 * nh, CHUNK, CHUNK), 2)
    ahead = jnp.where(bidx < nh, col - row, row - col)
    incl = ahead <= 0
    strict = ahead < 0
    fwd1 = lax.broadcasted_iota(jnp.int32, (2 * nh, 1, 1), 0) < nh
    for c in range(nsub):
        slf = slice(c * CHUNK, (c + 1) * CHUNK)
        cb = nsub - 1 - c
        slb = slice(cb * CHUNK, (cb + 1) * CHUNK)
        heads = lambda off: jnp.stack(
            [xf[slf, off + h * GDN_DK:off + (h + 1) * GDN_DK] for h in hs]
            + [xb[slb, off + h * GDN_DK:off + (h + 1) * GDN_DK] for h in hs])
        q, k, v = heads(0), heads(qk_w), heads(2 * qk_w)
        lanes = lambda l0: jnp.stack([gf[slf, l0 + h:l0 + h + 1] for h in hs]
                                     + [gb[slb, l0 + nh + h:l0 + nh + h + 1] for h in hs])
        cum, beta = lanes(GDN_A_LANE), lanes(GDN_B_LANE)
        cum_row = jnp.stack(
            [tf_ref[c * LANE + GDN_A_LANE + h:c * LANE + GDN_A_LANE + h + 1, :CHUNK] for h in hs]
            + [tb_ref[cb * LANE + GDN_A_LANE + nh + h:cb * LANE + GDN_A_LANE + nh + h + 1, :CHUNK] for h in hs])
        tot = jnp.where(fwd1, cum[:, CHUNK - 1:CHUNK], cum[:, 0:1])
        decay = jnp.exp(jnp.where(incl, cum - cum_row, -jnp.inf))
        k16 = k.astype(BF16)
        qk_kk = _bdot_nt(jnp.concatenate([q, k], axis=1).astype(BF16), k16)
        a = jnp.where(strict, -(qk_kk[:, CHUNK:] * decay * beta), 0.0)
        t_m1 = _unit_triangular_inverse_minus_eye(a)
        e_cum = jnp.exp(cum)
        rhs = jnp.concatenate([v * beta, k * (beta * e_cum)], axis=2)
        uw = rhs + _bdot(t_m1.astype(BF16), rhs.astype(BF16))
        u, w = uw[:, :, :GDN_DK], uw[:, :, GDN_DK:]
        sc = jnp.where(incl, qk_kk[:, :CHUNK] * decay, 0.0).astype(BF16)
        ke = (k * jnp.exp(tot - cum)).astype(BF16)
        s = s_ref[...]
        ws_qs = _bdot(jnp.concatenate([w, q * e_cum], axis=1).astype(BF16), s.astype(BF16))
        vn16 = (u - ws_qs[:, :CHUNK]).astype(BF16)
        o = (ws_qs[:, CHUNK:] + _bdot(sc, vn16)).astype(of.dtype)
        for h in hs:
            of[slf, h * GDN_DK:(h + 1) * GDN_DK] = o[h]
            ob[slb, h * GDN_DK:(h + 1) * GDN_DK] = o[nh + h]
        s_ref[...] = jnp.exp(tot) * s + _bdot_tn(ke, vn16)


def _gdn_call(qkv, gates, gates_t, n_lat_tiles):
    nt = n_lat_tiles + 1
    tf = functools.partial(_tile_fwd, n_lat_tiles=n_lat_tiles)
    tb = functools.partial(_tile_bwd, n_lat_tiles=n_lat_tiles)
    w = qkv.shape[1]
    out_w = GDN_HEADS * GDN_DK
    tl = (ROW_TILE // CHUNK) * LANE

    def dir_specs(t):
        return [pl.BlockSpec((ROW_TILE, w), lambda s: (t(s), 0)),
                pl.BlockSpec((ROW_TILE, LANE), lambda s: (t(s), 0)),
                pl.BlockSpec((tl, LANE), lambda s: (t(s), 0))]

    out_sds = jax.ShapeDtypeStruct((nt * ROW_TILE, out_w), BF16)
    return pl.pallas_call(
        _gdn_kernel, grid=(nt,),
        in_specs=dir_specs(tf) + dir_specs(tb),
        out_specs=[pl.BlockSpec((ROW_TILE, out_w), lambda s: (tf(s), 0)),
                   pl.BlockSpec((ROW_TILE, out_w), lambda s: (tb(s), 0))],
        out_shape=[out_sds, out_sds],
        scratch_shapes=[pltpu.VMEM((2 * GDN_HEADS, GDN_DK, GDN_DK), F32)],
        compiler_params=_cparams(1), name="gdn_scan",
    )(qkv, gates, gates_t, qkv, gates, gates_t)


def _merge_ab_kernel(gf, gb, df, db, gz, dz, gn, dn, o_ref):
    gla_w = GLA_HEADS * GLA_DV
    og = gf[...].astype(F32) + gb[...].astype(F32)
    zg = gz[...]
    for h in range(GLA_HEADS):
        sl = slice(h * GLA_DV, (h + 1) * GLA_DV)
        seg = og[:, sl]
        y = seg * lax.rsqrt(jnp.mean(seg * seg, axis=-1, keepdims=True) + NORM_EPS) * gn[...]
        o_ref[:, sl] = (y * _silu(zg[:, sl])).astype(BF16)
    od = df[...].astype(F32) + db[...].astype(F32)
    zd = dz[...]
    for h in range(GDN_HEADS):
        sl = slice(h * GDN_DK, (h + 1) * GDN_DK)
        seg = od[:, sl]
        y = seg * lax.rsqrt(jnp.mean(seg * seg, axis=-1, keepdims=True) + NORM_EPS) * dn[...]
        o_ref[:, gla_w + h * GDN_DK:gla_w + (h + 1) * GDN_DK] = (y * _silu(zd[:, sl])).astype(BF16)


def _out0_kernel(gf, gb, df, db, gz, dz, gn, dn, w_ref, x_ref, c_ref, m_ref, m1_ref, g1_ref,
                 o_ref, h1_ref, a_scr, *, n_lat_tiles, d):
    is_ctx = pl.program_id(0) == n_lat_tiles
    _merge_ab_kernel(gf, gb, df, db, gz, dz, gn, dn, a_scr)
    acc = _dot(a_scr[...], w_ref[...])
    m = m_ref[...]
    gate = jnp.where(is_ctx, m[1:2, 2 * d:], m[0:1, 2 * d:])
    xn = jnp.where(is_ctx, c_ref[...], x_ref[...]) + gate * acc
    o_ref[...] = xn
    m1 = m1_ref[...]
    y = xn * lax.rsqrt(jnp.mean(xn * xn, axis=-1, keepdims=True) + NORM_EPS) * g1_ref[...]
    shift = jnp.where(is_ctx, m1[1:2, :d], m1[0:1, :d])
    scale = jnp.where(is_ctx, m1[1:2, d:2 * d], m1[0:1, d:2 * d])
    h1_ref[...] = (y * (1.0 + scale) + shift).astype(BF16)


def _out0_call(gla_f, gla_b, gdn_f, gdn_b, zab, gla_norm, gdn_norm, w_out, x_lat, x_ctx, mods,
               mods_next, g_next, n_lat_tiles):
    m = gla_f.shape[0]
    w = GLA_HEADS * GLA_DV
    k, d = w_out.shape
    row = lambda i: (i, 0)
    const = lambda i: (0, 0)
    return pl.pallas_call(
        functools.partial(_out0_kernel, n_lat_tiles=n_lat_tiles, d=d), grid=(m // ROW_TILE,),
        in_specs=[pl.BlockSpec((ROW_TILE, w), row)] * 4
        + [pl.BlockSpec((ROW_TILE, w), lambda i: (i, 2)), pl.BlockSpec((ROW_TILE, w), lambda i: (i, 6)),
           pl.BlockSpec((1, GLA_DV), const), pl.BlockSpec((1, GDN_DK), const),
           pl.BlockSpec((k, d), const),
           pl.BlockSpec((ROW_TILE, d), lambda i: (jnp.minimum(i, n_lat_tiles - 1), 0)),
           pl.BlockSpec((ROW_TILE, d), const),
           pl.BlockSpec((8, 3 * d), const), pl.BlockSpec((8, 3 * d), const), pl.BlockSpec((1, d), const)],
        out_specs=[pl.BlockSpec((ROW_TILE, d), row), pl.BlockSpec((ROW_TILE, d), row)],
        out_shape=[jax.ShapeDtypeStruct((m, d), F32), jax.ShapeDtypeStruct((m, d), BF16)],
        scratch_shapes=[pltpu.VMEM((ROW_TILE, k), BF16)],
        compiler_params=_cparams(1), name="out_proj_ab",
    )(gla_f, gla_b, gdn_f, gdn_b, zab, zab, gla_norm.reshape(1, -1), gdn_norm.reshape(1, -1),
      w_out, x_lat, x_ctx, mods, mods_next, g_next.reshape(1, d))


def _lru_gates(seg, wa, ba, wx, bx, lam):
    seg16 = seg.astype(BF16)
    sigmoid = lambda t: 0.5 * jnp.tanh(0.5 * t) + 0.5
    r = sigmoid(_dot(seg16, wa) + ba)
    gi = sigmoid(_dot(seg16, wx) + bx)
    log_a = -LRU_C * _softplus(-lam) * r
    a = jnp.exp(log_a)
    return a, jnp.sqrt(jnp.tanh(-log_a) * (a * a + 1.0)) * (gi * seg)


def _lru_ctx_kernel(x_ref, cw_ref, cb_ref, wa_ref, ba_ref, wx_ref, bx_ref, lam_ref, o_ref, a_scr, d_scr):
    x = x_ref[...]
    rows, width = x.shape
    blk = width // LRU_BLOCKS
    cw = cw_ref[...]
    row = lax.broadcasted_iota(jnp.int32, x.shape, 0)
    zero = jnp.zeros((1, width), F32)
    xc = (cw[0:1] * _row_shift(x, -1, row, zero, zero, zero) + cw[1:2] * x
          + cw[2:3] * _row_shift(x, 1, row, zero, zero, zero)
          + cw[3:4] * _row_shift(x, 2, row, zero, zero, zero) + cb_ref[...])
    for d in range(2):
        for b in range(LRU_BLOCKS):
            sl = slice(b * blk, (b + 1) * blk)
            a, drv = _lru_gates(xc[:, sl], wa_ref[d, b], ba_ref[d, :, sl], wx_ref[d, b],
                                bx_ref[d, :, sl], lam_ref[d, :, sl])
            a_scr[d, :, sl] = a
            d_scr[d, :, sl] = drv

    def body(t, carry):
        hf, hb = carry
        hf = a_scr[0, pl.ds(t, 1), :] * hf + d_scr[0, pl.ds(t, 1), :]
        tb = rows - 1 - t
        hb = a_scr[1, pl.ds(tb, 1), :] * hb + d_scr[1, pl.ds(tb, 1), :]
        return hf, hb

    hf, hb = lax.fori_loop(0, rows, body, (zero, zero), unroll=8)
    o_ref[...] = jnp.zeros_like(o_ref)
    o_ref[0:1, :] = hf
    o_ref[1:2, :] = hb


def _lru_lat_kernel(xf, pf, nf, xb, pb, nb, h0_ref, cw_ref, cb_ref, wa_ref, ba_ref, wx_ref, bx_ref,
                    lam_ref, of, ob, a_scr, d_scr, hl_scr, pc_scr, carry, *, n_colblocks):
    ci, s = pl.program_id(0), pl.program_id(1)

    @pl.when(s == 0)
    def _():
        carry[...] = h0_ref[...]

    cw = cw_ref[...]
    dirs = ((xf, pf, nf, of, s), (xb, pb, nb, ob, n_colblocks - 1 - s))
    for d, (x_ref, p_ref, n_ref, o_ref, cblk) in enumerate(dirs):
        forward = d == 0
        x3 = x_ref[...]
        rows, ncol, tc = x3.shape
        has_prev, has_next = cblk > 0, cblk < n_colblocks - 1
        sub = lax.broadcasted_iota(jnp.int32, (ncol, tc), 0)
        prev_last = jnp.where(has_prev, p_ref[7, ncol - 1:ncol, :], 0.0)
        next_r0 = jnp.where(has_next, n_ref[0, 0:1, :], 0.0)
        next_r1 = jnp.where(has_next, n_ref[1, 0:1, :], 0.0)
        first_m1 = jnp.where(sub == 0, prev_last, pltpu.roll(x3[rows - 1], 1, 0))
        last_p1 = jnp.where(sub == ncol - 1, next_r0, pltpu.roll(x3[0], ncol - 1, 0))
        last_p2 = jnp.where(sub == ncol - 1, next_r1, pltpu.roll(x3[1], ncol - 1, 0))
        xm1 = jnp.concatenate([first_m1[None], x3[:rows - 1]], axis=0)
        xp1 = jnp.concatenate([x3[1:], last_p1[None]], axis=0)
        xp2 = jnp.concatenate([x3[2:], last_p1[None], last_p2[None]], axis=0)
        xc = cw[0:1] * xm1 + cw[1:2] * x3 + cw[2:3] * xp1 + cw[3:4] * xp2 + cb_ref[...]
        a, drv = _lru_gates(xc.reshape(rows * ncol, tc), wa_ref[d, ci], ba_ref[d], wx_ref[d, ci],
                            bx_ref[d], lam_ref[d])
        a_scr[d] = a.reshape(rows, ncol, tc)
        d_scr[d] = drv.reshape(rows, ncol, tc)

    def body(i, carry_):
        out = []
        for d, r in ((0, i), (1, rows - 1 - i)):
            h, p = carry_[2 * d], carry_[2 * d + 1]
            ar = a_scr[d, r]
            h = ar * h + d_scr[d, r]
            p = ar * p
            hl_scr[d, r] = h
            pc_scr[d, r] = p
            out += [h, p]
        return tuple(out)

    zero, one = jnp.zeros((ncol, tc), F32), jnp.ones((ncol, tc), F32)
    lax.fori_loop(0, rows, body, (zero, one, zero, one), unroll=8)

    for d, (x_ref, p_ref, n_ref, o_ref, cblk) in enumerate(dirs):
        forward = d == 0
        end = rows - 1 if forward else 0
        hl_end, pc_end = hl_scr[d, end], pc_scr[d, end]
        h = carry[d:d + 1, :]
        h_in = jnp.zeros((ncol, tc), F32)
        for j in (range(ncol) if forward else reversed(range(ncol))):
            h_in = jnp.where(sub == j, h, h_in)
            h = hl_end[j:j + 1, :] + pc_end[j:j + 1, :] * h
        carry[d:d + 1, :] = h
        o_ref[...] = hl_scr[d] + pc_scr[d] * h_in[None]


def _lru_call(zcd, conv_w, conv_b, wa, ba, wx, bx, lam, n_lat):
    m, ncols_z = zcd.shape
    width = conv_w.shape[1]
    n_cols = GRID_W
    rows = n_lat // n_cols
    tc = width // LRU_BLOCKS
    ncb = n_cols // 8
    full = lambda a: pl.BlockSpec(a.shape, lambda *_: (0,) * a.ndim)
    wa16, wx16 = wa.astype(BF16), wx.astype(BF16)
    cb2, ba3, bx3, lam3 = (conv_b.reshape(1, width), ba.reshape(2, 1, width), bx.reshape(2, 1, width),
                           lam.reshape(2, 1, width))

    params = (conv_w, cb2, wa16, ba3, wx16, bx3, lam3)
    h0 = pl.pallas_call(
        _lru_ctx_kernel, grid=(1,),
        in_specs=[pl.BlockSpec((ROW_TILE, width), lambda i: (n_lat // ROW_TILE, 0))] + [full(p) for p in params],
        out_specs=pl.BlockSpec((8, width), lambda i: (0, 0)),
        out_shape=jax.ShapeDtypeStruct((8, width), F32),
        scratch_shapes=[pltpu.VMEM((2, ROW_TILE, width), F32)] * 2,
        compiler_params=_cparams(1), name="rglru_ctx",
    )(zcd, *params)

    z3 = zcd.reshape(m // n_cols, n_cols, ncols_z)
    hb8 = rows // 8
    bf = lambda s: s
    bb = lambda s: ncb - 1 - s

    def dir_specs(blk):
        return [pl.BlockSpec((rows, 8, tc), lambda c, s: (0, blk(s), c)),
                pl.BlockSpec((8, 8, tc), lambda c, s: (hb8 - 1, jnp.maximum(blk(s) - 1, 0), c)),
                pl.BlockSpec((8, 8, tc), lambda c, s: (0, jnp.minimum(blk(s) + 1, ncb - 1), c))]

    vec = lambda lead: pl.BlockSpec((lead, 1, tc), lambda c, s: (0, 0, c))
    out_sds = jax.ShapeDtypeStruct((rows, n_cols, width), F32)
    hf, hb = pl.pallas_call(
        functools.partial(_lru_lat_kernel, n_colblocks=ncb), grid=(width // tc, ncb),
        in_specs=dir_specs(bf) + dir_specs(bb)
        + [pl.BlockSpec((8, tc), lambda c, s: (0, c)),
           pl.BlockSpec((4, tc), lambda c, s: (0, c)), pl.BlockSpec((1, tc), lambda c, s: (0, c)),
           full(wa16), vec(2), full(wx16), vec(2), vec(2)],
        out_specs=[pl.BlockSpec((rows, 8, tc), lambda c, s: (0, bf(s), c)),
                   pl.BlockSpec((rows, 8, tc), lambda c, s: (0, bb(s), c))],
        out_shape=[out_sds, out_sds],
        scratch_shapes=[pltpu.VMEM((2, rows, 8, tc), F32)] * 4 + [pltpu.VMEM((8, tc), F32)],
        compiler_params=_cparams(2), name="rglru_scan",
    )(z3, z3, z3, z3, z3, z3, h0, conv_w, cb2, wa16, ba3, wx16, bx3, lam3)
    return hf.reshape(n_lat, width), hb.reshape(n_lat, width)


def _hy_prep_kernel(*refs, n_lat_tiles):
    xs, (w_ref, b_ref, x0_ref, z_ref) = refs[:9], refs[9:]
    i = pl.program_id(0)
    has_prev = i > 0
    has_next = i < n_lat_tiles - 1
    outs = []
    width = x0_ref.shape[1]
    for j in range(3):
        x_ref, p_ref, n_ref = xs[3 * j:3 * j + 3]
        x = x_ref[...]
        row = lax.broadcasted_iota(jnp.int32, x.shape, 0)
        prev_row = jnp.where(has_prev, p_ref[7:8, :], 0.0)
        next0 = jnp.where(has_next, n_ref[0:1, :], 0.0)
        sl = slice(j * width, (j + 1) * width)
        outs.append(w_ref[0:1, sl] * _row_shift(x, -1, row, prev_row, next0, next0)
                    + w_ref[1:2, sl] * x
                    + w_ref[2:3, sl] * _row_shift(x, 1, row, prev_row, next0, next0)
                    + b_ref[:, sl])
    x0_ref[...] = outs[0]
    z_ref[...] = outs[1] * outs[2]


def _hy_prep_call(zcd, conv_w, conv_b, n_lat_tiles):
    width = conv_w.shape[1] // 3
    hb = ROW_TILE // 8
    last8 = n_lat_tiles * hb - 1
    in_specs, args = [], []
    for j in range(3):
        cbk = 2 + j
        in_specs += [pl.BlockSpec((ROW_TILE, width), lambda i, cbk=cbk: (i, cbk)),
                     pl.BlockSpec((8, width), lambda i, cbk=cbk: (jnp.maximum(i * hb - 1, 0), cbk)),
                     pl.BlockSpec((8, width), lambda i, cbk=cbk: (jnp.minimum((i + 1) * hb, last8), cbk))]
        args += [zcd, zcd, zcd]
    in_specs += [pl.BlockSpec((3, 3 * width), lambda i: (0, 0)), pl.BlockSpec((1, 3 * width), lambda i: (0, 0))]
    out_sds = jax.ShapeDtypeStruct((n_lat_tiles * ROW_TILE, width), F32)
    return pl.pallas_call(
        functools.partial(_hy_prep_kernel, n_lat_tiles=n_lat_tiles), grid=(n_lat_tiles,),
        in_specs=in_specs,
        out_specs=[pl.BlockSpec((ROW_TILE, width), lambda i: (i, 0))] * 2,
        out_shape=[out_sds, out_sds],
        compiler_params=_cparams(1), name="hyena_prep",
    )(*args, conv_w, conv_b.reshape(1, -1))


def _hy_filter_kernel(e_ref, w1, b1, w2, b2, w3, fr, dec, f_ref, ssq_ref):
    i = pl.program_id(0)
    e = e_ref[...]
    hid = jnp.sin(fr[...] * (_dot(e, w1[...], HI) + b1[...]))
    hid = jnp.sin(fr[...] * (_dot(hid, w2[...], HI) + b2[...]))
    filt = _dot(hid.astype(BF16), w3[...]) * jnp.exp(-e[:, 0:1] * dec[...])
    half = filt.shape[1] // 2
    row = lax.broadcasted_iota(jnp.int32, filt.shape, 0)
    col = lax.broadcasted_iota(jnp.int32, filt.shape, 1)
    unused = jnp.logical_and(jnp.logical_and(i == 0, row == 0), col >= half)
    filt = jnp.where(unused, 0.0, filt)
    f_ref[...] = filt

    @pl.when(i == 0)
    def _():
        ssq_ref[...] = jnp.zeros_like(ssq_ref)

    ssq_ref[...] = ssq_ref[...] + jnp.sum(filt * filt, axis=0, keepdims=True)


def _hy_filter_call(emb, w1, b1, w2, b2, w3, freq, decay):
    length = emb.shape[0]
    n_out = w3.shape[1]
    hid = w1.shape[1]
    pad_r = lambda a: jnp.pad(a, ((0, LANE - a.shape[0]), (0, 0)))
    pad_c = lambda a: jnp.pad(a, ((0, 0), (0, LANE - a.shape[1])))
    params = (pad_c(pad_r(w1)), pad_c(b1.reshape(1, hid)), pad_c(pad_r(w2)), pad_c(b2.reshape(1, hid)),
              pad_r(w3).astype(BF16), pad_c(freq.reshape(1, hid)), decay.reshape(1, n_out))
    full = lambda a: pl.BlockSpec(a.shape, lambda i: (0,) * a.ndim)
    return pl.pallas_call(
        _hy_filter_kernel, grid=(length // ROW_TILE,),
        in_specs=[pl.BlockSpec((ROW_TILE, LANE), lambda i: (i, 0))] + [full(p) for p in params],
        out_specs=[pl.BlockSpec((ROW_TILE, n_out), lambda i: (i, 0)),
                   pl.BlockSpec((8, n_out), lambda i: (0, 0))],
        out_shape=[jax.ShapeDtypeStruct((length, n_out), F32), jax.ShapeDtypeStruct((8, n_out), F32)],
        compiler_params=_cparams(1), name="hyena_filter",
    )(emb, *params)


SUB = 16


def _dft_a_kernel(x_ref, c_ref, s_ref, re_ref, im_ref):
    xt = pltpu.einshape("kjc->jkc", x_ref[...])
    res_re, res_im = [], []
    for j in range(SUB):
        xb = xt[j].astype(BF16)
        res_re.append(_dot(c_ref[...], xb))
        res_im.append(_dot(s_ref[...], xb))
    re_ref[...] = pltpu.einshape("jkc->kjc", jnp.stack(res_re)).astype(BF16)
    im_ref[...] = pltpu.einshape("jkc->kjc", jnp.stack(res_im)).astype(BF16)


def _dft_a_call(x, cmat, smat, n1):
    length, ch = x.shape
    n2, n2h = cmat.shape
    tc = 256
    x4 = x.reshape(n2h, n1 // SUB, SUB, ch)
    out_sds = jax.ShapeDtypeStruct((n2, n1 // SUB, SUB, ch), BF16)
    re, im = pl.pallas_call(
        _dft_a_kernel, grid=(ch // tc, n1 // SUB),
        in_specs=[pl.BlockSpec((n2h, None, SUB, tc), lambda c, i: (0, i, 0, c)),
                  pl.BlockSpec((n2, n2h), lambda c, i: (0, 0)),
                  pl.BlockSpec((n2, n2h), lambda c, i: (0, 0))],
        out_specs=[pl.BlockSpec((n2, None, SUB, tc), lambda c, i: (0, i, 0, c))] * 2,
        out_shape=[out_sds, out_sds],
        compiler_params=_cparams(2), name="dft_stage_a",
    )(x4, cmat, smat)
    return re.reshape(n2, n1, ch), im.reshape(n2, n1, ch)


def _dft_b_kernel(are, aim, pre, pim, fre, fim, g_ref, gi_ref, sp_ref, sf_ref, bre_ref, bim_ref,
                  *, n1, inv_n):
    scale = lax.rsqrt(sp_ref[0:1, :] + sf_ref[0:1, :] + NORM_EPS) * inv_n
    res_re, res_im = [], []
    for j in range(SUB):
        g = g_ref[j]
        cat = lambda r, i: jnp.concatenate([r[j], i[j]], axis=0)
        xp, xf, x = _dot(g, cat(pre, pim)), _dot(g, cat(fre, fim)), _dot(g, cat(are, aim))
        hr = (xp[:n1] + xf[:n1]) * scale
        hi = (xp[n1:] - xf[n1:]) * scale
        xr, xi = x[:n1], x[n1:]
        y = jnp.concatenate([xr * hr - xi * hi, xr * hi + xi * hr], axis=0).astype(BF16)
        b = _dot(gi_ref[j], y)
        res_re.append(b[:n1])
        res_im.append(b[n1:])
    bre_ref[...] = pltpu.einshape("jkc->kjc", jnp.stack(res_re)).astype(BF16)
    bim_ref[...] = pltpu.einshape("jkc->kjc", jnp.stack(res_im)).astype(BF16)


def _dft_b_call(a_re, a_im, f_re, f_im, gmat, gimat, ssq):
    n2, n1, ch = a_re.shape
    tc = 256
    nct = ch // tc
    slab = lambda half: pl.BlockSpec((SUB, n1, tc), lambda k, c: (k, 0, half * nct + c))
    gspec = pl.BlockSpec((SUB, 2 * n1, 2 * n1), lambda k, c: (k, 0, 0))
    out_sds = jax.ShapeDtypeStruct((n1, n2 // SUB, SUB, ch), BF16)
    re, im = pl.pallas_call(
        functools.partial(_dft_b_kernel, n1=n1, inv_n=1.0 / (n1 * n2)), grid=(n2 // SUB, nct),
        in_specs=[slab(0), slab(0), slab(0), slab(0), slab(1), slab(1), gspec, gspec,
                  pl.BlockSpec((8, tc), lambda k, c: (0, c)),
                  pl.BlockSpec((8, tc), lambda k, c: (0, nct + c))],
        out_specs=[pl.BlockSpec((n1, None, SUB, tc), lambda k, c: (0, k, 0, c))] * 2,
        out_shape=[out_sds, out_sds],
        compiler_params=_cparams(2), name="dft_stage_b",
    )(a_re, a_im, f_re, f_im, f_re, f_im, gmat, gimat, ssq, ssq)
    return re.reshape(n1, n2, ch), im.reshape(n1, n2, ch)


def _dft_c_kernel(bre, bim, x0_ref, z_ref, hz_ref, bias_ref, c_ref, s_ref, o_ref):
    ys = [_dot(c_ref[...], bre[j].astype(BF16)) + _dot(s_ref[...], bim[j].astype(BF16)) for j in range(SUB)]
    y = pltpu.einshape("jkc->kjc", jnp.stack(ys))
    o_ref[...] = x0_ref[...] * (y + z_ref[...] * bias_ref[...]) * _silu(hz_ref[...])


def _dft_c_call(b_re, b_im, x0, z, zcd, bias, cmat, smat):
    n1, n2, ch = b_re.shape
    n2h = n2
    length = n1 * n2h
    tc = 256
    nct = ch // tc
    view = lambda a: a.reshape(a.shape[0] // n1, n1 // SUB, SUB, a.shape[1])
    tspec = lambda cblk: pl.BlockSpec((n2h, None, SUB, tc), lambda i, c: (0, i, 0, cblk * nct + c))
    out = pl.pallas_call(
        _dft_c_kernel, grid=(n1 // SUB, nct),
        in_specs=[pl.BlockSpec((SUB, n2, tc), lambda i, c: (i, 0, c))] * 2
        + [tspec(0), tspec(0), tspec(5), pl.BlockSpec((1, tc), lambda i, c: (0, c)),
           pl.BlockSpec((n2h, n2), lambda i, c: (0, 0)), pl.BlockSpec((n2h, n2), lambda i, c: (0, 0))],
        out_specs=tspec(0),
        out_shape=jax.ShapeDtypeStruct((n2h, n1 // SUB, SUB, ch), F32),
        compiler_params=_cparams(2), name="dft_stage_c",
    )(b_re, b_im, view(x0), view(z), view(zcd), bias.reshape(1, ch), cmat, smat)
    return out.reshape(length, ch)


def _dft_constants(n1, n2):
    n = n1 * n2
    n2h = n2 // 2
    k2 = jnp.arange(n2h, dtype=jnp.int32)
    ang_a = (math.pi / n2) * ((k2[None, :] * (2 * k2[:, None] + 1)) % (2 * n2)).astype(F32)
    ca, sa = jnp.cos(ang_a), -jnp.sin(ang_a)
    cc, sc = jnp.cos(ang_a).T, -jnp.sin(ang_a).T
    i1 = jnp.arange(n1, dtype=jnp.int32)
    ang_f = (2.0 * math.pi / n1) * ((i1[:, None] * i1[None, :]) % n1).astype(F32)
    ang_t = (math.pi / n) * ((2 * k2[:, None] + 1) * i1[None, :]).astype(F32)
    fr, fi = jnp.cos(ang_f)[None], -jnp.sin(ang_f)[None]
    wr, wi = jnp.cos(ang_t)[:, None, :], -jnp.sin(ang_t)[:, None, :]
    gr, gi = fr * wr - fi * wi, fr * wi + fi * wr
    g = jnp.concatenate([jnp.concatenate([gr, -gi], axis=2), jnp.concatenate([gi, gr], axis=2)], axis=1)
    tr, ti = jnp.swapaxes(gr, 1, 2), -jnp.swapaxes(gi, 1, 2)
    ginv = jnp.concatenate([jnp.concatenate([tr, -ti], axis=2), jnp.concatenate([ti, tr], axis=2)], axis=1)
    b = lambda a: a.astype(BF16)
    return b(ca), b(sa), b(cc), b(sc), b(g), b(ginv)


def _final_kernel(hf, hb, lz, hy, w1, w2, x_ref, m_ref, g_ref, o_ref, *, d):
    a1 = ((hf[...] + hb[...]) * _silu(lz[...])).astype(BF16)
    acc = _dot(a1, w1[...]) + _dot(hy[...].astype(BF16), w2[...])
    xn = x_ref[...] + m_ref[0:1, 2 * d:] * acc
    o_ref[...] = xn * lax.rsqrt(jnp.mean(xn * xn, axis=-1, keepdims=True) + NORM_EPS) * g_ref[...]


def _final_call(hf, hb, zcd, hy, w_out, x1, mods, final_g, n_lat_tiles):
    d = w_out.shape[1]
    half = w_out.shape[0] // 2
    row = lambda i: (i, 0)
    const = lambda i: (0, 0)
    tm = _row_tile(n_lat_tiles, 2)
    once = dict(pipeline_mode=pl.Buffered(1))
    return pl.pallas_call(
        functools.partial(_final_kernel, d=d), grid=(n_lat_tiles * ROW_TILE // tm,),
        in_specs=[pl.BlockSpec((tm, half), row), pl.BlockSpec((tm, half), row),
                  pl.BlockSpec((tm, half), lambda i: (i, 1)), pl.BlockSpec((tm, half), row),
                  pl.BlockSpec((half, d), const, **once), pl.BlockSpec((half, d), lambda i: (1, 0), **once),
                  pl.BlockSpec((tm, d), row), pl.BlockSpec((8, 3 * d), const),
                  pl.BlockSpec((1, d), const)],
        out_specs=pl.BlockSpec((tm, d), row),
        out_shape=jax.ShapeDtypeStruct((n_lat_tiles * ROW_TILE, d), F32),
        compiler_params=_cparams(1), name="out_proj_cd_final",
    )(hf, hb, zcd, hy, w_out, w_out, x1, mods, final_g.reshape(1, d))


def kernel(x, c, ctx, c_ctx, mod_w, mod_b, norm_g, ab_w_in, ab_gla_wg2, ab_gla_bg2, ab_gla_norm, ab_gdn_conv, ab_gdn_a_log, ab_gdn_dt_bias, ab_gdn_norm, ab_w_out, cd_w_in, cd_lru_conv_w, cd_lru_conv_b, cd_lru_wa, cd_lru_ba, cd_lru_wx, cd_lru_bx, cd_lru_lambda, cd_hy_conv_w, cd_hy_conv_b, cd_hy_w1, cd_hy_b1, cd_hy_w2, cd_hy_b2, cd_hy_w3, cd_hy_freq, cd_hy_decay, cd_hy_bias, cd_w_out, final_g):
    _, n_lat, d = x.shape
    assert x.shape[0] == 1 and ctx.shape[1] == ROW_TILE and n_lat % ROW_TILE == 0
    assert mod_w.shape[0] == 2 and d == 2 * GLA_HEADS * GLA_DV
    nlt = n_lat // ROW_TILE
    x_lat, x_ctx = x[0], ctx[0]

    cvec = jnp.zeros((8, d), F32).at[0].set(c[0]).at[1].set(c_ctx)
    mods = _mod_call(cvec, mod_w, mod_b)

    gla_qk, gla_v = GLA_HEADS * GLA_DK, GLA_HEADS * GLA_DV
    gdn_qk = GDN_HEADS * GDN_DK
    sizes = (gla_qk, gla_qk, gla_v, 2 * GLA_RANK, gla_v, 3 * gdn_qk, 2 * GDN_HEADS, 2 * GDN_HEADS, gdn_qk)
    gq, gk, gv, glr, gz, dqkv, da, db, dz = jnp.split(ab_w_in[0], np.cumsum(sizes)[:-1].tolist(), axis=1)
    w_main = jnp.concatenate([gq, gk, gv, gz, dqkv, dz], axis=1).astype(BF16)
    n_small = glr.shape[1] + da.shape[1] + db.shape[1]
    w_small = jnp.concatenate([glr, da, db, jnp.zeros((d, LANE - n_small), F32)], axis=1).astype(BF16)

    h0 = _norm_mod_call(x_lat, x_ctx, mods[0], norm_g[0], nlt)
    zab = _mm_call(h0, w_main, 1024, "in_proj_ab")
    zsm = _mm_call(h0, w_small, LANE, "in_proj_ab_small")

    gw = GLA_HEADS * GLA_DK
    wg = jnp.zeros((LANE, 2 * gw), F32)
    for dd in range(2):
        wg = wg.at[dd * GLA_RANK:(dd + 1) * GLA_RANK, dd * gw:(dd + 1) * gw].set(ab_gla_wg2[0, dd])
    gbias = ab_gla_bg2[0].reshape(1, 2 * gw)
    gparams = jnp.zeros((8, LANE), F32)
    gparams = gparams.at[0, GDN_A_LANE:GDN_B_LANE].set(ab_gdn_a_log[0].reshape(-1))
    gparams = gparams.at[1, GDN_A_LANE:GDN_B_LANE].set(ab_gdn_dt_bias[0].reshape(-1))
    qkv, gla_cum, gates, gates_t = _ab_prep_call(zab, zsm, ab_gdn_conv[0], wg.astype(BF16), gbias, gparams, nlt)
    gla_f, gla_b = _gla_call(zab, gla_cum, nlt)
    gdn_f, gdn_b = _gdn_call(qkv, gates, gates_t, nlt)

    x1, h1 = _out0_call(gla_f, gla_b, gdn_f, gdn_b, zab, ab_gla_norm[0], ab_gdn_norm[0],
                        ab_w_out[0].astype(BF16), x_lat, x_ctx, mods[0], mods[1], norm_g[1], nlt)

    zcd = _mm_call(h1, cd_w_in[0].astype(BF16), 1024, "in_proj_cd")

    lru_f, lru_b = _lru_call(zcd, cd_lru_conv_w[0], cd_lru_conv_b[0], cd_lru_wa[0], cd_lru_ba[0],
                             cd_lru_wx[0], cd_lru_bx[0], cd_lru_lambda[0], n_lat)

    x0, z = _hy_prep_call(zcd, cd_hy_conv_w[0], cd_hy_conv_b[0], nlt)

    pos = jnp.arange(n_lat, dtype=F32)
    t01 = pos / max(n_lat - 1, 1)
    bands = jnp.linspace(1e-4, HY_BANDS - 1, HY_BANDS, dtype=F32)
    ang = (2.0 * math.pi / n_lat) * pos[:, None] * bands[None, :]
    emb = jnp.concatenate([t01[:, None], jnp.cos(ang), jnp.sin(ang)], axis=-1)
    emb = jnp.pad(emb, ((0, 0), (0, LANE - emb.shape[1])))
    filt, ssq = _hy_filter_call(emb, cd_hy_w1[0], cd_hy_b1[0], cd_hy_w2[0], cd_hy_b2[0], cd_hy_w3[0],
                                cd_hy_freq[0], cd_hy_decay[0])

    n1 = LANE if n_lat >= LANE * LANE else 32
    n2 = 2 * n_lat // n1
    ca, sa, cc, sc, gmat, gimat = _dft_constants(n1, n2)
    fa_re, fa_im = _dft_a_call(filt, ca, sa, n1)
    za_re, za_im = _dft_a_call(z, ca, sa, n1)
    b_re, b_im = _dft_b_call(za_re, za_im, fa_re, fa_im, gmat, gimat, ssq)
    y_hy = _dft_c_call(b_re, b_im, x0, z, zcd, cd_hy_bias[0], cc, sc)

    out = _final_call(lru_f, lru_b, zcd, y_hy, cd_w_out[0].astype(BF16), x1, mods[1], final_g, nlt)
    return out[None]
```

```python
import functools
import math

import numpy as np
import jax
import jax.numpy as jnp
from jax import lax
from jax.experimental import pallas as pl
from jax.experimental.pallas import tpu as pltpu

F32 = jnp.float32
BF16 = jnp.bfloat16
HI = lax.Precision.HIGHEST

NORM_EPS = 1e-6
CHUNK = 64
ROW_TILE = 256
GRID_W = 64
LANE = 128
GLA_HEADS, GLA_DK, GLA_DV, GLA_RANK, GLA_GATE_NORM = 4, 128, 256, 16, 16.0
GDN_HEADS, GDN_DK = 8, 128
LRU_BLOCKS, LRU_C = 8, 8.0
HY_BANDS = 16
VMEM_LIMIT = 56 * 1024 * 1024


def _cparams(n_axes):
    return pltpu.CompilerParams(dimension_semantics=("arbitrary",) * n_axes,
                                vmem_limit_bytes=VMEM_LIMIT)


def _silu(x):
    return x * jax.nn.sigmoid(x)


def _softplus(x):
    return jnp.maximum(x, 0.0) + jnp.log1p(jnp.exp(-jnp.abs(x)))


def _dot(a, b, precision=None):
    return jnp.dot(a, b, precision=precision, preferred_element_type=F32)


def _mask_dot_f32(mask16, x):
    hi = x.astype(BF16)
    r = x - hi.astype(F32)
    mid = r.astype(BF16)
    lo = (r - mid.astype(F32)).astype(BF16)
    return _dot(mask16, hi) + _dot(mask16, mid) + _dot(mask16, lo)


def _bdot(a, b):
    return lax.dot_general(a, b, (((2,), (1,)), ((0,), (0,))), preferred_element_type=F32)


def _bdot_nt(a, b):
    return lax.dot_general(a, b, (((2,), (2,)), ((0,), (0,))), preferred_element_type=F32)


def _bdot_tn(a, b):
    return lax.dot_general(a, b, (((1,), (1,)), ((0,), (0,))), preferred_element_type=F32)


def _mod_kernel(c_ref, w_ref, b_ref, o_ref):
    o_ref[...] = _dot(_silu(c_ref[...]), w_ref[...], HI) + b_ref[...]


def _mod_call(cvec, mod_w, mod_b):
    depth, d, n = mod_w.shape
    tn = 512
    return pl.pallas_call(
        _mod_kernel, grid=(depth, n // tn),
        in_specs=[pl.BlockSpec((8, d), lambda l, j: (0, 0)),
                  pl.BlockSpec((None, d, tn), lambda l, j: (l, 0, j)),
                  pl.BlockSpec((None, 1, tn), lambda l, j: (l, 0, j))],
        out_specs=pl.BlockSpec((None, 8, tn), lambda l, j: (l, 0, j)),
        out_shape=jax.ShapeDtypeStruct((depth, 8, n), F32),
        compiler_params=_cparams(2), name="adaln_mod",
    )(cvec, mod_w, mod_b.reshape(depth, 1, n))


def _norm_mod_kernel(x_ref, c_ref, m_ref, g_ref, o_ref, *, n_lat_tiles, d):
    is_ctx = pl.program_id(0) == n_lat_tiles
    xt = jnp.where(is_ctx, c_ref[...], x_ref[...])
    y = xt * lax.rsqrt(jnp.mean(xt * xt, axis=-1, keepdims=True) + NORM_EPS) * g_ref[...]
    m = m_ref[...]
    shift = jnp.where(is_ctx, m[1:2, :d], m[0:1, :d])
    scale = jnp.where(is_ctx, m[1:2, d:2 * d], m[0:1, d:2 * d])
    o_ref[...] = (y * (1.0 + scale) + shift).astype(BF16)


def _norm_mod_call(x_lat, x_ctx, mods, g, n_lat_tiles):
    d = x_lat.shape[1]
    nt = n_lat_tiles + 1
    const = lambda i: (0, 0)
    return pl.pallas_call(
        functools.partial(_norm_mod_kernel, n_lat_tiles=n_lat_tiles, d=d), grid=(nt,),
        in_specs=[pl.BlockSpec((ROW_TILE, d), lambda i: (jnp.minimum(i, n_lat_tiles - 1), 0)),
                  pl.BlockSpec((ROW_TILE, d), const),
                  pl.BlockSpec((8, 3 * d), const), pl.BlockSpec((1, d), const)],
        out_specs=pl.BlockSpec((ROW_TILE, d), lambda i: (i, 0)),
        out_shape=jax.ShapeDtypeStruct((nt * ROW_TILE, d), BF16),
        compiler_params=_cparams(1), name="norm_mod",
    )(x_lat, x_ctx, mods, g.reshape(1, d))


def _mm_kernel(a_ref, w_ref, o_ref):
    o_ref[...] = _dot(a_ref[...], w_ref[...])


def _row_tile(n_tiles, max_mult):
    k = max(m for m in range(1, max_mult + 1) if n_tiles % m == 0)
    return k * ROW_TILE


def _mm_call(a, w, tn, name):
    m, k = a.shape
    n = w.shape[1]
    tm = _row_tile(m // ROW_TILE, 5)
    return pl.pallas_call(
        _mm_kernel, grid=(n // tn, m // tm),
        in_specs=[pl.BlockSpec((tm, k), lambda j, i: (i, 0)),
                  pl.BlockSpec((k, tn), lambda j, i: (0, j))],
        out_specs=pl.BlockSpec((tm, tn), lambda j, i: (i, j)),
        out_shape=jax.ShapeDtypeStruct((m, n), F32),
        compiler_params=_cparams(2), name=name,
    )(a, w)


def _row_shift(x, k, row, prev_row, next0, next1):
    n = x.shape[0]
    if k == -1:
        return jnp.where(row == 0, prev_row, pltpu.roll(x, 1, 0))
    if k == 1:
        return jnp.where(row == n - 1, next0, pltpu.roll(x, n - 1, 0))
    r = pltpu.roll(x, n - 2, 0)
    return jnp.where(row == n - 2, next0, jnp.where(row == n - 1, next1, r))


GDN_A_LANE = 2 * GLA_RANK
GDN_B_LANE = GDN_A_LANE + 2 * GDN_HEADS


def _ab_prep_kernel(x_ref, p_ref, n_ref, w_ref, z_ref, wg_ref, gb_ref, gp_ref,
                    o_ref, cum_ref, gates_ref, gt_ref, *, n_lat_tiles):
    zs = z_ref[...]
    logit = _dot(zs.astype(BF16), wg_ref[...]) + gb_ref[...]
    g = -_softplus(-logit) / GLA_GATE_NORM
    log_a = -jnp.exp(gp_ref[0:1, :]) * _softplus(zs + gp_ref[1:2, :])
    beta = jax.nn.sigmoid(zs)
    lower, _, _ = _tri_masks(True)
    upper, _, _ = _tri_masks(False)
    lower, upper = lower.astype(BF16), upper.astype(BF16)
    half = cum_ref.shape[1] // 2
    lane = lax.broadcasted_iota(jnp.int32, (CHUNK, LANE), 1)
    lane_t = lax.broadcasted_iota(jnp.int32, (LANE, CHUNK), 0)
    bwd_lo, bwd_hi = GDN_A_LANE + GDN_HEADS, GDN_B_LANE
    for c in range(ROW_TILE // CHUNK):
        sl = slice(c * CHUNK, (c + 1) * CHUNK)
        cum_ref[sl, :half] = _mask_dot_f32(lower, g[sl, :half])
        cum_ref[sl, half:] = _mask_dot_f32(upper, g[sl, half:])
        cum_f = _mask_dot_f32(lower, log_a[sl])
        cum_b = _mask_dot_f32(upper, log_a[sl])
        is_b = jnp.logical_and(lane >= bwd_lo, lane < bwd_hi)
        gates_ref[sl, :] = jnp.where(lane >= GDN_B_LANE, beta[sl], jnp.where(is_b, cum_b, cum_f))
        is_bt = jnp.logical_and(lane_t >= bwd_lo, lane_t < bwd_hi)
        gt_ref[c * LANE:(c + 1) * LANE, :CHUNK] = jnp.where(is_bt, cum_b.T, cum_f.T)
        gt_ref[c * LANE:(c + 1) * LANE, CHUNK:] = jnp.zeros((LANE, LANE - CHUNK), F32)

    i = pl.program_id(0)
    has_prev = jnp.logical_and(i != 0, i != n_lat_tiles)
    has_next = i < n_lat_tiles - 1
    x = x_ref[...]
    row = lax.broadcasted_iota(jnp.int32, x.shape, 0)
    prev_row = jnp.where(has_prev, p_ref[7:8, :], 0.0)
    next0 = jnp.where(has_next, n_ref[0:1, :], 0.0)
    next1 = jnp.where(has_next, n_ref[1:2, :], 0.0)
    w = w_ref[...]
    y = (w[0:1] * _row_shift(x, -1, row, prev_row, next0, next1) + w[1:2] * x
         + w[2:3] * _row_shift(x, 1, row, prev_row, next0, next1)
         + w[3:4] * _row_shift(x, 2, row, prev_row, next0, next1))
    y = _silu(y)
    qk = GDN_HEADS * GDN_DK
    for h in range(2 * GDN_HEADS):
        seg = y[:, h * GDN_DK:(h + 1) * GDN_DK]
        seg = seg * lax.rsqrt(jnp.sum(seg * seg, axis=-1, keepdims=True) + NORM_EPS)
        if h < GDN_HEADS:
            seg = seg * (GDN_DK ** -0.5)
        o_ref[:, h * GDN_DK:(h + 1) * GDN_DK] = seg
    o_ref[:, 2 * qk:] = y[:, 2 * qk:]


def _ab_prep_call(zab, zsm, conv_w, wg, gbias, gparams, n_lat_tiles):
    nt = n_lat_tiles + 1
    w = 3 * GDN_HEADS * GDN_DK
    gw = wg.shape[1]
    hb = ROW_TILE // 8
    last8 = nt * hb - 1
    nsub = ROW_TILE // CHUNK
    rows = nt * ROW_TILE
    const = lambda i: (0, 0)
    return pl.pallas_call(
        functools.partial(_ab_prep_kernel, n_lat_tiles=n_lat_tiles), grid=(nt,),
        in_specs=[pl.BlockSpec((ROW_TILE, w), lambda i: (i, 1)),
                  pl.BlockSpec((8, w), lambda i: (jnp.maximum(i * hb - 1, 0), 1)),
                  pl.BlockSpec((8, w), lambda i: (jnp.minimum((i + 1) * hb, last8), 1)),
                  pl.BlockSpec((4, w), const),
                  pl.BlockSpec((ROW_TILE, LANE), lambda i: (i, 0)),
                  pl.BlockSpec((LANE, gw), const), pl.BlockSpec((1, gw), const),
                  pl.BlockSpec((8, LANE), const)],
        out_specs=[pl.BlockSpec((ROW_TILE, w), lambda i: (i, 0)),
                   pl.BlockSpec((ROW_TILE, gw), lambda i: (i, 0)),
                   pl.BlockSpec((ROW_TILE, LANE), lambda i: (i, 0)),
                   pl.BlockSpec((nsub * LANE, LANE), lambda i: (i, 0))],
        out_shape=[jax.ShapeDtypeStruct((rows, w), F32), jax.ShapeDtypeStruct((rows, gw), F32),
                   jax.ShapeDtypeStruct((rows, LANE), F32),
                   jax.ShapeDtypeStruct((nt * nsub * LANE, LANE), F32)],
        compiler_params=_cparams(1), name="ab_prep",
    )(zab, zab, zab, conv_w, zsm, wg, gbias, gparams)


def _tile_fwd(s, n_lat_tiles):
    return jnp.where(s == 0, n_lat_tiles, s - 1)


def _tile_bwd(s, n_lat_tiles):
    return jnp.where(s == 0, n_lat_tiles, n_lat_tiles - s)


def _tri_masks(forward):
    row = lax.broadcasted_iota(jnp.int32, (CHUNK, CHUNK), 0)
    col = lax.broadcasted_iota(jnp.int32, (CHUNK, CHUNK), 1)
    incl = (col <= row) if forward else (col >= row)
    strict = (col < row) if forward else (col > row)
    return incl, strict, (row == col).astype(F32)


def _gla_kernel(qf, kf, vf, cf, qb, kb, vb, cb, of, ob, st_ref):
    @pl.when(pl.program_id(0) == 0)
    def _():
        st_ref[...] = jnp.zeros_like(st_ref)

    nsub = ROW_TILE // CHUNK
    nh = GLA_HEADS
    hs = range(nh)
    bidx = lax.broadcasted_iota(jnp.int32, (2 * nh, CHUNK, CHUNK), 0)
    row = lax.broadcasted_iota(jnp.int32, (2 * nh, CHUNK, CHUNK), 1)
    col = lax.broadcasted_iota(jnp.int32, (2 * nh, CHUNK, CHUNK), 2)
    incl = jnp.where(bidx < nh, col - row, row - col) <= 0
    fwd1 = lax.broadcasted_iota(jnp.int32, (2 * nh, 1, 1), 0) < nh
    for c in range(nsub):
        slf = slice(c * CHUNK, (c + 1) * CHUNK)
        cb_ = nsub - 1 - c
        slb = slice(cb_ * CHUNK, (cb_ + 1) * CHUNK)
        heads = lambda rf, rb, w: jnp.stack([rf[slf, h * w:(h + 1) * w] for h in hs]
                                            + [rb[slb, h * w:(h + 1) * w] for h in hs])
        q, k, v, cum = (heads(qf, qb, GLA_DK), heads(kf, kb, GLA_DK), heads(vf, vb, GLA_DV),
                        heads(cf, cb, GLA_DK))
        tot = jnp.where(fwd1, cum[:, CHUNK - 1:CHUNK], cum[:, 0:1])
        qd = (q * (GLA_DK ** -0.5) * jnp.exp(cum)).astype(BF16)
        ki = (k * jnp.exp(-cum)).astype(BF16)
        ke = (k * jnp.exp(tot - cum)).astype(BF16)
        vb16 = v.astype(BF16)
        sc = jnp.where(incl, _bdot_nt(qd, ki), 0.0).astype(BF16)
        st = st_ref[...]
        o = (_bdot(sc, vb16) + _bdot_nt(qd, st.astype(BF16))).astype(of.dtype)
        for h in hs:
            of[slf, h * GLA_DV:(h + 1) * GLA_DV] = o[h]
            ob[slb, h * GLA_DV:(h + 1) * GLA_DV] = o[nh + h]
        st_ref[...] = jnp.exp(tot) * st + _bdot_tn(vb16, ke)


def _gla_call(zab, gla_cum, n_lat_tiles):
    nt = n_lat_tiles + 1
    tf = functools.partial(_tile_fwd, n_lat_tiles=n_lat_tiles)
    tb = functools.partial(_tile_bwd, n_lat_tiles=n_lat_tiles)
    qk_w, v_w = GLA_HEADS * GLA_DK, GLA_HEADS * GLA_DV

    def dir_specs(t, d):
        return [pl.BlockSpec((ROW_TILE, qk_w), lambda s: (t(s), 0)),
                pl.BlockSpec((ROW_TILE, qk_w), lambda s: (t(s), 1)),
                pl.BlockSpec((ROW_TILE, v_w), lambda s: (t(s), 1)),
                pl.BlockSpec((ROW_TILE, qk_w), lambda s: (t(s), d))]

    out_sds = jax.ShapeDtypeStruct((nt * ROW_TILE, v_w), BF16)
    return pl.pallas_call(
        _gla_kernel, grid=(nt,),
        in_specs=dir_specs(tf, 0) + dir_specs(tb, 1),
        out_specs=[pl.BlockSpec((ROW_TILE, v_w), lambda s: (tf(s), 0)),
                   pl.BlockSpec((ROW_TILE, v_w), lambda s: (tb(s), 0))],
        out_shape=[out_sds, out_sds],
        scratch_shapes=[pltpu.VMEM((2 * GLA_HEADS, GLA_DV, GLA_DK), F32)],
        compiler_params=_cparams(1), name="gla_scan",
    )(zab, zab, zab, gla_cum, zab, zab, zab, gla_cum)


INV_BASE = 8


def _unit_triangular_inverse_minus_eye(a):
    n = a.shape[-1]
    row = lax.broadcasted_iota(jnp.int32, (n, n), 0)
    col = lax.broadcasted_iota(jnp.int32, (n, n), 1)

    def same_block(b):
        s = int(math.log2(b))
        return lax.shift_right_logical(row, s) == lax.shift_right_logical(col, s)

    bdot = lambda p, r: _bdot(p.astype(BF16), r.astype(BF16))
    q = jnp.where(same_block(INV_BASE), a, 0.0)
    x = bdot(q, q)
    m = 2
    while 2 * m < INV_BASE:
        r = bdot(jnp.concatenate([q, x], axis=1), x)
        q, x = q + x + r[:, :n], r[:, n:]
        m *= 2
    q = q + x + bdot(q, x)
    b = INV_BASE
    while b < n:
        a_b = jnp.where(jnp.logical_and(same_block(2 * b), jnp.logical_not(same_block(b))), a, 0.0)
        y = a_b + bdot(a_b, q)
        q = q + y + bdot(q, y)
        b *= 2
    return q


def _gdn_kernel(xf, gf, tf_ref, xb, gb, tb_ref, of, ob, s_ref):
    @pl.when(pl.program_id(0) == 0)
    def _():
        s_ref[...] = jnp.zeros_like(s_ref)

    nsub = ROW_TILE // CHUNK
    qk_w = GDN_HEADS * GDN_DK
    nh = GDN_HEADS
    hs = range(nh)
    bidx = lax.broadcasted_iota(jnp.int32, (2 * nh, CHUNK, CHUNK), 0)
    row = lax.broadcasted_iota(jnp.int32, (2 * nh, CHUNK, CHUNK), 1)
    col = lax.broadcasted_iota(jnp.int32, (2 * nh, CHUNK, CHUNK), 2)
    ahead = jnp.where(bidx < nh, col - row, row - col)
    incl = ahead <= 0
    strict = ahead < 0
    fwd1 = lax.broadcasted_iota(jnp.int32, (2 * nh, 1, 1), 0) < nh
    for c in range(nsub):
        slf = slice(c * CHUNK, (c + 1) * CHUNK)
        cb = nsub - 1 - c
        slb = slice(cb * CHUNK, (cb + 1) * CHUNK)
        heads = lambda off: jnp.stack(
            [xf[slf, off + h * GDN_DK:off + (h + 1) * GDN_DK] for h in hs]
            + [xb[slb, off + h * GDN_DK:off + (h + 1) * GDN_DK] for h in hs])
        q, k, v = heads(0), heads(qk_w), heads(2 * qk_w)
        lanes = lambda l0: jnp.stack([gf[slf, l0 + h:l0 + h + 1] for h in hs]
                                     + [gb[slb, l0 + nh + h:l0 + nh + h + 1] for h in hs])
        cum, beta = lanes(GDN_A_LANE), lanes(GDN_B_LANE)
        cum_row = jnp.stack(
            [tf_ref[c * LANE + GDN_A_LANE + h:c * LANE + GDN_A_LANE + h + 1, :CHUNK] for h in hs]
            + [tb_ref[cb * LANE + GDN_A_LANE + nh + h:cb * LANE + GDN_A_LANE + nh + h + 1, :CHUNK] for h in hs])
        tot = jnp.where(fwd1, cum[:, CHUNK - 1:CHUNK], cum[:, 0:1])
        decay = jnp.exp(jnp.where(incl, cum - cum_row, -jnp.inf))
        k16 = k.astype(BF16)
        qk_kk = _bdot_nt(jnp.concatenate([q, k], axis=1).astype(BF16), k16)
        a = jnp.where(strict, -(qk_kk[:, CHUNK:] * decay * beta), 0.0)
        t_m1 = _unit_triangular_inverse_minus_eye(a)
        e_cum = jnp.exp(cum)
        rhs = jnp.concatenate([v * beta, k * (beta * e_cum)], axis=2)
        uw = rhs + _bdot(t_m1.astype(BF16), rhs.astype(BF16))
        u, w = uw[:, :, :GDN_DK], uw[:, :, GDN_DK:]
        sc = jnp.where(incl, qk_kk[:, :CHUNK] * decay, 0.0).astype(BF16)
        ke = (k * jnp.exp(tot - cum)).astype(BF16)
        s = s_ref[...]
        ws_qs = _bdot(jnp.concatenate([w, q * e_cum], axis=1).astype(BF16), s.astype(BF16))
        vn16 = (u - ws_qs[:, :CHUNK]).astype(BF16)
        o = (ws_qs[:, CHUNK:] + _bdot(sc, vn16)).astype(of.dtype)
        for h in hs:
            of[slf, h * GDN_DK:(h + 1) * GDN_DK] = o[h]
            ob[slb, h * GDN_DK:(h + 1) * GDN_DK] = o[nh + h]
        s_ref[...] = jnp.exp(tot) * s + _bdot_tn(ke, vn16)


def _gdn_call(qkv, gates, gates_t, n_lat_tiles):
    nt = n_lat_tiles + 1
    tf = functools.partial(_tile_fwd, n_lat_tiles=n_lat_tiles)
    tb = functools.partial(_tile_bwd, n_lat_tiles=n_lat_tiles)
    w = qkv.shape[1]
    out_w = GDN_HEADS * GDN_DK
    tl = (ROW_TILE // CHUNK) * LANE

    def dir_specs(t):
        return [pl.BlockSpec((ROW_TILE, w), lambda s: (t(s), 0)),
                pl.BlockSpec((ROW_TILE, LANE), lambda s: (t(s), 0)),
                pl.BlockSpec((tl, LANE), lambda s: (t(s), 0))]

    out_sds = jax.ShapeDtypeStruct((nt * ROW_TILE, out_w), BF16)
    return pl.pallas_call(
        _gdn_kernel, grid=(nt,),
        in_specs=dir_specs(tf) + dir_specs(tb),
        out_specs=[pl.BlockSpec((ROW_TILE, out_w), lambda s: (tf(s), 0)),
                   pl.BlockSpec((ROW_TILE, out_w), lambda s: (tb(s), 0))],
        out_shape=[out_sds, out_sds],
        scratch_shapes=[pltpu.VMEM((2 * GDN_HEADS, GDN_DK, GDN_DK), F32)],
        compiler_params=_cparams(1), name="gdn_scan",
    )(qkv, gates, gates_t, qkv, gates, gates_t)


def _merge_ab_kernel(gf, gb, df, db, gz, dz, gn, dn, o_ref):
    gla_w = GLA_HEADS * GLA_DV
    og = gf[...].astype(F32) + gb[...].astype(F32)
    zg = gz[...]
    for h in range(GLA_HEADS):
        sl = slice(h * GLA_DV, (h + 1) * GLA_DV)
        seg = og[:, sl]
        y = seg * lax.rsqrt(jnp.mean(seg * seg, axis=-1, keepdims=True) + NORM_EPS) * gn[...]
        o_ref[:, sl] = (y * _silu(zg[:, sl])).astype(BF16)
    od = df[...].astype(F32) + db[...].astype(F32)
    zd = dz[...]
    for h in range(GDN_HEADS):
        sl = slice(h * GDN_DK, (h + 1) * GDN_DK)
        seg = od[:, sl]
        y = seg * lax.rsqrt(jnp.mean(seg * seg, axis=-1, keepdims=True) + NORM_EPS) * dn[...]
        o_ref[:, gla_w + h * GDN_DK:gla_w + (h + 1) * GDN_DK] = (y * _silu(zd[:, sl])).astype(BF16)


def _out0_kernel(gf, gb, df, db, gz, dz, gn, dn, w_ref, x_ref, c_ref, m_ref, m1_ref, g1_ref,
                 o_ref, h1_ref, a_scr, *, n_lat_tiles, d):
    is_ctx = pl.program_id(0) == n_lat_tiles
    _merge_ab_kernel(gf, gb, df, db, gz, dz, gn, dn, a_scr)
    acc = _dot(a_scr[...], w_ref[...])
    m = m_ref[...]
    gate = jnp.where(is_ctx, m[1:2, 2 * d:], m[0:1, 2 * d:])
    xn = jnp.where(is_ctx, c_ref[...], x_ref[...]) + gate * acc
    o_ref[...] = xn
    m1 = m1_ref[...]
    y = xn * lax.rsqrt(jnp.mean(xn * xn, axis=-1, keepdims=True) + NORM_EPS) * g1_ref[...]
    shift = jnp.where(is_ctx, m1[1:2, :d], m1[0:1, :d])
    scale = jnp.where(is_ctx, m1[1:2, d:2 * d], m1[0:1, d:2 * d])
    h1_ref[...] = (y * (1.0 + scale) + shift).astype(BF16)


def _out0_call(gla_f, gla_b, gdn_f, gdn_b, zab, gla_norm, gdn_norm, w_out, x_lat, x_ctx, mods,
               mods_next, g_next, n_lat_tiles):
    m = gla_f.shape[0]
    w = GLA_HEADS * GLA_DV
    k, d = w_out.shape
    row = lambda i: (i, 0)
    const = lambda i: (0, 0)
    return pl.pallas_call(
        functools.partial(_out0_kernel, n_lat_tiles=n_lat_tiles, d=d), grid=(m // ROW_TILE,),
        in_specs=[pl.BlockSpec((ROW_TILE, w), row)] * 4
        + [pl.BlockSpec((ROW_TILE, w), lambda i: (i, 2)), pl.BlockSpec((ROW_TILE, w), lambda i: (i, 6)),
           pl.BlockSpec((1, GLA_DV), const), pl.BlockSpec((1, GDN_DK), const),
           pl.BlockSpec((k, d), const),
           pl.BlockSpec((ROW_TILE, d), lambda i: (jnp.minimum(i, n_lat_tiles - 1), 0)),
           pl.BlockSpec((ROW_TILE, d), const),
           pl.BlockSpec((8, 3 * d), const), pl.BlockSpec((8, 3 * d), const), pl.BlockSpec((1, d), const)],
        out_specs=[pl.BlockSpec((ROW_TILE, d), row), pl.BlockSpec((ROW_TILE, d), row)],
        out_shape=[jax.ShapeDtypeStruct((m, d), F32), jax.ShapeDtypeStruct((m, d), BF16)],
        scratch_shapes=[pltpu.VMEM((ROW_TILE, k), BF16)],
        compiler_params=_cparams(1), name="out_proj_ab",
    )(gla_f, gla_b, gdn_f, gdn_b, zab, zab, gla_norm.reshape(1, -1), gdn_norm.reshape(1, -1),
      w_out, x_lat, x_ctx, mods, mods_next, g_next.reshape(1, d))


def _lru_gates(seg, wa, ba, wx, bx, lam):
    seg16 = seg.astype(BF16)
    sigmoid = lambda t: 0.5 * jnp.tanh(0.5 * t) + 0.5
    r = sigmoid(_dot(seg16, wa) + ba)
    gi = sigmoid(_dot(seg16, wx) + bx)
    log_a = -LRU_C * _softplus(-lam) * r
    a = jnp.exp(log_a)
    return a, jnp.sqrt(jnp.tanh(-log_a) * (a * a + 1.0)) * (gi * seg)


def _lru_ctx_kernel(x_ref, cw_ref, cb_ref, wa_ref, ba_ref, wx_ref, bx_ref, lam_ref, o_ref, a_scr, d_scr):
    x = x_ref[...]
    rows, width = x.shape
    blk = width // LRU_BLOCKS
    cw = cw_ref[...]
    row = lax.broadcasted_iota(jnp.int32, x.shape, 0)
    zero = jnp.zeros((1, width), F32)
    xc = (cw[0:1] * _row_shift(x, -1, row, zero, zero, zero) + cw[1:2] * x
          + cw[2:3] * _row_shift(x, 1, row, zero, zero, zero)
          + cw[3:4] * _row_shift(x, 2, row, zero, zero, zero) + cb_ref[...])
    for d in range(2):
        for b in range(LRU_BLOCKS):
            sl = slice(b * blk, (b + 1) * blk)
            a, drv = _lru_gates(xc[:, sl], wa_ref[d, b], ba_ref[d, :, sl], wx_ref[d, b],
                                bx_ref[d, :, sl], lam_ref[d, :, sl])
            a_scr[d, :, sl] = a
            d_scr[d, :, sl] = drv

    def body(t, carry):
        hf, hb = carry
        hf = a_scr[0, pl.ds(t, 1), :] * hf + d_scr[0, pl.ds(t, 1), :]
        tb = rows - 1 - t
        hb = a_scr[1, pl.ds(tb, 1), :] * hb + d_scr[1, pl.ds(tb, 1), :]
        return hf, hb

    hf, hb = lax.fori_loop(0, rows, body, (zero, zero), unroll=8)
    o_ref[...] = jnp.zeros_like(o_ref)
    o_ref[0:1, :] = hf
    o_ref[1:2, :] = hb


def _lru_lat_kernel(xf, pf, nf, xb, pb, nb, h0_ref, cw_ref, cb_ref, wa_ref, ba_ref, wx_ref, bx_ref,
                    lam_ref, of, ob, a_scr, d_scr, hl_scr, pc_scr, carry, *, n_colblocks):
    ci, s = pl.program_id(0), pl.program_id(1)

    @pl.when(s == 0)
    def _():
        carry[...] = h0_ref[...]

    cw = cw_ref[...]
    dirs = ((xf, pf, nf, of, s), (xb, pb, nb, ob, n_colblocks - 1 - s))
    for d, (x_ref, p_ref, n_ref, o_ref, cblk) in enumerate(dirs):
        forward = d == 0
        x3 = x_ref[...]
        rows, ncol, tc = x3.shape
        has_prev, has_next = cblk > 0, cblk < n_colblocks - 1
        sub = lax.broadcasted_iota(jnp.int32, (ncol, tc), 0)
        prev_last = jnp.where(has_prev, p_ref[7, ncol - 1:ncol, :], 0.0)
        next_r0 = jnp.where(has_next, n_ref[0, 0:1, :], 0.0)
        next_r1 = jnp.where(has_next, n_ref[1, 0:1, :], 0.0)
        first_m1 = jnp.where(sub == 0, prev_last, pltpu.roll(x3[rows - 1], 1, 0))
        last_p1 = jnp.where(sub == ncol - 1, next_r0, pltpu.roll(x3[0], ncol - 1, 0))
        last_p2 = jnp.where(sub == ncol - 1, next_r1, pltpu.roll(x3[1], ncol - 1, 0))
        xm1 = jnp.concatenate([first_m1[None], x3[:rows - 1]], axis=0)
        xp1 = jnp.concatenate([x3[1:], last_p1[None]], axis=0)
        xp2 = jnp.concatenate([x3[2:], last_p1[None], last_p2[None]], axis=0)
        xc = cw[0:1] * xm1 + cw[1:2] * x3 + cw[2:3] * xp1 + cw[3:4] * xp2 + cb_ref[...]
        a, drv = _lru_gates(xc.reshape(rows * ncol, tc), wa_ref[d, ci], ba_ref[d], wx_ref[d, ci],
                            bx_ref[d], lam_ref[d])
        a_scr[d] = a.reshape(rows, ncol, tc)
        d_scr[d] = drv.reshape(rows, ncol, tc)

    def body(i, carry_):
        out = []
        for d, r in ((0, i), (1, rows - 1 - i)):
            h, p = carry_[2 * d], carry_[2 * d + 1]
            ar = a_scr[d, r]
            h = ar * h + d_scr[d, r]
            p = ar * p
            hl_scr[d, r] = h
            pc_scr[d, r] = p
            out += [h, p]
        return tuple(out)

    zero, one = jnp.zeros((ncol, tc), F32), jnp.ones((ncol, tc), F32)
    lax.fori_loop(0, rows, body, (zero, one, zero, one), unroll=8)

    for d, (x_ref, p_ref, n_ref, o_ref, cblk) in enumerate(dirs):
        forward = d == 0
        end = rows - 1 if forward else 0
        hl_end, pc_end = hl_scr[d, end], pc_scr[d, end]
        h = carry[d:d + 1, :]
        h_in = jnp.zeros((ncol, tc), F32)
        for j in (range(ncol) if forward else reversed(range(ncol))):
            h_in = jnp.where(sub == j, h, h_in)
            h = hl_end[j:j + 1, :] + pc_end[j:j + 1, :] * h
        carry[d:d + 1, :] = h
        o_ref[...] = hl_scr[d] + pc_scr[d] * h_in[None]


def _lru_call(zcd, conv_w, conv_b, wa, ba, wx, bx, lam, n_lat):
    m, ncols_z = zcd.shape
    width = conv_w.shape[1]
    n_cols = GRID_W
    rows = n_lat // n_cols
    tc = width // LRU_BLOCKS
    ncb = n_cols // 8
    full = lambda a: pl.BlockSpec(a.shape, lambda *_: (0,) * a.ndim)
    wa16, wx16 = wa.astype(BF16), wx.astype(BF16)
    cb2, ba3, bx3, lam3 = (conv_b.reshape(1, width), ba.reshape(2, 1, width), bx.reshape(2, 1, width),
                           lam.reshape(2, 1, width))

    params = (conv_w, cb2, wa16, ba3, wx16, bx3, lam3)
    h0 = pl.pallas_call(
        _lru_ctx_kernel, grid=(1,),
        in_specs=[pl.BlockSpec((ROW_TILE, width), lambda i: (n_lat // ROW_TILE, 0))] + [full(p) for p in params],
        out_specs=pl.BlockSpec((8, width), lambda i: (0, 0)),
        out_shape=jax.ShapeDtypeStruct((8, width), F32),
        scratch_shapes=[pltpu.VMEM((2, ROW_TILE, width), F32)] * 2,
        compiler_params=_cparams(1), name="rglru_ctx",
    )(zcd, *params)

    z3 = zcd.reshape(m // n_cols, n_cols, ncols_z)
    hb8 = rows // 8
    bf = lambda s: s
    bb = lambda s: ncb - 1 - s

    def dir_specs(blk):
        return [pl.BlockSpec((rows, 8, tc), lambda c, s: (0, blk(s), c)),
                pl.BlockSpec((8, 8, tc), lambda c, s: (hb8 - 1, jnp.maximum(blk(s) - 1, 0), c)),
                pl.BlockSpec((8, 8, tc), lambda c, s: (0, jnp.minimum(blk(s) + 1, ncb - 1), c))]

    vec = lambda lead: pl.BlockSpec((lead, 1, tc), lambda c, s: (0, 0, c))
    out_sds = jax.ShapeDtypeStruct((rows, n_cols, width), F32)
    hf, hb = pl.pallas_call(
        functools.partial(_lru_lat_kernel, n_colblocks=ncb), grid=(width // tc, ncb),
        in_specs=dir_specs(bf) + dir_specs(bb)
        + [pl.BlockSpec((8, tc), lambda c, s: (0, c)),
           pl.BlockSpec((4, tc), lambda c, s: (0, c)), pl.BlockSpec((1, tc), lambda c, s: (0, c)),
           full(wa16), vec(2), full(wx16), vec(2), vec(2)],
        out_specs=[pl.BlockSpec((rows, 8, tc), lambda c, s: (0, bf(s), c)),
                   pl.BlockSpec((rows, 8, tc), lambda c, s: (0, bb(s), c))],
        out_shape=[out_sds, out_sds],
        scratch_shapes=[pltpu.VMEM((2, rows, 8, tc), F32)] * 4 + [pltpu.VMEM((8, tc), F32)],
        compiler_params=_cparams(2), name="rglru_scan",
    )(z3, z3, z3, z3, z3, z3, h0, conv_w, cb2, wa16, ba3, wx16, bx3, lam3)
    return hf.reshape(n_lat, width), hb.reshape(n_lat, width)


def _hy_prep_kernel(*refs, n_lat_tiles):
    xs, (w_ref, b_ref, x0_ref, z_ref) = refs[:9], refs[9:]
    i = pl.program_id(0)
    has_prev = i > 0
    has_next = i < n_lat_tiles - 1
    outs = []
    width = x0_ref.shape[1]
    for j in range(3):
        x_ref, p_ref, n_ref = xs[3 * j:3 * j + 3]
        x = x_ref[...]
        row = lax.broadcasted_iota(jnp.int32, x.shape, 0)
        prev_row = jnp.where(has_prev, p_ref[7:8, :], 0.0)
        next0 = jnp.where(has_next, n_ref[0:1, :], 0.0)
        sl = slice(j * width, (j + 1) * width)
        outs.append(w_ref[0:1, sl] * _row_shift(x, -1, row, prev_row, next0, next0)
                    + w_ref[1:2, sl] * x
                    + w_ref[2:3, sl] * _row_shift(x, 1, row, prev_row, next0, next0)
                    + b_ref[:, sl])
    x0_ref[...] = outs[0]
    z_ref[...] = outs[1] * outs[2]


def _hy_prep_call(zcd, conv_w, conv_b, n_lat_tiles):
    width = conv_w.shape[1] // 3
    tm = _row_tile(n_lat_tiles, 2)
    n_tiles = n_lat_tiles * ROW_TILE // tm
    hb = tm // 8
    last8 = n_tiles * hb - 1
    in_specs, args = [], []
    for j in range(3):
        cbk = 2 + j
        in_specs += [pl.BlockSpec((tm, width), lambda i, cbk=cbk: (i, cbk)),
                     pl.BlockSpec((8, width), lambda i, cbk=cbk: (jnp.maximum(i * hb - 1, 0), cbk)),
                     pl.BlockSpec((8, width), lambda i, cbk=cbk: (jnp.minimum((i + 1) * hb, last8), cbk))]
        args += [zcd, zcd, zcd]
    in_specs += [pl.BlockSpec((3, 3 * width), lambda i: (0, 0)), pl.BlockSpec((1, 3 * width), lambda i: (0, 0))]
    out_sds = jax.ShapeDtypeStruct((n_tiles * tm, width), F32)
    return pl.pallas_call(
        functools.partial(_hy_prep_kernel, n_lat_tiles=n_tiles), grid=(n_tiles,),
        in_specs=in_specs,
        out_specs=[pl.BlockSpec((tm, width), lambda i: (i, 0))] * 2,
        out_shape=[out_sds, out_sds],
        compiler_params=_cparams(1), name="hyena_prep",
    )(*args, conv_w, conv_b.reshape(1, -1))


def _hy_filter_kernel(e_ref, w1, b1, w2, b2, w3, fr, dec, f_ref, ssq_ref):
    i = pl.program_id(0)
    e = e_ref[...]
    hid = jnp.sin(fr[...] * (_dot(e, w1[...], HI) + b1[...]))
    hid = jnp.sin(fr[...] * (_dot(hid, w2[...], HI) + b2[...]))
    filt = _dot(hid.astype(BF16), w3[...]) * jnp.exp(-e[:, 0:1] * dec[...])
    half = filt.shape[1] // 2
    row = lax.broadcasted_iota(jnp.int32, filt.shape, 0)
    col = lax.broadcasted_iota(jnp.int32, filt.shape, 1)
    unused = jnp.logical_and(jnp.logical_and(i == 0, row == 0), col >= half)
    filt = jnp.where(unused, 0.0, filt)
    f_ref[...] = filt

    @pl.when(i == 0)
    def _():
        ssq_ref[...] = jnp.zeros_like(ssq_ref)

    ssq_ref[...] = ssq_ref[...] + jnp.sum(filt * filt, axis=0, keepdims=True)


def _hy_filter_call(emb, w1, b1, w2, b2, w3, freq, decay):
    length = emb.shape[0]
    n_out = w3.shape[1]
    hid = w1.shape[1]
    pad_r = lambda a: jnp.pad(a, ((0, LANE - a.shape[0]), (0, 0)))
    pad_c = lambda a: jnp.pad(a, ((0, 0), (0, LANE - a.shape[1])))
    params = (pad_c(pad_r(w1)), pad_c(b1.reshape(1, hid)), pad_c(pad_r(w2)), pad_c(b2.reshape(1, hid)),
              pad_r(w3).astype(BF16), pad_c(freq.reshape(1, hid)), decay.reshape(1, n_out))
    full = lambda a: pl.BlockSpec(a.shape, lambda i: (0,) * a.ndim)
    return pl.pallas_call(
        _hy_filter_kernel, grid=(length // ROW_TILE,),
        in_specs=[pl.BlockSpec((ROW_TILE, LANE), lambda i: (i, 0))] + [full(p) for p in params],
        out_specs=[pl.BlockSpec((ROW_TILE, n_out), lambda i: (i, 0)),
                   pl.BlockSpec((8, n_out), lambda i: (0, 0))],
        out_shape=[jax.ShapeDtypeStruct((length, n_out), F32), jax.ShapeDtypeStruct((8, n_out), F32)],
        compiler_params=_cparams(1), name="hyena_filter",
    )(emb, *params)


SUB = 16


def _dft_a_kernel(x_ref, c_ref, s_ref, re_ref, im_ref):
    xt = pltpu.einshape("kjc->jkc", x_ref[...])
    res_re, res_im = [], []
    for j in range(SUB):
        xb = xt[j].astype(BF16)
        res_re.append(_dot(c_ref[...], xb))
        res_im.append(_dot(s_ref[...], xb))
    re_ref[...] = pltpu.einshape("jkc->kjc", jnp.stack(res_re)).astype(BF16)
    im_ref[...] = pltpu.einshape("jkc->kjc", jnp.stack(res_im)).astype(BF16)


def _dft_a_call(x, cmat, smat, n1):
    length, ch = x.shape
    n2, n2h = cmat.shape
    tc = 512
    x4 = x.reshape(n2h, n1 // SUB, SUB, ch)
    out_sds = jax.ShapeDtypeStruct((n2, n1 // SUB, SUB, ch), BF16)
    re, im = pl.pallas_call(
        _dft_a_kernel, grid=(ch // tc, n1 // SUB),
        in_specs=[pl.BlockSpec((n2h, None, SUB, tc), lambda c, i: (0, i, 0, c)),
                  pl.BlockSpec((n2, n2h), lambda c, i: (0, 0)),
                  pl.BlockSpec((n2, n2h), lambda c, i: (0, 0))],
        out_specs=[pl.BlockSpec((n2, None, SUB, tc), lambda c, i: (0, i, 0, c))] * 2,
        out_shape=[out_sds, out_sds],
        compiler_params=_cparams(2), name="dft_stage_a",
    )(x4, cmat, smat)
    return re.reshape(n2, n1, ch), im.reshape(n2, n1, ch)


def _dft_b_kernel(are, aim, pre, pim, fre, fim, g_ref, gi_ref, sp_ref, sf_ref, bre_ref, bim_ref,
                  *, n1, inv_n):
    scale = lax.rsqrt(sp_ref[0:1, :] + sf_ref[0:1, :] + NORM_EPS) * inv_n
    res_re, res_im = [], []
    for j in range(SUB):
        g = g_ref[j]
        cat = lambda r, i: jnp.concatenate([r[j], i[j]], axis=0)
        xp, xf, x = _dot(g, cat(pre, pim)), _dot(g, cat(fre, fim)), _dot(g, cat(are, aim))
        hr = (xp[:n1] + xf[:n1]) * scale
        hi = (xp[n1:] - xf[n1:]) * scale
        xr, xi = x[:n1], x[n1:]
        y = jnp.concatenate([xr * hr - xi * hi, xr * hi + xi * hr], axis=0).astype(BF16)
        b = _dot(gi_ref[j], y)
        res_re.append(b[:n1])
        res_im.append(b[n1:])
    bre_ref[...] = pltpu.einshape("jkc->kjc", jnp.stack(res_re)).astype(BF16)
    bim_ref[...] = pltpu.einshape("jkc->kjc", jnp.stack(res_im)).astype(BF16)


def _dft_b_call(a_re, a_im, f_re, f_im, gmat, gimat, ssq):
    n2, n1, ch = a_re.shape
    tc = 256
    nct = ch // tc
    slab = lambda half: pl.BlockSpec((SUB, n1, tc), lambda k, c: (k, 0, half * nct + c))
    gspec = pl.BlockSpec((SUB, 2 * n1, 2 * n1), lambda k, c: (k, 0, 0))
    out_sds = jax.ShapeDtypeStruct((n1, n2 // SUB, SUB, ch), BF16)
    re, im = pl.pallas_call(
        functools.partial(_dft_b_kernel, n1=n1, inv_n=1.0 / (n1 * n2)), grid=(n2 // SUB, nct),
        in_specs=[slab(0), slab(0), slab(0), slab(0), slab(1), slab(1), gspec, gspec,
                  pl.BlockSpec((8, tc), lambda k, c: (0, c)),
                  pl.BlockSpec((8, tc), lambda k, c: (0, nct + c))],
        out_specs=[pl.BlockSpec((n1, None, SUB, tc), lambda k, c: (0, k, 0, c))] * 2,
        out_shape=[out_sds, out_sds],
        compiler_params=_cparams(2), name="dft_stage_b",
    )(a_re, a_im, f_re, f_im, f_re, f_im, gmat, gimat, ssq, ssq)
    return re.reshape(n1, n2, ch), im.reshape(n1, n2, ch)


def _dft_c_kernel(bre, bim, x0_ref, z_ref, hz_ref, bias_ref, c_ref, s_ref, o_ref):
    ys = [_dot(c_ref[...], bre[j].astype(BF16)) + _dot(s_ref[...], bim[j].astype(BF16)) for j in range(SUB)]
    y = pltpu.einshape("jkc->kjc", jnp.stack(ys))
    o_ref[...] = x0_ref[...] * (y + z_ref[...] * bias_ref[...]) * _silu(hz_ref[...])


def _dft_c_call(b_re, b_im, x0, z, zcd, bias, cmat, smat):
    n1, n2, ch = b_re.shape
    n2h = n2
    length = n1 * n2h
    tc = 256
    nct = ch // tc
    view = lambda a: a.reshape(a.shape[0] // n1, n1 // SUB, SUB, a.shape[1])
    tspec = lambda cblk: pl.BlockSpec((n2h, None, SUB, tc), lambda i, c: (0, i, 0, cblk * nct + c))
    out = pl.pallas_call(
        _dft_c_kernel, grid=(n1 // SUB, nct),
        in_specs=[pl.BlockSpec((SUB, n2, tc), lambda i, c: (i, 0, c))] * 2
        + [tspec(0), tspec(0), tspec(5), pl.BlockSpec((1, tc), lambda i, c: (0, c)),
           pl.BlockSpec((n2h, n2), lambda i, c: (0, 0)), pl.BlockSpec((n2h, n2), lambda i, c: (0, 0))],
        out_specs=tspec(0),
        out_shape=jax.ShapeDtypeStruct((n2h, n1 // SUB, SUB, ch), F32),
        compiler_params=_cparams(2), name="dft_stage_c",
    )(b_re, b_im, view(x0), view(z), view(zcd), bias.reshape(1, ch), cmat, smat)
    return out.reshape(length, ch)


def _dft_constants(n1, n2):
    n = n1 * n2
    n2h = n2 // 2
    k2 = jnp.arange(n2h, dtype=jnp.int32)
    ang_a = (math.pi / n2) * ((k2[None, :] * (2 * k2[:, None] + 1)) % (2 * n2)).astype(F32)
    ca, sa = jnp.cos(ang_a), -jnp.sin(ang_a)
    cc, sc = jnp.cos(ang_a).T, -jnp.sin(ang_a).T
    i1 = jnp.arange(n1, dtype=jnp.int32)
    ang_f = (2.0 * math.pi / n1) * ((i1[:, None] * i1[None, :]) % n1).astype(F32)
    ang_t = (math.pi / n) * ((2 * k2[:, None] + 1) * i1[None, :]).astype(F32)
    fr, fi = jnp.cos(ang_f)[None], -jnp.sin(ang_f)[None]
    wr, wi = jnp.cos(ang_t)[:, None, :], -jnp.sin(ang_t)[:, None, :]
    gr, gi = fr * wr - fi * wi, fr * wi + fi * wr
    g = jnp.concatenate([jnp.concatenate([gr, -gi], axis=2), jnp.concatenate([gi, gr], axis=2)], axis=1)
    tr, ti = jnp.swapaxes(gr, 1, 2), -jnp.swapaxes(gi, 1, 2)
    ginv = jnp.concatenate([jnp.concatenate([tr, -ti], axis=2), jnp.concatenate([ti, tr], axis=2)], axis=1)
    b = lambda a: a.astype(BF16)
    return b(ca), b(sa), b(cc), b(sc), b(g), b(ginv)


def _final_kernel(hf, hb, lz, hy, w1, w2, x_ref, m_ref, g_ref, o_ref, *, d):
    a1 = ((hf[...] + hb[...]) * _silu(lz[...])).astype(BF16)
    acc = _dot(a1, w1[...]) + _dot(hy[...].astype(BF16), w2[...])
    xn = x_ref[...] + m_ref[0:1, 2 * d:] * acc
    o_ref[...] = xn * lax.rsqrt(jnp.mean(xn * xn, axis=-1, keepdims=True) + NORM_EPS) * g_ref[...]


def _final_call(hf, hb, zcd, hy, w_out, x1, mods, final_g, n_lat_tiles):
    d = w_out.shape[1]
    half = w_out.shape[0] // 2
    row = lambda i: (i, 0)
    const = lambda i: (0, 0)
    tm = _row_tile(n_lat_tiles, 2)
    once = dict(pipeline_mode=pl.Buffered(1))
    return pl.pallas_call(
        functools.partial(_final_kernel, d=d), grid=(n_lat_tiles * ROW_TILE // tm,),
        in_specs=[pl.BlockSpec((tm, half), row), pl.BlockSpec((tm, half), row),
                  pl.BlockSpec((tm, half), lambda i: (i, 1)), pl.BlockSpec((tm, half), row),
                  pl.BlockSpec((half, d), const, **once), pl.BlockSpec((half, d), lambda i: (1, 0), **once),
                  pl.BlockSpec((tm, d), row), pl.BlockSpec((8, 3 * d), const),
                  pl.BlockSpec((1, d), const)],
        out_specs=pl.BlockSpec((tm, d), row),
        out_shape=jax.ShapeDtypeStruct((n_lat_tiles * ROW_TILE, d), F32),
        compiler_params=_cparams(1), name="out_proj_cd_final",
    )(hf, hb, zcd, hy, w_out, w_out, x1, mods, final_g.reshape(1, d))


def kernel(x, c, ctx, c_ctx, mod_w, mod_b, norm_g, ab_w_in, ab_gla_wg2, ab_gla_bg2, ab_gla_norm, ab_gdn_conv, ab_gdn_a_log, ab_gdn_dt_bias, ab_gdn_norm, ab_w_out, cd_w_in, cd_lru_conv_w, cd_lru_conv_b, cd_lru_wa, cd_lru_ba, cd_lru_wx, cd_lru_bx, cd_lru_lambda, cd_hy_conv_w, cd_hy_conv_b, cd_hy_w1, cd_hy_b1, cd_hy_w2, cd_hy_b2, cd_hy_w3, cd_hy_freq, cd_hy_decay, cd_hy_bias, cd_w_out, final_g):
    _, n_lat, d = x.shape
    assert x.shape[0] == 1 and ctx.shape[1] == ROW_TILE and n_lat % ROW_TILE == 0
    assert mod_w.shape[0] == 2 and d == 2 * GLA_HEADS * GLA_DV
    nlt = n_lat // ROW_TILE
    x_lat, x_ctx = x[0], ctx[0]

    cvec = jnp.zeros((8, d), F32).at[0].set(c[0]).at[1].set(c_ctx)
    mods = _mod_call(cvec, mod_w, mod_b)

    gla_qk, gla_v = GLA_HEADS * GLA_DK, GLA_HEADS * GLA_DV
    gdn_qk = GDN_HEADS * GDN_DK
    sizes = (gla_qk, gla_qk, gla_v, 2 * GLA_RANK, gla_v, 3 * gdn_qk, 2 * GDN_HEADS, 2 * GDN_HEADS, gdn_qk)
    gq, gk, gv, glr, gz, dqkv, da, db, dz = jnp.split(ab_w_in[0], np.cumsum(sizes)[:-1].tolist(), axis=1)
    w_main = jnp.concatenate([gq, gk, gv, gz, dqkv, dz], axis=1).astype(BF16)
    n_small = glr.shape[1] + da.shape[1] + db.shape[1]
    w_small = jnp.concatenate([glr, da, db, jnp.zeros((d, LANE - n_small), F32)], axis=1).astype(BF16)

    h0 = _norm_mod_call(x_lat, x_ctx, mods[0], norm_g[0], nlt)
    zab = _mm_call(h0, w_main, 1024, "in_proj_ab")
    zsm = _mm_call(h0, w_small, LANE, "in_proj_ab_small")

    gw = GLA_HEADS * GLA_DK
    wg = jnp.zeros((LANE, 2 * gw), F32)
    for dd in range(2):
        wg = wg.at[dd * GLA_RANK:(dd + 1) * GLA_RANK, dd * gw:(dd + 1) * gw].set(ab_gla_wg2[0, dd])
    gbias = ab_gla_bg2[0].reshape(1, 2 * gw)
    gparams = jnp.zeros((8, LANE), F32)
    gparams = gparams.at[0, GDN_A_LANE:GDN_B_LANE].set(ab_gdn_a_log[0].reshape(-1))
    gparams = gparams.at[1, GDN_A_LANE:GDN_B_LANE].set(ab_gdn_dt_bias[0].reshape(-1))
    qkv, gla_cum, gates, gates_t = _ab_prep_call(zab, zsm, ab_gdn_conv[0], wg.astype(BF16), gbias, gparams, nlt)
    gla_f, gla_b = _gla_call(zab, gla_cum, nlt)
    gdn_f, gdn_b = _gdn_call(qkv, gates, gates_t, nlt)

    x1, h1 = _out0_call(gla_f, gla_b, gdn_f, gdn_b, zab, ab_gla_norm[0], ab_gdn_norm[0],
                        ab_w_out[0].astype(BF16), x_lat, x_ctx, mods[0], mods[1], norm_g[1], nlt)

    zcd = _mm_call(h1, cd_w_in[0].astype(BF16), 1024, "in_proj_cd")

    lru_f, lru_b = _lru_call(zcd, cd_lru_conv_w[0], cd_lru_conv_b[0], cd_lru_wa[0], cd_lru_ba[0],
                             cd_lru_wx[0], cd_lru_bx[0], cd_lru_lambda[0], n_lat)

    x0, z = _hy_prep_call(zcd, cd_hy_conv_w[0], cd_hy_conv_b[0], nlt)

    pos = jnp.arange(n_lat, dtype=F32)
    t01 = pos / max(n_lat - 1, 1)
    bands = jnp.linspace(1e-4, HY_BANDS - 1, HY_BANDS, dtype=F32)
    ang = (2.0 * math.pi / n_lat) * pos[:, None] * bands[None, :]
    emb = jnp.concatenate([t01[:, None], jnp.cos(ang), jnp.sin(ang)], axis=-1)
    emb = jnp.pad(emb, ((0, 0), (0, LANE - emb.shape[1])))
    filt, ssq = _hy_filter_call(emb, cd_hy_w1[0], cd_hy_b1[0], cd_hy_w2[0], cd_hy_b2[0], cd_hy_w3[0],
                                cd_hy_freq[0], cd_hy_decay[0])

    n1 = LANE if n_lat >= LANE * LANE else 32
    n2 = 2 * n_lat // n1
    ca, sa, cc, sc, gmat, gimat = _dft_constants(n1, n2)
    fa_re, fa_im = _dft_a_call(filt, ca, sa, n1)
    za_re, za_im = _dft_a_call(z, ca, sa, n1)
    b_re, b_im = _dft_b_call(za_re, za_im, fa_re, fa_im, gmat, gimat, ssq)
    y_hy = _dft_c_call(b_re, b_im, x0, z, zcd, cd_hy_bias[0], cc, sc)

    out = _final_call(lru_f, lru_b, zcd, y_hy, cd_w_out[0].astype(BF16), x1, mods[1], final_g, nlt)
    return out[None]
```

```python
import functools
import math

import numpy as np
import jax
import jax.numpy as jnp
from jax import lax
from jax.experimental import pallas as pl
from jax.experimental.pallas import tpu as pltpu

F32 = jnp.float32
BF16 = jnp.bfloat16
HI = lax.Precision.HIGHEST

NORM_EPS = 1e-6
CHUNK = 64
ROW_TILE = 256
GRID_W = 64
LANE = 128
GLA_HEADS, GLA_DK, GLA_DV, GLA_RANK, GLA_GATE_NORM = 4, 128, 256, 16, 16.0
GDN_HEADS, GDN_DK = 8, 128
LRU_BLOCKS, LRU_C = 8, 8.0
HY_BANDS = 16
VMEM_LIMIT = 56 * 1024 * 1024


def _cparams(n_axes):
    return pltpu.CompilerParams(dimension_semantics=("arbitrary",) * n_axes,
                                vmem_limit_bytes=VMEM_LIMIT)


def _silu(x):
    return x * jax.nn.sigmoid(x)


def _softplus(x):
    return jnp.maximum(x, 0.0) + jnp.log1p(jnp.exp(-jnp.abs(x)))


def _dot(a, b, precision=None):
    return jnp.dot(a, b, precision=precision, preferred_element_type=F32)


def _mask_dot_f32(mask16, x):
    hi = x.astype(BF16)
    r = x - hi.astype(F32)
    mid = r.astype(BF16)
    lo = (r - mid.astype(F32)).astype(BF16)
    return _dot(mask16, hi) + _dot(mask16, mid) + _dot(mask16, lo)


def _bdot(a, b):
    return lax.dot_general(a, b, (((2,), (1,)), ((0,), (0,))), preferred_element_type=F32)


def _bdot_nt(a, b):
    return lax.dot_general(a, b, (((2,), (2,)), ((0,), (0,))), preferred_element_type=F32)


def _bdot_tn(a, b):
    return lax.dot_general(a, b, (((1,), (1,)), ((0,), (0,))), preferred_element_type=F32)


def _mod_kernel(c_ref, w_ref, b_ref, o_ref):
    o_ref[...] = _dot(_silu(c_ref[...]), w_ref[...], HI) + b_ref[...]


def _mod_call(cvec, mod_w, mod_b):
    depth, d, n = mod_w.shape
    tn = 512
    return pl.pallas_call(
        _mod_kernel, grid=(depth, n // tn),
        in_specs=[pl.BlockSpec((8, d), lambda l, j: (0, 0)),
                  pl.BlockSpec((None, d, tn), lambda l, j: (l, 0, j)),
                  pl.BlockSpec((None, 1, tn), lambda l, j: (l, 0, j))],
        out_specs=pl.BlockSpec((None, 8, tn), lambda l, j: (l, 0, j)),
        out_shape=jax.ShapeDtypeStruct((depth, 8, n), F32),
        compiler_params=_cparams(2), name="adaln_mod",
    )(cvec, mod_w, mod_b.reshape(depth, 1, n))


def _norm_mod_kernel(x_ref, c_ref, m_ref, g_ref, ws_ref, o_ref, zs_ref, *, n_lat_tiles, d):
    is_ctx = pl.program_id(0) == n_lat_tiles
    xt = jnp.where(is_ctx, c_ref[...], x_ref[...])
    y = xt * lax.rsqrt(jnp.mean(xt * xt, axis=-1, keepdims=True) + NORM_EPS) * g_ref[...]
    m = m_ref[...]
    shift = jnp.where(is_ctx, m[1:2, :d], m[0:1, :d])
    scale = jnp.where(is_ctx, m[1:2, d:2 * d], m[0:1, d:2 * d])
    h = (y * (1.0 + scale) + shift).astype(BF16)
    o_ref[...] = h
    zs_ref[...] = _dot(h, ws_ref[...])


def _norm_mod_call(x_lat, x_ctx, mods, g, w_small, n_lat_tiles):
    d = x_lat.shape[1]
    nt = n_lat_tiles + 1
    const = lambda i: (0, 0)
    row = lambda i: (i, 0)
    return pl.pallas_call(
        functools.partial(_norm_mod_kernel, n_lat_tiles=n_lat_tiles, d=d), grid=(nt,),
        in_specs=[pl.BlockSpec((ROW_TILE, d), lambda i: (jnp.minimum(i, n_lat_tiles - 1), 0)),
                  pl.BlockSpec((ROW_TILE, d), const),
                  pl.BlockSpec((8, 3 * d), const), pl.BlockSpec((1, d), const),
                  pl.BlockSpec((d, LANE), const)],
        out_specs=[pl.BlockSpec((ROW_TILE, d), row), pl.BlockSpec((ROW_TILE, LANE), row)],
        out_shape=[jax.ShapeDtypeStruct((nt * ROW_TILE, d), BF16),
                   jax.ShapeDtypeStruct((nt * ROW_TILE, LANE), F32)],
        compiler_params=_cparams(1), name="norm_mod",
    )(x_lat, x_ctx, mods, g.reshape(1, d), w_small)


def _mm_kernel(a_ref, w_ref, o_ref):
    o_ref[...] = _dot(a_ref[...], w_ref[...])


def _row_tile(n_tiles, max_mult):
    k = max(m for m in range(1, max_mult + 1) if n_tiles % m == 0)
    return k * ROW_TILE


def _mm_call(a, w, tn, name):
    m, k = a.shape
    n = w.shape[1]
    tm = _row_tile(m // ROW_TILE, 5)
    return pl.pallas_call(
        _mm_kernel, grid=(n // tn, m // tm),
        in_specs=[pl.BlockSpec((tm, k), lambda j, i: (i, 0)),
                  pl.BlockSpec((k, tn), lambda j, i: (0, j))],
        out_specs=pl.BlockSpec((tm, tn), lambda j, i: (i, j)),
        out_shape=jax.ShapeDtypeStruct((m, n), F32),
        compiler_params=_cparams(2), name=name,
    )(a, w)


def _row_shift(x, k, row, prev_row, next0, next1):
    n = x.shape[0]
    if k == -1:
        return jnp.where(row == 0, prev_row, pltpu.roll(x, 1, 0))
    if k == 1:
        return jnp.where(row == n - 1, next0, pltpu.roll(x, n - 1, 0))
    r = pltpu.roll(x, n - 2, 0)
    return jnp.where(row == n - 2, next0, jnp.where(row == n - 1, next1, r))


GDN_A_LANE = 2 * GLA_RANK
GDN_B_LANE = GDN_A_LANE + 2 * GDN_HEADS


def _ab_prep_kernel(x_ref, p_ref, n_ref, w_ref, z_ref, wg_ref, gb_ref, gp_ref,
                    o_ref, cum_ref, gates_ref, gt_ref, *, n_lat_tiles):
    zs = z_ref[...]
    logit = _dot(zs.astype(BF16), wg_ref[...]) + gb_ref[...]
    g = -_softplus(-logit) / GLA_GATE_NORM
    log_a = -jnp.exp(gp_ref[0:1, :]) * _softplus(zs + gp_ref[1:2, :])
    beta = jax.nn.sigmoid(zs)
    lower, _, _ = _tri_masks(True)
    upper, _, _ = _tri_masks(False)
    lower, upper = lower.astype(BF16), upper.astype(BF16)
    half = cum_ref.shape[1] // 2
    lane = lax.broadcasted_iota(jnp.int32, (CHUNK, LANE), 1)
    lane_t = lax.broadcasted_iota(jnp.int32, (LANE, CHUNK), 0)
    bwd_lo, bwd_hi = GDN_A_LANE + GDN_HEADS, GDN_B_LANE
    for c in range(ROW_TILE // CHUNK):
        sl = slice(c * CHUNK, (c + 1) * CHUNK)
        cum_ref[sl, :half] = _mask_dot_f32(lower, g[sl, :half])
        cum_ref[sl, half:] = _mask_dot_f32(upper, g[sl, half:])
        cum_f = _mask_dot_f32(lower, log_a[sl])
        cum_b = _mask_dot_f32(upper, log_a[sl])
        is_b = jnp.logical_and(lane >= bwd_lo, lane < bwd_hi)
        gates_ref[sl, :] = jnp.where(lane >= GDN_B_LANE, beta[sl], jnp.where(is_b, cum_b, cum_f))
        is_bt = jnp.logical_and(lane_t >= bwd_lo, lane_t < bwd_hi)
        gt_ref[c * LANE:(c + 1) * LANE, :CHUNK] = jnp.where(is_bt, cum_b.T, cum_f.T)
        gt_ref[c * LANE:(c + 1) * LANE, CHUNK:] = jnp.zeros((LANE, LANE - CHUNK), F32)

    i = pl.program_id(0)
    has_prev = jnp.logical_and(i != 0, i != n_lat_tiles)
    has_next = i < n_lat_tiles - 1
    x = x_ref[...]
    row = lax.broadcasted_iota(jnp.int32, x.shape, 0)
    prev_row = jnp.where(has_prev, p_ref[7:8, :], 0.0)
    next0 = jnp.where(has_next, n_ref[0:1, :], 0.0)
    next1 = jnp.where(has_next, n_ref[1:2, :], 0.0)
    w = w_ref[...]
    y = (w[0:1] * _row_shift(x, -1, row, prev_row, next0, next1) + w[1:2] * x
         + w[2:3] * _row_shift(x, 1, row, prev_row, next0, next1)
         + w[3:4] * _row_shift(x, 2, row, prev_row, next0, next1))
    y = _silu(y)
    qk = GDN_HEADS * GDN_DK
    for h in range(2 * GDN_HEADS):
        seg = y[:, h * GDN_DK:(h + 1) * GDN_DK]
        seg = seg * lax.rsqrt(jnp.sum(seg * seg, axis=-1, keepdims=True) + NORM_EPS)
        if h < GDN_HEADS:
            seg = seg * (GDN_DK ** -0.5)
        o_ref[:, h * GDN_DK:(h + 1) * GDN_DK] = seg
    o_ref[:, 2 * qk:] = y[:, 2 * qk:]


def _ab_prep_call(zab, zsm, conv_w, wg, gbias, gparams, n_lat_tiles):
    nt = n_lat_tiles + 1
    w = 3 * GDN_HEADS * GDN_DK
    gw = wg.shape[1]
    hb = ROW_TILE // 8
    last8 = nt * hb - 1
    nsub = ROW_TILE // CHUNK
    rows = nt * ROW_TILE
    const = lambda i: (0, 0)
    return pl.pallas_call(
        functools.partial(_ab_prep_kernel, n_lat_tiles=n_lat_tiles), grid=(nt,),
        in_specs=[pl.BlockSpec((ROW_TILE, w), lambda i: (i, 1)),
                  pl.BlockSpec((8, w), lambda i: (jnp.maximum(i * hb - 1, 0), 1)),
                  pl.BlockSpec((8, w), lambda i: (jnp.minimum((i + 1) * hb, last8), 1)),
                  pl.BlockSpec((4, w), const),
                  pl.BlockSpec((ROW_TILE, LANE), lambda i: (i, 0)),
                  pl.BlockSpec((LANE, gw), const), pl.BlockSpec((1, gw), const),
                  pl.BlockSpec((8, LANE), const)],
        out_specs=[pl.BlockSpec((ROW_TILE, w), lambda i: (i, 0)),
                   pl.BlockSpec((ROW_TILE, gw), lambda i: (i, 0)),
                   pl.BlockSpec((ROW_TILE, LANE), lambda i: (i, 0)),
                   pl.BlockSpec((nsub * LANE, LANE), lambda i: (i, 0))],
        out_shape=[jax.ShapeDtypeStruct((rows, w), F32), jax.ShapeDtypeStruct((rows, gw), F32),
                   jax.ShapeDtypeStruct((rows, LANE), F32),
                   jax.ShapeDtypeStruct((nt * nsub * LANE, LANE), F32)],
        compiler_params=_cparams(1), name="ab_prep",
    )(zab, zab, zab, conv_w, zsm, wg, gbias, gparams)


def _tile_fwd(s, n_lat_tiles):
    return jnp.where(s == 0, n_lat_tiles, s - 1)


def _tile_bwd(s, n_lat_tiles):
    return jnp.where(s == 0, n_lat_tiles, n_lat_tiles - s)


def _tri_masks(forward):
    row = lax.broadcasted_iota(jnp.int32, (CHUNK, CHUNK), 0)
    col = lax.broadcasted_iota(jnp.int32, (CHUNK, CHUNK), 1)
    incl = (col <= row) if forward else (col >= row)
    strict = (col < row) if forward else (col > row)
    return incl, strict, (row == col).astype(F32)


def _gla_kernel(qf, kf, vf, cf, qb, kb, vb, cb, of, ob, st_ref):
    @pl.when(pl.program_id(0) == 0)
    def _():
        st_ref[...] = jnp.zeros_like(st_ref)

    nsub = ROW_TILE // CHUNK
    nh = GLA_HEADS
    hs = range(nh)
    bidx = lax.broadcasted_iota(jnp.int32, (2 * nh, CHUNK, CHUNK), 0)
    row = lax.broadcasted_iota(jnp.int32, (2 * nh, CHUNK, CHUNK), 1)
    col = lax.broadcasted_iota(jnp.int32, (2 * nh, CHUNK, CHUNK), 2)
    incl = jnp.where(bidx < nh, col - row, row - col) <= 0
    fwd1 = lax.broadcasted_iota(jnp.int32, (2 * nh, 1, 1), 0) < nh
    for c in range(nsub):
        slf = slice(c * CHUNK, (c + 1) * CHUNK)
        cb_ = nsub - 1 - c
        slb = slice(cb_ * CHUNK, (cb_ + 1) * CHUNK)
        heads = lambda rf, rb, w: jnp.stack([rf[slf, h * w:(h + 1) * w] for h in hs]
                                            + [rb[slb, h * w:(h + 1) * w] for h in hs])
        q, k, v, cum = (heads(qf, qb, GLA_DK), heads(kf, kb, GLA_DK), heads(vf, vb, GLA_DV),
                        heads(cf, cb, GLA_DK))
        tot = jnp.where(fwd1, cum[:, CHUNK - 1:CHUNK], cum[:, 0:1])
        qd = (q * (GLA_DK ** -0.5) * jnp.exp(cum)).astype(BF16)
        ki = (k * jnp.exp(-cum)).astype(BF16)
        ke = (k * jnp.exp(tot - cum)).astype(BF16)
        vb16 = v.astype(BF16)
        sc = jnp.where(incl, _bdot_nt(qd, ki), 0.0).astype(BF16)
        st = st_ref[...]
        o = (_bdot(sc, vb16) + _bdot_nt(qd, st.astype(BF16))).astype(of.dtype)
        for h in hs:
            of[slf, h * GLA_DV:(h + 1) * GLA_DV] = o[h]
            ob[slb, h * GLA_DV:(h + 1) * GLA_DV] = o[nh + h]
        st_ref[...] = jnp.exp(tot) * st + _bdot_tn(vb16, ke)


def _gla_call(zab, gla_cum, n_lat_tiles):
    nt = n_lat_tiles + 1
    tf = functools.partial(_tile_fwd, n_lat_tiles=n_lat_tiles)
    tb = functools.partial(_tile_bwd, n_lat_tiles=n_lat_tiles)
    qk_w, v_w = GLA_HEADS * GLA_DK, GLA_HEADS * GLA_DV

    def dir_specs(t, d):
        return [pl.BlockSpec((ROW_TILE, qk_w), lambda s: (t(s), 0)),
                pl.BlockSpec((ROW_TILE, qk_w), lambda s: (t(s), 1)),
                pl.BlockSpec((ROW_TILE, v_w), lambda s: (t(s), 1)),
                pl.BlockSpec((ROW_TILE, qk_w), lambda s: (t(s), d))]

    out_sds = jax.ShapeDtypeStruct((nt * ROW_TILE, v_w), BF16)
    return pl.pallas_call(
        _gla_kernel, grid=(nt,),
        in_specs=dir_specs(tf, 0) + dir_specs(tb, 1),
        out_specs=[pl.BlockSpec((ROW_TILE, v_w), lambda s: (tf(s), 0)),
                   pl.BlockSpec((ROW_TILE, v_w), lambda s: (tb(s), 0))],
        out_shape=[out_sds, out_sds],
        scratch_shapes=[pltpu.VMEM((2 * GLA_HEADS, GLA_DV, GLA_DK), F32)],
        compiler_params=_cparams(1), name="gla_scan",
    )(zab, zab, zab, gla_cum, zab, zab, zab, gla_cum)


INV_BASE = 8


def _unit_triangular_inverse_minus_eye(a):
    n = a.shape[-1]
    row = lax.broadcasted_iota(jnp.int32, (n, n), 0)
    col = lax.broadcasted_iota(jnp.int32, (n, n), 1)

    def same_block(b):
        s = int(math.log2(b))
        return lax.shift_right_logical(row, s) == lax.shift_right_logical(col, s)

    bdot = lambda p, r: _bdot(p.astype(BF16), r.astype(BF16))
    q = jnp.where(same_block(INV_BASE), a, 0.0)
    x = bdot(q, q)
    m = 2
    while 2 * m < INV_BASE:
        r = bdot(jnp.concatenate([q, x], axis=1), x)
        q, x = q + x + r[:, :n], r[:, n:]
        m *= 2
    q = q + x + bdot(q, x)
    b = INV_BASE
    while b < n:
        a_b = jnp.where(jnp.logical_and(same_block(2 * b), jnp.logical_not(same_block(b))), a, 0.0)
        y = a_b + bdot(a_b, q)
        q = q + y + bdot(q, y)
        b *= 2
    return q


def _gdn_kernel(xf, gf, tf_ref, xb, gb, tb_ref, of, ob, s_ref):
    @pl.when(pl.program_id(0) == 0)
    def _():
        s_ref[...] = jnp.zeros_like(s_ref)

    nsub = ROW_TILE // CHUNK
    qk_w = GDN_HEADS * GDN_DK
    nh = GDN_HEADS
    hs = range(nh)
    bidx = lax.broadcasted_iota(jnp.int32, (2 * nh, CHUNK, CHUNK), 0)
    row = lax.broadcasted_iota(jnp.int32, (2 * nh, CHUNK, CHUNK), 1)
    col = lax.broadcasted_iota(jnp.int32, (2 * nh, CHUNK, CHUNK), 2)
    ahead = jnp.where(bidx < nh, col - row, row - col)
    incl = ahead <= 0
    strict = ahead < 0
    fwd1 = lax.broadcasted_iota(jnp.int32, (2 * nh, 1, 1), 0) < nh
    for c in range(nsub):
        slf = slice(c * CHUNK, (c + 1) * CHUNK)
        cb = nsub - 1 - c
        slb = slice(cb * CHUNK, (cb + 1) * CHUNK)
        heads = lambda off: jnp.stack(
            [xf[slf, off + h * GDN_DK:off + (h + 1) * GDN_DK] for h in hs]
            + [xb[slb, off + h * GDN_DK:off + (h + 1) * GDN_DK] for h in hs])
        q, k, v = heads(0), heads(qk_w), heads(2 * qk_w)
        lanes = lambda l0: jnp.stack([gf[slf, l0 + h:l0 + h + 1] for h in hs]
                                     + [gb[slb, l0 + nh + h:l0 + nh + h + 1] for h in hs])
        cum, beta = lanes(GDN_A_LANE), lanes(GDN_B_LANE)
        cum_row = jnp.stack(
            [tf_ref[c * LANE + GDN_A_LANE + h:c * LANE + GDN_A_LANE + h + 1, :CHUNK] for h in hs]
            + [tb_ref[cb * LANE + GDN_A_LANE + nh + h:cb * LANE + GDN_A_LANE + nh + h + 1, :CHUNK] for h in hs])
        tot = jnp.where(fwd1, cum[:, CHUNK - 1:CHUNK], cum[:, 0:1])
        decay = jnp.exp(jnp.where(incl, cum - cum_row, -jnp.inf))
        k16 = k.astype(BF16)
        qk_kk = _bdot_nt(jnp.concatenate([q, k], axis=1).astype(BF16), k16)
        a = jnp.where(strict, -(qk_kk[:, CHUNK:] * decay * beta), 0.0)
        t_m1 = _unit_triangular_inverse_minus_eye(a)
        e_cum = jnp.exp(cum)
        rhs = jnp.concatenate([v * beta, k * (beta * e_cum)], axis=2)
        uw = rhs + _bdot(t_m1.astype(BF16), rhs.astype(BF16))
        u, w = uw[:, :, :GDN_DK], uw[:, :, GDN_DK:]
        sc = jnp.where(incl, qk_kk[:, :CHUNK] * decay, 0.0).astype(BF16)
        ke = (k * jnp.exp(tot - cum)).astype(BF16)
        s = s_ref[...]
        ws_qs = _bdot(jnp.concatenate([w, q * e_cum], axis=1).astype(BF16), s.astype(BF16))
        vn16 = (u - ws_qs[:, :CHUNK]).astype(BF16)
        o = (ws_qs[:, CHUNK:] + _bdot(sc, vn16)).astype(of.dtype)
        for h in hs:
            of[slf, h * GDN_DK:(h + 1) * GDN_DK] = o[h]
            ob[slb, h * GDN_DK:(h + 1) * GDN_DK] = o[nh + h]
        s_ref[...] = jnp.exp(tot) * s + _bdot_tn(ke, vn16)


def _gdn_call(qkv, gates, gates_t, n_lat_tiles):
    nt = n_lat_tiles + 1
    tf = functools.partial(_tile_fwd, n_lat_tiles=n_lat_tiles)
    tb = functools.partial(_tile_bwd, n_lat_tiles=n_lat_tiles)
    w = qkv.shape[1]
    out_w = GDN_HEADS * GDN_DK
    tl = (ROW_TILE // CHUNK) * LANE

    def dir_specs(t):
        return [pl.BlockSpec((ROW_TILE, w), lambda s: (t(s), 0)),
                pl.BlockSpec((ROW_TILE, LANE), lambda s: (t(s), 0)),
                pl.BlockSpec((tl, LANE), lambda s: (t(s), 0))]

    out_sds = jax.ShapeDtypeStruct((nt * ROW_TILE, out_w), BF16)
    return pl.pallas_call(
        _gdn_kernel, grid=(nt,),
        in_specs=dir_specs(tf) + dir_specs(tb),
        out_specs=[pl.BlockSpec((ROW_TILE, out_w), lambda s: (tf(s), 0)),
                   pl.BlockSpec((ROW_TILE, out_w), lambda s: (tb(s), 0))],
        out_shape=[out_sds, out_sds],
        scratch_shapes=[pltpu.VMEM((2 * GDN_HEADS, GDN_DK, GDN_DK), F32)],
        compiler_params=_cparams(1), name="gdn_scan",
    )(qkv, gates, gates_t, qkv, gates, gates_t)


def _merge_ab_kernel(gf, gb, df, db, gz, dz, gn, dn, o_ref):
    gla_w = GLA_HEADS * GLA_DV
    og = gf[...].astype(F32) + gb[...].astype(F32)
    zg = gz[...]
    for h in range(GLA_HEADS):
        sl = slice(h * GLA_DV, (h + 1) * GLA_DV)
        seg = og[:, sl]
        y = seg * lax.rsqrt(jnp.mean(seg * seg, axis=-1, keepdims=True) + NORM_EPS) * gn[...]
        o_ref[:, sl] = (y * _silu(zg[:, sl])).astype(BF16)
    od = df[...].astype(F32) + db[...].astype(F32)
    zd = dz[...]
    for h in range(GDN_HEADS):
        sl = slice(h * GDN_DK, (h + 1) * GDN_DK)
        seg = od[:, sl]
        y = seg * lax.rsqrt(jnp.mean(seg * seg, axis=-1, keepdims=True) + NORM_EPS) * dn[...]
        o_ref[:, gla_w + h * GDN_DK:gla_w + (h + 1) * GDN_DK] = (y * _silu(zd[:, sl])).astype(BF16)


def _out0_kernel(gf, gb, df, db, gz, dz, gn, dn, w_ref, x_ref, c_ref, m_ref, m1_ref, g1_ref,
                 o_ref, h1_ref, a_scr, *, n_lat_tiles, d):
    is_ctx = pl.program_id(0) == n_lat_tiles
    _merge_ab_kernel(gf, gb, df, db, gz, dz, gn, dn, a_scr)
    acc = _dot(a_scr[...], w_ref[...])
    m = m_ref[...]
    gate = jnp.where(is_ctx, m[1:2, 2 * d:], m[0:1, 2 * d:])
    xn = jnp.where(is_ctx, c_ref[...], x_ref[...]) + gate * acc
    o_ref[...] = xn
    m1 = m1_ref[...]
    y = xn * lax.rsqrt(jnp.mean(xn * xn, axis=-1, keepdims=True) + NORM_EPS) * g1_ref[...]
    shift = jnp.where(is_ctx, m1[1:2, :d], m1[0:1, :d])
    scale = jnp.where(is_ctx, m1[1:2, d:2 * d], m1[0:1, d:2 * d])
    h1_ref[...] = (y * (1.0 + scale) + shift).astype(BF16)


def _out0_call(gla_f, gla_b, gdn_f, gdn_b, zab, gla_norm, gdn_norm, w_out, x_lat, x_ctx, mods,
               mods_next, g_next, n_lat_tiles):
    m = gla_f.shape[0]
    w = GLA_HEADS * GLA_DV
    k, d = w_out.shape
    row = lambda i: (i, 0)
    const = lambda i: (0, 0)
    return pl.pallas_call(
        functools.partial(_out0_kernel, n_lat_tiles=n_lat_tiles, d=d), grid=(m // ROW_TILE,),
        in_specs=[pl.BlockSpec((ROW_TILE, w), row)] * 4
        + [pl.BlockSpec((ROW_TILE, w), lambda i: (i, 2)), pl.BlockSpec((ROW_TILE, w), lambda i: (i, 6)),
           pl.BlockSpec((1, GLA_DV), const), pl.BlockSpec((1, GDN_DK), const),
           pl.BlockSpec((k, d), const),
           pl.BlockSpec((ROW_TILE, d), lambda i: (jnp.minimum(i, n_lat_tiles - 1), 0)),
           pl.BlockSpec((ROW_TILE, d), const),
           pl.BlockSpec((8, 3 * d), const), pl.BlockSpec((8, 3 * d), const), pl.BlockSpec((1, d), const)],
        out_specs=[pl.BlockSpec((ROW_TILE, d), row), pl.BlockSpec((ROW_TILE, d), row)],
        out_shape=[jax.ShapeDtypeStruct((m, d), F32), jax.ShapeDtypeStruct((m, d), BF16)],
        scratch_shapes=[pltpu.VMEM((ROW_TILE, k), BF16)],
        compiler_params=_cparams(1), name="out_proj_ab",
    )(gla_f, gla_b, gdn_f, gdn_b, zab, zab, gla_norm.reshape(1, -1), gdn_norm.reshape(1, -1),
      w_out, x_lat, x_ctx, mods, mods_next, g_next.reshape(1, d))


def _lru_gates(seg, wa, ba, wx, bx, lam):
    seg16 = seg.astype(BF16)
    sigmoid = lambda t: 0.5 * jnp.tanh(0.5 * t) + 0.5
    r = sigmoid(_dot(seg16, wa) + ba)
    gi = sigmoid(_dot(seg16, wx) + bx)
    log_a = -LRU_C * _softplus(-lam) * r
    a = jnp.exp(log_a)
    return a, jnp.sqrt(jnp.tanh(-log_a) * (a * a + 1.0)) * (gi * seg)


def _lru_ctx_kernel(x_ref, cw_ref, cb_ref, wa_ref, ba_ref, wx_ref, bx_ref, lam_ref, o_ref, a_scr, d_scr):
    x = x_ref[...]
    rows, width = x.shape
    blk = width // LRU_BLOCKS
    cw = cw_ref[...]
    row = lax.broadcasted_iota(jnp.int32, x.shape, 0)
    zero = jnp.zeros((1, width), F32)
    xc = (cw[0:1] * _row_shift(x, -1, row, zero, zero, zero) + cw[1:2] * x
          + cw[2:3] * _row_shift(x, 1, row, zero, zero, zero)
          + cw[3:4] * _row_shift(x, 2, row, zero, zero, zero) + cb_ref[...])
    for d in range(2):
        for b in range(LRU_BLOCKS):
            sl = slice(b * blk, (b + 1) * blk)
            a, drv = _lru_gates(xc[:, sl], wa_ref[d, b], ba_ref[d, :, sl], wx_ref[d, b],
                                bx_ref[d, :, sl], lam_ref[d, :, sl])
            a_scr[d, :, sl] = a
            d_scr[d, :, sl] = drv

    def body(t, carry):
        hf, hb = carry
        hf = a_scr[0, pl.ds(t, 1), :] * hf + d_scr[0, pl.ds(t, 1), :]
        tb = rows - 1 - t
        hb = a_scr[1, pl.ds(tb, 1), :] * hb + d_scr[1, pl.ds(tb, 1), :]
        return hf, hb

    hf, hb = lax.fori_loop(0, rows, body, (zero, zero), unroll=8)
    o_ref[...] = jnp.zeros_like(o_ref)
    o_ref[0:1, :] = hf
    o_ref[1:2, :] = hb


def _lru_lat_kernel(xf, pf, nf, xb, pb, nb, h0_ref, cw_ref, cb_ref, wa_ref, ba_ref, wx_ref, bx_ref,
                    lam_ref, of, ob, a_scr, d_scr, hl_scr, pc_scr, carry, *, n_colblocks):
    ci, s = pl.program_id(0), pl.program_id(1)

    @pl.when(s == 0)
    def _():
        carry[...] = h0_ref[...]

    cw = cw_ref[...]
    dirs = ((xf, pf, nf, of, s), (xb, pb, nb, ob, n_colblocks - 1 - s))
    for d, (x_ref, p_ref, n_ref, o_ref, cblk) in enumerate(dirs):
        forward = d == 0
        x3 = x_ref[...]
        rows, ncol, tc = x3.shape
        has_prev, has_next = cblk > 0, cblk < n_colblocks - 1
        sub = lax.broadcasted_iota(jnp.int32, (ncol, tc), 0)
        prev_last = jnp.where(has_prev, p_ref[7, ncol - 1:ncol, :], 0.0)
        next_r0 = jnp.where(has_next, n_ref[0, 0:1, :], 0.0)
        next_r1 = jnp.where(has_next, n_ref[1, 0:1, :], 0.0)
        first_m1 = jnp.where(sub == 0, prev_last, pltpu.roll(x3[rows - 1], 1, 0))
        last_p1 = jnp.where(sub == ncol - 1, next_r0, pltpu.roll(x3[0], ncol - 1, 0))
        last_p2 = jnp.where(sub == ncol - 1, next_r1, pltpu.roll(x3[1], ncol - 1, 0))
        xm1 = jnp.concatenate([first_m1[None], x3[:rows - 1]], axis=0)
        xp1 = jnp.concatenate([x3[1:], last_p1[None]], axis=0)
        xp2 = jnp.concatenate([x3[2:], last_p1[None], last_p2[None]], axis=0)
        xc = cw[0:1] * xm1 + cw[1:2] * x3 + cw[2:3] * xp1 + cw[3:4] * xp2 + cb_ref[...]
        a, drv = _lru_gates(xc.reshape(rows * ncol, tc), wa_ref[d, ci], ba_ref[d], wx_ref[d, ci],
                            bx_ref[d], lam_ref[d])
        a_scr[d] = a.reshape(rows, ncol, tc)
        d_scr[d] = drv.reshape(rows, ncol, tc)

    def body(i, carry_):
        out = []
        for d, r in ((0, i), (1, rows - 1 - i)):
            h, p = carry_[2 * d], carry_[2 * d + 1]
            ar = a_scr[d, r]
            h = ar * h + d_scr[d, r]
            p = ar * p
            hl_scr[d, r] = h
            pc_scr[d, r] = p
            out += [h, p]
        return tuple(out)

    zero, one = jnp.zeros((ncol, tc), F32), jnp.ones((ncol, tc), F32)
    lax.fori_loop(0, rows, body, (zero, one, zero, one), unroll=8)

    for d, (x_ref, p_ref, n_ref, o_ref, cblk) in enumerate(dirs):
        forward = d == 0
        end = rows - 1 if forward else 0
        hl_end, pc_end = hl_scr[d, end], pc_scr[d, end]
        h = carry[d:d + 1, :]
        h_in = jnp.zeros((ncol, tc), F32)
        for j in (range(ncol) if forward else reversed(range(ncol))):
            h_in = jnp.where(sub == j, h, h_in)
            h = hl_end[j:j + 1, :] + pc_end[j:j + 1, :] * h
        carry[d:d + 1, :] = h
        o_ref[...] = hl_scr[d] + pc_scr[d] * h_in[None]


def _lru_call(zcd, conv_w, conv_b, wa, ba, wx, bx, lam, n_lat):
    m, ncols_z = zcd.shape
    width = conv_w.shape[1]
    n_cols = GRID_W
    rows = n_lat // n_cols
    tc = width // LRU_BLOCKS
    ncb = n_cols // 8
    full = lambda a: pl.BlockSpec(a.shape, lambda *_: (0,) * a.ndim)
    wa16, wx16 = wa.astype(BF16), wx.astype(BF16)
    cb2, ba3, bx3, lam3 = (conv_b.reshape(1, width), ba.reshape(2, 1, width), bx.reshape(2, 1, width),
                           lam.reshape(2, 1, width))

    params = (conv_w, cb2, wa16, ba3, wx16, bx3, lam3)
    h0 = pl.pallas_call(
        _lru_ctx_kernel, grid=(1,),
        in_specs=[pl.BlockSpec((ROW_TILE, width), lambda i: (n_lat // ROW_TILE, 0))] + [full(p) for p in params],
        out_specs=pl.BlockSpec((8, width), lambda i: (0, 0)),
        out_shape=jax.ShapeDtypeStruct((8, width), F32),
        scratch_shapes=[pltpu.VMEM((2, ROW_TILE, width), F32)] * 2,
        compiler_params=_cparams(1), name="rglru_ctx",
    )(zcd, *params)

    z3 = zcd.reshape(m // n_cols, n_cols, ncols_z)
    hb8 = rows // 8
    bf = lambda s: s
    bb = lambda s: ncb - 1 - s

    def dir_specs(blk):
        return [pl.BlockSpec((rows, 8, tc), lambda c, s: (0, blk(s), c)),
                pl.BlockSpec((8, 8, tc), lambda c, s: (hb8 - 1, jnp.maximum(blk(s) - 1, 0), c)),
                pl.BlockSpec((8, 8, tc), lambda c, s: (0, jnp.minimum(blk(s) + 1, ncb - 1), c))]

    vec = lambda lead: pl.BlockSpec((lead, 1, tc), lambda c, s: (0, 0, c))
    out_sds = jax.ShapeDtypeStruct((rows, n_cols, width), F32)
    hf, hb = pl.pallas_call(
        functools.partial(_lru_lat_kernel, n_colblocks=ncb), grid=(width // tc, ncb),
        in_specs=dir_specs(bf) + dir_specs(bb)
        + [pl.BlockSpec((8, tc), lambda c, s: (0, c)),
           pl.BlockSpec((4, tc), lambda c, s: (0, c)), pl.BlockSpec((1, tc), lambda c, s: (0, c)),
           full(wa16), vec(2), full(wx16), vec(2), vec(2)],
        out_specs=[pl.BlockSpec((rows, 8, tc), lambda c, s: (0, bf(s), c)),
                   pl.BlockSpec((rows, 8, tc), lambda c, s: (0, bb(s), c))],
        out_shape=[out_sds, out_sds],
        scratch_shapes=[pltpu.VMEM((2, rows, 8, tc), F32)] * 4 + [pltpu.VMEM((8, tc), F32)],
        compiler_params=_cparams(2), name="rglru_scan",
    )(z3, z3, z3, z3, z3, z3, h0, conv_w, cb2, wa16, ba3, wx16, bx3, lam3)
    return hf.reshape(n_lat, width), hb.reshape(n_lat, width)


def _hy_prep_kernel(*refs, n_lat_tiles):
    xs, (w_ref, b_ref, x0_ref, z_ref) = refs[:9], refs[9:]
    i = pl.program_id(0)
    has_prev = i > 0
    has_next = i < n_lat_tiles - 1
    outs = []
    width = x0_ref.shape[1]
    for j in range(3):
        x_ref, p_ref, n_ref = xs[3 * j:3 * j + 3]
        x = x_ref[...]
        row = lax.broadcasted_iota(jnp.int32, x.shape, 0)
        prev_row = jnp.where(has_prev, p_ref[7:8, :], 0.0)
        next0 = jnp.where(has_next, n_ref[0:1, :], 0.0)
        sl = slice(j * width, (j + 1) * width)
        outs.append(w_ref[0:1, sl] * _row_shift(x, -1, row, prev_row, next0, next0)
                    + w_ref[1:2, sl] * x
                    + w_ref[2:3, sl] * _row_shift(x, 1, row, prev_row, next0, next0)
                    + b_ref[:, sl])
    x0_ref[...] = outs[0]
    z_ref[...] = outs[1] * outs[2]


def _hy_prep_call(zcd, conv_w, conv_b, n_lat_tiles):
    width = conv_w.shape[1] // 3
    tm = _row_tile(n_lat_tiles, 2)
    n_tiles = n_lat_tiles * ROW_TILE // tm
    hb = tm // 8
    last8 = n_tiles * hb - 1
    in_specs, args = [], []
    for j in range(3):
        cbk = 2 + j
        in_specs += [pl.BlockSpec((tm, width), lambda i, cbk=cbk: (i, cbk)),
                     pl.BlockSpec((8, width), lambda i, cbk=cbk: (jnp.maximum(i * hb - 1, 0), cbk)),
                     pl.BlockSpec((8, width), lambda i, cbk=cbk: (jnp.minimum((i + 1) * hb, last8), cbk))]
        args += [zcd, zcd, zcd]
    in_specs += [pl.BlockSpec((3, 3 * width), lambda i: (0, 0)), pl.BlockSpec((1, 3 * width), lambda i: (0, 0))]
    out_sds = jax.ShapeDtypeStruct((n_tiles * tm, width), F32)
    return pl.pallas_call(
        functools.partial(_hy_prep_kernel, n_lat_tiles=n_tiles), grid=(n_tiles,),
        in_specs=in_specs,
        out_specs=[pl.BlockSpec((tm, width), lambda i: (i, 0))] * 2,
        out_shape=[out_sds, out_sds],
        compiler_params=_cparams(1), name="hyena_prep",
    )(*args, conv_w, conv_b.reshape(1, -1))


def _hy_filter_kernel(e_ref, w1, b1, w2, b2, w3, fr, dec, f_ref, ssq_ref):
    i = pl.program_id(0)
    e = e_ref[...]
    hid = jnp.sin(fr[...] * (_dot(e, w1[...], HI) + b1[...]))
    hid = jnp.sin(fr[...] * (_dot(hid, w2[...], HI) + b2[...]))
    filt = _dot(hid.astype(BF16), w3[...]) * jnp.exp(-e[:, 0:1] * dec[...])
    half = filt.shape[1] // 2
    row = lax.broadcasted_iota(jnp.int32, filt.shape, 0)
    col = lax.broadcasted_iota(jnp.int32, filt.shape, 1)
    unused = jnp.logical_and(jnp.logical_and(i == 0, row == 0), col >= half)
    filt = jnp.where(unused, 0.0, filt)
    f_ref[...] = filt

    @pl.when(i == 0)
    def _():
        ssq_ref[...] = jnp.zeros_like(ssq_ref)

    ssq_ref[...] = ssq_ref[...] + jnp.sum(filt * filt, axis=0, keepdims=True)


def _hy_filter_call(emb, w1, b1, w2, b2, w3, freq, decay):
    length = emb.shape[0]
    n_out = w3.shape[1]
    hid = w1.shape[1]
    pad_r = lambda a: jnp.pad(a, ((0, LANE - a.shape[0]), (0, 0)))
    pad_c = lambda a: jnp.pad(a, ((0, 0), (0, LANE - a.shape[1])))
    params = (pad_c(pad_r(w1)), pad_c(b1.reshape(1, hid)), pad_c(pad_r(w2)), pad_c(b2.reshape(1, hid)),
              pad_r(w3).astype(BF16), pad_c(freq.reshape(1, hid)), decay.reshape(1, n_out))
    full = lambda a: pl.BlockSpec(a.shape, lambda i: (0,) * a.ndim)
    return pl.pallas_call(
        _hy_filter_kernel, grid=(length // ROW_TILE,),
        in_specs=[pl.BlockSpec((ROW_TILE, LANE), lambda i: (i, 0))] + [full(p) for p in params],
        out_specs=[pl.BlockSpec((ROW_TILE, n_out), lambda i: (i, 0)),
                   pl.BlockSpec((8, n_out), lambda i: (0, 0))],
        out_shape=[jax.ShapeDtypeStruct((length, n_out), F32), jax.ShapeDtypeStruct((8, n_out), F32)],
        compiler_params=_cparams(1), name="hyena_filter",
    )(emb, *params)


SUB = 16


def _dft_a_kernel(x_ref, c_ref, s_ref, re_ref, im_ref):
    xt = pltpu.einshape("kjc->jkc", x_ref[...])
    res_re, res_im = [], []
    for j in range(SUB):
        xb = xt[j].astype(BF16)
        res_re.append(_dot(c_ref[...], xb))
        res_im.append(_dot(s_ref[...], xb))
    re_ref[...] = pltpu.einshape("jkc->kjc", jnp.stack(res_re)).astype(BF16)
    im_ref[...] = pltpu.einshape("jkc->kjc", jnp.stack(res_im)).astype(BF16)


def _dft_a_call(x, cmat, smat, n1):
    length, ch = x.shape
    n2, n2h = cmat.shape
    tc = 512
    x4 = x.reshape(n2h, n1 // SUB, SUB, ch)
    out_sds = jax.ShapeDtypeStruct((n2, n1 // SUB, SUB, ch), BF16)
    re, im = pl.pallas_call(
        _dft_a_kernel, grid=(ch // tc, n1 // SUB),
        in_specs=[pl.BlockSpec((n2h, None, SUB, tc), lambda c, i: (0, i, 0, c)),
                  pl.BlockSpec((n2, n2h), lambda c, i: (0, 0)),
                  pl.BlockSpec((n2, n2h), lambda c, i: (0, 0))],
        out_specs=[pl.BlockSpec((n2, None, SUB, tc), lambda c, i: (0, i, 0, c))] * 2,
        out_shape=[out_sds, out_sds],
        compiler_params=_cparams(2), name="dft_stage_a",
    )(x4, cmat, smat)
    return re.reshape(n2, n1, ch), im.reshape(n2, n1, ch)


def _dft_b_kernel(are, aim, pre, pim, fre, fim, g_ref, gi_ref, sp_ref, sf_ref, bre_ref, bim_ref,
                  *, n1, inv_n):
    scale = lax.rsqrt(sp_ref[0:1, :] + sf_ref[0:1, :] + NORM_EPS) * inv_n
    res_re, res_im = [], []
    for j in range(SUB):
        g = g_ref[j]
        cat = lambda r, i: jnp.concatenate([r[j], i[j]], axis=0)
        xp, xf, x = _dot(g, cat(pre, pim)), _dot(g, cat(fre, fim)), _dot(g, cat(are, aim))
        hr = (xp[:n1] + xf[:n1]) * scale
        hi = (xp[n1:] - xf[n1:]) * scale
        xr, xi = x[:n1], x[n1:]
        y = jnp.concatenate([xr * hr - xi * hi, xr * hi + xi * hr], axis=0).astype(BF16)
        b = _dot(gi_ref[j], y)
        res_re.append(b[:n1])
        res_im.append(b[n1:])
    bre_ref[...] = pltpu.einshape("jkc->kjc", jnp.stack(res_re)).astype(BF16)
    bim_ref[...] = pltpu.einshape("jkc->kjc", jnp.stack(res_im)).astype(BF16)


def _dft_b_call(a_re, a_im, f_re, f_im, gmat, gimat, ssq):
    n2, n1, ch = a_re.shape
    tc = 256
    nct = ch // tc
    slab = lambda half: pl.BlockSpec((SUB, n1, tc), lambda k, c: (k, 0, half * nct + c))
    gspec = pl.BlockSpec((SUB, 2 * n1, 2 * n1), lambda k, c: (k, 0, 0))
    out_sds = jax.ShapeDtypeStruct((n1, n2 // SUB, SUB, ch), BF16)
    re, im = pl.pallas_call(
        functools.partial(_dft_b_kernel, n1=n1, inv_n=1.0 / (n1 * n2)), grid=(n2 // SUB, nct),
        in_specs=[slab(0), slab(0), slab(0), slab(0), slab(1), slab(1), gspec, gspec,
                  pl.BlockSpec((8, tc), lambda k, c: (0, c)),
                  pl.BlockSpec((8, tc), lambda k, c: (0, nct + c))],
        out_specs=[pl.BlockSpec((n1, None, SUB, tc), lambda k, c: (0, k, 0, c))] * 2,
        out_shape=[out_sds, out_sds],
        compiler_params=_cparams(2), name="dft_stage_b",
    )(a_re, a_im, f_re, f_im, f_re, f_im, gmat, gimat, ssq, ssq)
    return re.reshape(n1, n2, ch), im.reshape(n1, n2, ch)


def _dft_c_kernel(bre, bim, x0_ref, z_ref, hz_ref, bias_ref, c_ref, s_ref, o_ref):
    ys = [_dot(c_ref[...], bre[j].astype(BF16)) + _dot(s_ref[...], bim[j].astype(BF16)) for j in range(SUB)]
    y = pltpu.einshape("jkc->kjc", jnp.stack(ys))
    o_ref[...] = x0_ref[...] * (y + z_ref[...] * bias_ref[...]) * _silu(hz_ref[...])


def _dft_c_call(b_re, b_im, x0, z, zcd, bias, cmat, smat):
    n1, n2, ch = b_re.shape
    n2h = n2
    length = n1 * n2h
    tc = 256
    nct = ch // tc
    view = lambda a: a.reshape(a.shape[0] // n1, n1 // SUB, SUB, a.shape[1])
    tspec = lambda cblk: pl.BlockSpec((n2h, None, SUB, tc), lambda i, c: (0, i, 0, cblk * nct + c))
    out = pl.pallas_call(
        _dft_c_kernel, grid=(n1 // SUB, nct),
        in_specs=[pl.BlockSpec((SUB, n2, tc), lambda i, c: (i, 0, c))] * 2
        + [tspec(0), tspec(0), tspec(5), pl.BlockSpec((1, tc), lambda i, c: (0, c)),
           pl.BlockSpec((n2h, n2), lambda i, c: (0, 0)), pl.BlockSpec((n2h, n2), lambda i, c: (0, 0))],
        out_specs=tspec(0),
        out_shape=jax.ShapeDtypeStruct((n2h, n1 // SUB, SUB, ch), F32),
        compiler_params=_cparams(2), name="dft_stage_c",
    )(b_re, b_im, view(x0), view(z), view(zcd), bias.reshape(1, ch), cmat, smat)
    return out.reshape(length, ch)


def _dft_constants(n1, n2):
    n = n1 * n2
    n2h = n2 // 2
    k2 = jnp.arange(n2h, dtype=jnp.int32)
    ang_a = (math.pi / n2) * ((k2[None, :] * (2 * k2[:, None] + 1)) % (2 * n2)).astype(F32)
    ca, sa = jnp.cos(ang_a), -jnp.sin(ang_a)
    cc, sc = jnp.cos(ang_a).T, -jnp.sin(ang_a).T
    i1 = jnp.arange(n1, dtype=jnp.int32)
    ang_f = (2.0 * math.pi / n1) * ((i1[:, None] * i1[None, :]) % n1).astype(F32)
    ang_t = (math.pi / n) * ((2 * k2[:, None] + 1) * i1[None, :]).astype(F32)
    fr, fi = jnp.cos(ang_f)[None], -jnp.sin(ang_f)[None]
    wr, wi = jnp.cos(ang_t)[:, None, :], -jnp.sin(ang_t)[:, None, :]
    gr, gi = fr * wr - fi * wi, fr * wi + fi * wr
    g = jnp.concatenate([jnp.concatenate([gr, -gi], axis=2), jnp.concatenate([gi, gr], axis=2)], axis=1)
    tr, ti = jnp.swapaxes(gr, 1, 2), -jnp.swapaxes(gi, 1, 2)
    ginv = jnp.concatenate([jnp.concatenate([tr, -ti], axis=2), jnp.concatenate([ti, tr], axis=2)], axis=1)
    b = lambda a: a.astype(BF16)
    return b(ca), b(sa), b(cc), b(sc), b(g), b(ginv)


def _final_kernel(hf, hb, lz, hy, w1, w2, x_ref, m_ref, g_ref, o_ref, *, d):
    a1 = ((hf[...] + hb[...]) * _silu(lz[...])).astype(BF16)
    acc = _dot(a1, w1[...]) + _dot(hy[...].astype(BF16), w2[...])
    xn = x_ref[...] + m_ref[0:1, 2 * d:] * acc
    o_ref[...] = xn * lax.rsqrt(jnp.mean(xn * xn, axis=-1, keepdims=True) + NORM_EPS) * g_ref[...]


def _final_call(hf, hb, zcd, hy, w_out, x1, mods, final_g, n_lat_tiles):
    d = w_out.shape[1]
    half = w_out.shape[0] // 2
    row = lambda i: (i, 0)
    const = lambda i: (0, 0)
    tm = _row_tile(n_lat_tiles, 2)
    once = dict(pipeline_mode=pl.Buffered(1))
    return pl.pallas_call(
        functools.partial(_final_kernel, d=d), grid=(n_lat_tiles * ROW_TILE // tm,),
        in_specs=[pl.BlockSpec((tm, half), row), pl.BlockSpec((tm, half), row),
                  pl.BlockSpec((tm, half), lambda i: (i, 1)), pl.BlockSpec((tm, half), row),
                  pl.BlockSpec((half, d), const, **once), pl.BlockSpec((half, d), lambda i: (1, 0), **once),
                  pl.BlockSpec((tm, d), row), pl.BlockSpec((8, 3 * d), const),
                  pl.BlockSpec((1, d), const)],
        out_specs=pl.BlockSpec((tm, d), row),
        out_shape=jax.ShapeDtypeStruct((n_lat_tiles * ROW_TILE, d), F32),
        compiler_params=_cparams(1), name="out_proj_cd_final",
    )(hf, hb, zcd, hy, w_out, w_out, x1, mods, final_g.reshape(1, d))


def kernel(x, c, ctx, c_ctx, mod_w, mod_b, norm_g, ab_w_in, ab_gla_wg2, ab_gla_bg2, ab_gla_norm, ab_gdn_conv, ab_gdn_a_log, ab_gdn_dt_bias, ab_gdn_norm, ab_w_out, cd_w_in, cd_lru_conv_w, cd_lru_conv_b, cd_lru_wa, cd_lru_ba, cd_lru_wx, cd_lru_bx, cd_lru_lambda, cd_hy_conv_w, cd_hy_conv_b, cd_hy_w1, cd_hy_b1, cd_hy_w2, cd_hy_b2, cd_hy_w3, cd_hy_freq, cd_hy_decay, cd_hy_bias, cd_w_out, final_g):
    _, n_lat, d = x.shape
    assert x.shape[0] == 1 and ctx.shape[1] == ROW_TILE and n_lat % ROW_TILE == 0
    assert mod_w.shape[0] == 2 and d == 2 * GLA_HEADS * GLA_DV
    nlt = n_lat // ROW_TILE
    x_lat, x_ctx = x[0], ctx[0]

    cvec = jnp.zeros((8, d), F32).at[0].set(c[0]).at[1].set(c_ctx)
    mods = _mod_call(cvec, mod_w, mod_b)

    gla_qk, gla_v = GLA_HEADS * GLA_DK, GLA_HEADS * GLA_DV
    gdn_qk = GDN_HEADS * GDN_DK
    sizes = (gla_qk, gla_qk, gla_v, 2 * GLA_RANK, gla_v, 3 * gdn_qk, 2 * GDN_HEADS, 2 * GDN_HEADS, gdn_qk)
    gq, gk, gv, glr, gz, dqkv, da, db, dz = jnp.split(ab_w_in[0], np.cumsum(sizes)[:-1].tolist(), axis=1)
    w_main = jnp.concatenate([gq, gk, gv, gz, dqkv, dz], axis=1).astype(BF16)
    n_small = glr.shape[1] + da.shape[1] + db.shape[1]
    w_small = jnp.concatenate([glr, da, db, jnp.zeros((d, LANE - n_small), F32)], axis=1).astype(BF16)

    h0, zsm = _norm_mod_call(x_lat, x_ctx, mods[0], norm_g[0], w_small, nlt)
    zab = _mm_call(h0, w_main, 1024, "in_proj_ab")

    gw = GLA_HEADS * GLA_DK
    wg = jnp.zeros((LANE, 2 * gw), F32)
    for dd in range(2):
        wg = wg.at[dd * GLA_RANK:(dd + 1) * GLA_RANK, dd * gw:(dd + 1) * gw].set(ab_gla_wg2[0, dd])
    gbias = ab_gla_bg2[0].reshape(1, 2 * gw)
    gparams = jnp.zeros((8, LANE), F32)
    gparams = gparams.at[0, GDN_A_LANE:GDN_B_LANE].set(ab_gdn_a_log[0].reshape(-1))
    gparams = gparams.at[1, GDN_A_LANE:GDN_B_LANE].set(ab_gdn_dt_bias[0].reshape(-1))
    qkv, gla_cum, gates, gates_t = _ab_prep_call(zab, zsm, ab_gdn_conv[0], wg.astype(BF16), gbias, gparams, nlt)
    gla_f, gla_b = _gla_call(zab, gla_cum, nlt)
    gdn_f, gdn_b = _gdn_call(qkv, gates, gates_t, nlt)

    x1, h1 = _out0_call(gla_f, gla_b, gdn_f, gdn_b, zab, ab_gla_norm[0], ab_gdn_norm[0],
                        ab_w_out[0].astype(BF16), x_lat, x_ctx, mods[0], mods[1], norm_g[1], nlt)

    zcd = _mm_call(h1, cd_w_in[0].astype(BF16), 1024, "in_proj_cd")

    lru_f, lru_b = _lru_call(zcd, cd_lru_conv_w[0], cd_lru_conv_b[0], cd_lru_wa[0], cd_lru_ba[0],
                             cd_lru_wx[0], cd_lru_bx[0], cd_lru_lambda[0], n_lat)

    x0, z = _hy_prep_call(zcd, cd_hy_conv_w[0], cd_hy_conv_b[0], nlt)

    pos = jnp.arange(n_lat, dtype=F32)
    t01 = pos / max(n_lat - 1, 1)
    bands = jnp.linspace(1e-4, HY_BANDS - 1, HY_BANDS, dtype=F32)
    ang = (2.0 * math.pi / n_lat) * pos[:, None] * bands[None, :]
    emb = jnp.concatenate([t01[:, None], jnp.cos(ang), jnp.sin(ang)], axis=-1)
    emb = jnp.pad(emb, ((0, 0), (0, LANE - emb.shape[1])))
    filt, ssq = _hy_filter_call(emb, cd_hy_w1[0], cd_hy_b1[0], cd_hy_w2[0], cd_hy_b2[0], cd_hy_w3[0],
                                cd_hy_freq[0], cd_hy_decay[0])

    n1 = LANE if n_lat >= LANE * LANE else 32
    n2 = 2 * n_lat // n1
    ca, sa, cc, sc, gmat, gimat = _dft_constants(n1, n2)
    fa_re, fa_im = _dft_a_call(filt, ca, sa, n1)
    za_re, za_im = _dft_a_call(z, ca, sa, n1)
    b_re, b_im = _dft_b_call(za_re, za_im, fa_re, fa_im, gmat, gimat, ssq)
    y_hy = _dft_c_call(b_re, b_im, x0, z, zcd, cd_hy_bias[0], cc, sc)

    out = _final_call(lru_f, lru_b, zcd, y_hy, cd_w_out[0].astype(BF16), x1, mods[1], final_g, nlt)
    return out[None]
```
